```python
import jax, jax.numpy as jnp
from jax import lax
import numpy as np

D_MODEL = 1024
BATCH = 8
SEQ = 2048
DEPTH = 2

CHUNK = 64
GROUP_W = 512
N_GROUPS = 3
D_MIX = N_GROUPS * GROUP_W
H_A = 4
DH_A = GROUP_W // H_A
CONV_K = 4
POOL_WINDOWS = (2, 4, 8, 16)
N_POOL = len(POOL_WINDOWS)
DC_B = GROUP_W // N_POOL
H_C = 4
DH_C = GROUP_W // H_C
N_MEM = 256
EPS = 1e-6

_SIZES = (2 * GROUP_W, GROUP_W, GROUP_W, GROUP_W, 2 * H_A, GROUP_W, GROUP_W, GROUP_W, GROUP_W)
N_IN = sum(_SIZES)
_OFFS = tuple(int(v) for v in np.cumsum(_SIZES)[:-1])

kernel_name = "hybrid_mlstm_pool_memattn_block"


def rmsnorm(x, g):
    xf = x.astype(jnp.float32)
    y = xf * lax.rsqrt(jnp.mean(xf * xf, axis=-1, keepdims=True) + EPS)
    return (y * g.astype(jnp.float32)).astype(x.dtype)


def causal_dwconv(x, w, b):
    S = x.shape[1]
    xp = jnp.pad(x, ((0, 0), (CONV_K - 1, 0), (0, 0)))
    return b + sum(xp[:, j:j + S] * w[j] for j in range(CONV_K))


def mlstm_chunkwise(q, k, v, ig, logf):
    B, S, H, D = q.shape
    NC, L = S // CHUNK, CHUNK
    to_c = lambda a: a.reshape(B, NC, L, H, D).transpose(0, 3, 1, 2, 4)
    q, k, v = to_c(q), to_c(k), to_c(v)
    ig = ig.reshape(B, NC, L, H).transpose(0, 3, 1, 2)
    logf = logf.reshape(B, NC, L, H).transpose(0, 3, 1, 2)
    b = jnp.cumsum(logf, axis=-1)
    bL = b[..., -1]
    g = bL[..., None] - b + ig
    gmax = jnp.max(g, axis=-1)
    w = jnp.exp(g - gmax[..., None])
    kv_loc = jnp.einsum('bhcld,bhcle->bhcde', k * w[..., None], v)
    n_loc = jnp.einsum('bhcld,bhcl->bhcd', k, w)

    def step(carry, inp):
        C, n, m = carry
        kv_c, n_c, bL_c, gmax_c = inp
        m_new = jnp.maximum(bL_c + m, gmax_c)
        a = jnp.exp(bL_c + m - m_new)
        bc = jnp.exp(gmax_c - m_new)
        C_new = a[..., None, None] * C + bc[..., None, None] * kv_c
        n_new = a[..., None] * n + bc[..., None] * n_c
        return (C_new, n_new, m_new), (C, n, m)

    init = (jnp.zeros((B, H, D, D), jnp.float32), jnp.zeros((B, H, D), jnp.float32),
            jnp.zeros((B, H), jnp.float32))
    mv = lambda a: jnp.moveaxis(a, 2, 0)
    _, (C_prev, n_prev, m_prev) = lax.scan(step, init, (mv(kv_loc), mv(n_loc), mv(bL), mv(gmax)))
    C_prev = jnp.moveaxis(C_prev, 0, 2)
    n_prev = jnp.moveaxis(n_prev, 0, 2)
    m_prev = jnp.moveaxis(m_prev, 0, 2)

    causal = jnp.tril(jnp.ones((L, L), dtype=bool))
    dmat = jnp.where(causal, b[..., :, None] - b[..., None, :] + ig[..., None, :], -jnp.inf)
    inter = b + m_prev[..., None]
    m_t = jnp.maximum(inter, jnp.max(dmat, axis=-1))
    s = jnp.einsum('bhcld,bhcsd->bhcls', q, k) * jnp.exp(dmat - m_t[..., None])
    a_t = jnp.exp(inter - m_t)
    num = a_t[..., None] * jnp.einsum('bhcld,bhcde->bhcle', q, C_prev) + jnp.einsum('bhcls,bhcse->bhcle', s, v)
    den = a_t * jnp.einsum('bhcld,bhcd->bhcl', q, n_prev) + jnp.sum(s, axis=-1)
    h = num / jnp.maximum(jnp.abs(den), jnp.exp(-m_t))[..., None]
    return h.transpose(0, 2, 3, 1, 4).reshape(B, S, H, D)


def mlstm_branch(qk_raw, v, o, ig, fg, conv_w, conv_b, norm_g):
    B, S, _ = v.shape
    f32 = jnp.float32
    qk = jax.nn.silu(causal_dwconv(qk_raw, conv_w, conv_b))
    q, k = jnp.split(qk, 2, axis=-1)
    q = q.astype(f32).reshape(B, S, H_A, DH_A)
    k = k.astype(f32).reshape(B, S, H_A, DH_A) * (DH_A ** -0.5)
    vh = v.astype(f32).reshape(B, S, H_A, DH_A)
    h = mlstm_chunkwise(q, k, vh, ig.astype(f32), jax.nn.log_sigmoid(fg.astype(f32)))
    h = jax.nn.sigmoid(o.astype(f32)).reshape(B, S, H_A, DH_A) * h
    h = h * lax.rsqrt(jnp.mean(h * h, axis=-1, keepdims=True) + EPS)
    h = h * norm_g.astype(f32).reshape(H_A, DH_A)
    return h.reshape(B, S, GROUP_W).astype(v.dtype)


def pool_branch(u, w, scale):
    B, S, _ = u.shape
    uf = u.astype(jnp.float32).reshape(B, S, N_POOL, DC_B)
    cs = jnp.cumsum(uf, axis=1)
    t1 = jnp.arange(1, S + 1, dtype=jnp.float32)
    pooled = []
    for gi, win in enumerate(POOL_WINDOWS):
        c = cs[:, :, gi]
        lo = jnp.pad(c, ((0, 0), (win, 0), (0, 0)))[:, :S]
        cnt = jnp.minimum(t1, float(win))[None, :, None]
        pooled.append((c - lo) / cnt)
    p = (jnp.stack(pooled, axis=2) - uf).astype(u.dtype)
    y = jnp.einsum('bsgc,gcd->bsgd', p, w).reshape(B, S, GROUP_W)
    return y * scale


def mem_attn_branch(qc, mem, mem_norm_g, w_kv):
    B, S, _ = qc.shape
    mn = rmsnorm(mem, mem_norm_g)
    kv = mn @ w_kv
    km, vm = jnp.split(kv, 2, axis=-1)
    km = km.reshape(B, N_MEM, H_C, DH_C)
    vm = vm.reshape(B, N_MEM, H_C, DH_C)
    q = qc.reshape(B, S, H_C, DH_C)
    s = jnp.einsum('bshd,bmhd->bhsm', q, km).astype(jnp.float32) * (DH_C ** -0.5)
    p = jax.nn.softmax(s, axis=-1).astype(vm.dtype)
    return jnp.einsum('bhsm,bmhd->bshd', p, vm).reshape(B, S, GROUP_W)


def setup_inputs(seed: int = 0) -> dict:
    key = jax.random.key(seed)
    ks = jax.random.split(key, 16)
    nrm = jax.random.normal
    f32 = jnp.float32
    x = nrm(ks[0], (BATCH, SEQ, D_MODEL), f32)
    mem = nrm(ks[1], (BATCH, N_MEM, D_MODEL), f32)
    norm_g = 1.0 + 0.1 * nrm(ks[2], (DEPTH, D_MODEL), f32)
    w_in = nrm(ks[3], (DEPTH, D_MODEL, N_IN), f32) * D_MODEL ** -0.5
    b_i = 0.1 * nrm(ks[4], (DEPTH, H_A), f32)
    b_f = jnp.linspace(3.0, 6.0, H_A, dtype=f32)[None, :] + 0.1 * nrm(ks[5], (DEPTH, H_A), f32)
    b_gates = jnp.concatenate([b_i, b_f], axis=-1)
    conv_w = nrm(ks[6], (DEPTH, CONV_K, 2 * GROUP_W), f32) * CONV_K ** -0.5
    conv_b = 0.02 * nrm(ks[7], (DEPTH, 2 * GROUP_W), f32)
    mlstm_norm_g = 1.0 + 0.1 * nrm(ks[8], (DEPTH, GROUP_W), f32)
    pool_w = nrm(ks[9], (DEPTH, N_POOL, DC_B, DC_B), f32) * DC_B ** -0.5
    pool_scale = 1.0 + 0.1 * nrm(ks[10], (DEPTH, GROUP_W), f32)
    mem_norm_g = 1.0 + 0.1 * nrm(ks[11], (DEPTH, D_MODEL), f32)
    w_mem_kv = nrm(ks[12], (DEPTH, D_MODEL, 2 * GROUP_W), f32) * D_MODEL ** -0.5
    w_out = nrm(ks[13], (DEPTH, D_MIX, D_MODEL), f32) * D_MIX ** -0.5
    final_norm_g = 1.0 + 0.1 * nrm(ks[14], (D_MODEL,), f32)
    return {"x": x, "mem": mem, "norm_g": norm_g, "w_in": w_in, "b_gates": b_gates,
            "conv_w": conv_w, "conv_b": conv_b, "mlstm_norm_g": mlstm_norm_g,
            "pool_w": pool_w, "pool_scale": pool_scale, "mem_norm_g": mem_norm_g,
            "w_mem_kv": w_mem_kv, "w_out": w_out, "final_norm_g": final_norm_g}


def reference(x, mem, norm_g, w_in, b_gates, conv_w, conv_b, mlstm_norm_g, pool_w, pool_scale,
              mem_norm_g, w_mem_kv, w_out, final_norm_g):
    for l in range(DEPTH):
        h = rmsnorm(x, norm_g[l])
        proj = h @ w_in[l]
        qk_raw, v, o, z_a, gates, u, z_b, qc, z_c = jnp.split(proj, _OFFS, axis=-1)
        gates = gates + b_gates[l]
        ig, fg = gates[..., :H_A], gates[..., H_A:]
        y_a = mlstm_branch(qk_raw, v, o, ig, fg, conv_w[l], conv_b[l], mlstm_norm_g[l]) * jax.nn.silu(z_a)
        y_b = pool_branch(u, pool_w[l], pool_scale[l]) * jax.nn.silu(z_b)
        y_c = mem_attn_branch(qc, mem, mem_norm_g[l], w_mem_kv[l]) * jax.nn.silu(z_c)
        x = x + jnp.concatenate([y_a, y_b, y_c], axis=-1) @ w_out[l]
    return rmsnorm(x, final_norm_g)
```

```python
import functools

import jax
import jax.numpy as jnp
from jax import lax
from jax.experimental import pallas as pl
from jax.experimental.pallas import tpu as pltpu

D_MODEL = 1024
GROUP_W = 512
N_HEADS = 4
HEAD_D = 128
CONV_K = 4
POOL_WINDOWS = (2, 4, 8, 16)
N_MEM = 256
EPS = 1e-6

SEQ_TILE = 256
CHUNK = 128
CONV_HIST = 8
POOL_HIST = 16
LANES = 128
VMEM_LIMIT_BYTES = 48 * 1024 * 1024

Q0, K0, V0, O0, ZA0, U0, ZB0, QC0, ZC0, GI0, GF0 = (
    0, 512, 1024, 1536, 2048, 2560, 3072, 3584, 4096, 4608, 4736)
N_PROJ = 4864
P_V, P_O, P_ZA, P_ZB, P_QC, P_ZC, P_GI, P_GF = 0, 512, 1024, 1536, 2048, 2560, 3072, 3200
P_COLS = 3328

BF16 = jnp.bfloat16
F32 = jnp.float32


def _silu(x):
    return x * jax.nn.sigmoid(x)


def _log_sigmoid(x):
    return jnp.minimum(x, 0.0) - jnp.log1p(jnp.exp(-jnp.abs(x)))


def _scan_rows(x, op, identity):
    n = x.shape[0]
    rows = lax.broadcasted_iota(jnp.int32, x.shape, 0)
    k = 1
    while k < n:
        x = op(x, jnp.where(rows >= k, pltpu.roll(x, k, axis=0), identity))
        k *= 2
    return x


def _layer_kernel(x_ref, kmt_ref, vmx_ref, ng_ref, w_ref, bg_ref, cw_ref, cb_ref, mng_ref,
                  pw_ref, ps_ref, wo_ref, fng_ref, o_ref,
                  h_ref, proj_ref, qk_ext, u_ext, q_ref, k_ref, vext_ref, cn_ref, m_ref, y_ref,
                  *, final):
    T, L = SEQ_TILE, CHUNK
    t = pl.program_id(1)

    @pl.when(t == 0)
    def _init():
        qk_ext[0:CONV_HIST, :] = jnp.zeros((CONV_HIST, 2 * GROUP_W), F32)
        u_ext[0:POOL_HIST, :] = jnp.zeros((POOL_HIST, GROUP_W), F32)
        cn_ref[...] = jnp.zeros(cn_ref.shape, F32)
        m_ref[...] = jnp.zeros(m_ref.shape, F32)
        col = lax.broadcasted_iota(jnp.int32, vext_ref.shape, 1)
        vext_ref[...] = jnp.where(col % (2 * HEAD_D) == HEAD_D, 1.0, 0.0).astype(BF16)

    x = x_ref[...]
    ms = jnp.mean(x * x, axis=-1, keepdims=True)
    h_ref[...] = ((x * lax.rsqrt(ms + EPS)) * ng_ref[...]).astype(BF16)
    hb = h_ref[...]

    def proj(c0, n):
        return jnp.dot(hb, w_ref[:, c0:c0 + n], preferred_element_type=F32)

    qk_ext[CONV_HIST:CONV_HIST + T, 0:GROUP_W] = proj(Q0, GROUP_W)
    qk_ext[CONV_HIST:CONV_HIST + T, GROUP_W:2 * GROUP_W] = proj(K0, GROUP_W)
    u_ext[POOL_HIST:POOL_HIST + T, :] = proj(U0, GROUP_W)
    for src, dst in ((V0, P_V), (O0, P_O), (ZA0, P_ZA), (ZB0, P_ZB), (QC0, P_QC), (ZC0, P_ZC)):
        proj_ref[:, dst:dst + GROUP_W] = proj(src, GROUP_W)
    proj_ref[:, P_GI:P_GI + 2 * LANES] = proj(GI0, 2 * LANES) + bg_ref[...]

    acc = cb_ref[...]
    for j in range(CONV_K):
        acc = acc + cw_ref[j:j + 1, :] * qk_ext[pl.ds(CONV_HIST - (CONV_K - 1) + j, T), :]
    qk = _silu(acc)
    q_ref[...] = qk[:, 0:GROUP_W].astype(BF16)
    k_ref[...] = qk[:, GROUP_W:] * (HEAD_D ** -0.5)
    for hd in range(N_HEADS):
        vext_ref[:, hd * 2 * HEAD_D:hd * 2 * HEAD_D + HEAD_D] = (
            proj_ref[:, P_V + hd * HEAD_D:P_V + (hd + 1) * HEAD_D].astype(BF16))

    rowi = lax.broadcasted_iota(jnp.int32, (L, L), 0)
    coli = lax.broadcasted_iota(jnp.int32, (L, L), 1)
    causal = coli <= rowi
    for c in range(T // L):
        r0 = c * L
        gi = proj_ref[r0:r0 + L, P_GI:P_GI + LANES]
        logf = _log_sigmoid(proj_ref[r0:r0 + L, P_GF:P_GF + LANES])
        b = _scan_rows(logf, jnp.add, 0.0)
        r = gi - b
        m_prev = m_ref[...]
        mm = jnp.maximum(_scan_rows(r, jnp.maximum, -jnp.inf), m_prev)
        a = jnp.exp(m_prev - mm)
        e = jnp.exp(-(mm + b))
        r_t = r.T
        m_ref[...] = (mm + b)[L - 1:L, :]
        for hd in range(N_HEADS):
            hs = slice(hd * HEAD_D, (hd + 1) * HEAD_D)
            arg = r_t[hd:hd + 1, :] - mm[:, hd:hd + 1]
            p = jnp.exp(jnp.where(causal, arg, -1e30))
            qh = q_ref[r0:r0 + L, hs]
            k_t = k_ref[r0:r0 + L, hs].T
            s = jnp.dot(qh, k_t.astype(BF16), preferred_element_type=F32) * p
            vx = vext_ref[r0:r0 + L, hd * 2 * HEAD_D:(hd + 1) * 2 * HEAD_D]
            cn = cn_ref[hd]
            nd = (a[:, hd:hd + 1] * jnp.dot(qh, cn.astype(BF16), preferred_element_type=F32)
                  + jnp.dot(s.astype(BF16), vx, preferred_element_type=F32))
            hh = nd[:, 0:HEAD_D] / jnp.maximum(jnp.abs(nd[:, HEAD_D:HEAD_D + 1]), e[:, hd:hd + 1])
            k_tw = (k_t * p[L - 1:L, :]).astype(BF16)
            cn_ref[hd] = (a[L - 1:L, hd:hd + 1] * cn
                          + jnp.dot(k_tw, vx, preferred_element_type=F32))
            hg = jax.nn.sigmoid(proj_ref[r0:r0 + L, P_O + hd * HEAD_D:P_O + (hd + 1) * HEAD_D]) * hh
            hg = hg * lax.rsqrt(jnp.mean(hg * hg, axis=-1, keepdims=True) + EPS)
            hg = hg * mng_ref[:, hs]
            ya = hg * _silu(proj_ref[r0:r0 + L, P_ZA + hd * HEAD_D:P_ZA + (hd + 1) * HEAD_D])
            y_ref[r0:r0 + L, hs] = ya.astype(BF16)

    tg = (t * T + lax.broadcasted_iota(jnp.int32, (T, LANES), 0) + 1).astype(F32)
    for g, win in enumerate(POOL_WINDOWS):
        gs = slice(g * LANES, (g + 1) * LANES)
        u = u_ext[POOL_HIST:POOL_HIST + T, gs]
        ssum = u
        for d in range(1, win):
            ssum = ssum + u_ext[pl.ds(POOL_HIST - d, T), gs]
        pooled = ssum / jnp.minimum(tg, float(win)) - u
        yb = jnp.dot(pooled.astype(BF16), pw_ref[g], preferred_element_type=F32)
        yb = yb * ps_ref[:, gs] * _silu(proj_ref[:, P_ZB + g * LANES:P_ZB + (g + 1) * LANES])
        y_ref[:, GROUP_W + g * LANES:GROUP_W + (g + 1) * LANES] = yb.astype(BF16)

    for hd in range(N_HEADS):
        hs = slice(hd * HEAD_D, (hd + 1) * HEAD_D)
        qc = proj_ref[:, P_QC + hd * HEAD_D:P_QC + (hd + 1) * HEAD_D].astype(BF16)
        s = jnp.dot(qc, kmt_ref[hs, :], preferred_element_type=F32) * (HEAD_D ** -0.5)
        pexp = jnp.exp(s - jnp.max(s, axis=-1, keepdims=True))
        pv = jnp.dot(pexp.astype(BF16), vmx_ref[hd], preferred_element_type=F32)
        yc = pv[:, 0:HEAD_D] / pv[:, HEAD_D:HEAD_D + 1]
        yc = yc * _silu(proj_ref[:, P_ZC + hd * HEAD_D:P_ZC + (hd + 1) * HEAD_D])
        y_ref[:, 2 * GROUP_W + hd * HEAD_D:2 * GROUP_W + (hd + 1) * HEAD_D] = yc.astype(BF16)

    out = x_ref[...] + jnp.dot(y_ref[...], wo_ref[...], preferred_element_type=F32)
    if final:
        ms = jnp.mean(out * out, axis=-1, keepdims=True)
        out = (out * lax.rsqrt(ms + EPS)) * fng_ref[...]
    o_ref[...] = out

    qk_ext[0:CONV_HIST, :] = qk_ext[T:T + CONV_HIST, :]
    u_ext[0:POOL_HIST, :] = u_ext[T:T + POOL_HIST, :]


def _mem_kv_kernel(mem_ref, g_ref, w_ref, kmt_ref, vmx_ref):
    mem = mem_ref[...]
    ms = jnp.mean(mem * mem, axis=-1, keepdims=True)
    mn = ((mem * lax.rsqrt(ms + EPS)) * g_ref[...]).astype(BF16)
    kv = jnp.dot(mn, w_ref[...], preferred_element_type=F32)
    col = lax.broadcasted_iota(jnp.int32, (N_MEM, HEAD_D), 1)
    ones_col = jnp.where(col == 0, 1.0, 0.0).astype(BF16)
    for hd in range(N_HEADS):
        hs = slice(hd * HEAD_D, (hd + 1) * HEAD_D)
        kmt_ref[hs, :] = kv[:, hs].T.astype(BF16)
        vmx_ref[hd, :, 0:HEAD_D] = kv[:, GROUP_W + hd * HEAD_D:GROUP_W + (hd + 1) * HEAD_D].astype(BF16)
        vmx_ref[hd, :, HEAD_D:2 * HEAD_D] = ones_col


def _resident(shape):
    nd = len(shape)
    return pl.BlockSpec(shape, lambda *_: (0,) * nd, pipeline_mode=pl.Buffered(1))


def _mem_kv(mem, mem_norm_g, w_kv):
    depth, batch = w_kv.shape[0], mem.shape[0]
    return pl.pallas_call(
        _mem_kv_kernel,
        grid=(depth, batch),
        in_specs=[
            pl.BlockSpec((None, N_MEM, D_MODEL), lambda l, b: (b, 0, 0)),
            pl.BlockSpec((None, 1, D_MODEL), lambda l, b: (l, 0, 0)),
            pl.BlockSpec((None, D_MODEL, 2 * GROUP_W), lambda l, b: (l, 0, 0)),
        ],
        out_specs=[
            pl.BlockSpec((None, None, GROUP_W, N_MEM), lambda l, b: (l, b, 0, 0)),
            pl.BlockSpec((None, None, N_HEADS, N_MEM, 2 * HEAD_D), lambda l, b: (l, b, 0, 0, 0)),
        ],
        out_shape=[
            jax.ShapeDtypeStruct((depth, batch, GROUP_W, N_MEM), BF16),
            jax.ShapeDtypeStruct((depth, batch, N_HEADS, N_MEM, 2 * HEAD_D), BF16),
        ],
        compiler_params=pltpu.CompilerParams(
            dimension_semantics=("arbitrary", "arbitrary"), vmem_limit_bytes=VMEM_LIMIT_BYTES),
        name="mem_kv",
    )(mem, mem_norm_g.reshape(depth, 1, D_MODEL), w_kv.astype(BF16))


def _layer(x, kmt, vmx, ng, w_all, b_g, conv_w, conv_b, mng, pool_w, pool_scale, w_out, fng, *, final):
    batch, seq, _ = x.shape
    T = SEQ_TILE
    assert seq % T == 0 and T % CHUNK == 0
    kern = functools.partial(_layer_kernel, final=final)
    return pl.pallas_call(
        kern,
        grid=(batch, seq // T),
        in_specs=[
            pl.BlockSpec((None, T, D_MODEL), lambda b, t: (b, t, 0)),
            pl.BlockSpec((None, GROUP_W, N_MEM), lambda b, t: (b, 0, 0)),
            pl.BlockSpec((None, N_HEADS, N_MEM, 2 * HEAD_D), lambda b, t: (b, 0, 0, 0)),
            _resident((1, D_MODEL)),
            _resident((D_MODEL, N_PROJ)),
            _resident((1, 2 * LANES)),
            _resident((CONV_K, 2 * GROUP_W)),
            _resident((1, 2 * GROUP_W)),
            _resident((1, GROUP_W)),
            _resident((len(POOL_WINDOWS), LANES, LANES)),
            _resident((1, GROUP_W)),
            _resident((3 * GROUP_W, D_MODEL)),
            _resident((1, D_MODEL)),
        ],
        out_specs=pl.BlockSpec((None, T, D_MODEL), lambda b, t: (b, t, 0)),
        out_shape=jax.ShapeDtypeStruct(x.shape, F32),
        scratch_shapes=[
            pltpu.VMEM((T, D_MODEL), BF16),
            pltpu.VMEM((T, P_COLS), F32),
            pltpu.VMEM((CONV_HIST + T, 2 * GROUP_W), F32),
            pltpu.VMEM((POOL_HIST + T, GROUP_W), F32),
            pltpu.VMEM((T, GROUP_W), BF16),
            pltpu.VMEM((T, GROUP_W), F32),
            pltpu.VMEM((T, N_HEADS * 2 * HEAD_D), BF16),
            pltpu.VMEM((N_HEADS, HEAD_D, 2 * HEAD_D), F32),
            pltpu.VMEM((1, LANES), F32),
            pltpu.VMEM((T, 3 * GROUP_W), BF16),
        ],
        compiler_params=pltpu.CompilerParams(
            dimension_semantics=("arbitrary", "arbitrary"), vmem_limit_bytes=VMEM_LIMIT_BYTES),
        name="trunk_layer_final" if final else "trunk_layer",
    )(x, kmt, vmx, ng, w_all, b_g, conv_w, conv_b, mng, pool_w, pool_scale, w_out, fng)


def _prep_w_in(w_in, b_gates):
    g0 = 2 * GROUP_W + 3 * GROUP_W
    pad = jnp.zeros((D_MODEL, LANES - N_HEADS), w_in.dtype)
    w_all = jnp.concatenate(
        [w_in[:, :g0], w_in[:, g0 + 2 * N_HEADS:],
         w_in[:, g0:g0 + N_HEADS], pad, w_in[:, g0 + N_HEADS:g0 + 2 * N_HEADS], pad], axis=1)
    bpad = jnp.zeros((LANES - N_HEADS,), b_gates.dtype)
    b_g = jnp.concatenate([b_gates[:N_HEADS], bpad, b_gates[N_HEADS:], bpad]).reshape(1, 2 * LANES)
    return w_all.astype(BF16), b_g


def kernel(x, mem, norm_g, w_in, b_gates, conv_w, conv_b, mlstm_norm_g, pool_w, pool_scale,
           mem_norm_g, w_mem_kv, w_out, final_norm_g):
    depth = w_in.shape[0]
    kmt, vmx = _mem_kv(mem, mem_norm_g, w_mem_kv)
    fng = final_norm_g.reshape(1, D_MODEL)
    for l in range(depth):
        w_all, b_g = _prep_w_in(w_in[l], b_gates[l])
        x = _layer(x, kmt[l], vmx[l], norm_g[l].reshape(1, D_MODEL), w_all, b_g,
                   conv_w[l], conv_b[l].reshape(1, 2 * GROUP_W), mlstm_norm_g[l].reshape(1, GROUP_W),
                   pool_w[l].astype(BF16), pool_scale[l].reshape(1, GROUP_W), w_out[l].astype(BF16),
                   fng, final=(l == depth - 1))
    return x
```

```python
import functools

import jax
import jax.numpy as jnp
from jax import lax
from jax.experimental import pallas as pl
from jax.experimental.pallas import tpu as pltpu

D_MODEL = 1024
GROUP_W = 512
N_HEADS = 4
HEAD_D = 128
CONV_K = 4
POOL_WINDOWS = (2, 4, 8, 16)
N_MEM = 256
EPS = 1e-6

SEQ_TILE = 256
CHUNK = 128
CONV_HIST = 8
POOL_HIST = 16
LANES = 128
VMEM_LIMIT_BYTES = 52 * 1024 * 1024

N_IN = 2 * GROUP_W + 3 * GROUP_W + 2 * N_HEADS + 4 * GROUP_W
GATE0 = 5 * GROUP_W
Q0, K0, V0, O0, ZA0, U0, ZB0, QC0, ZC0 = 0, 512, 1024, 1536, 2048, 2560, 3072, 3584, 4096
N_PROJ = 4608
GATE_ROWS = 16
P_V, P_O, P_ZA, P_ZB, P_QC, P_ZC = 0, 512, 1024, 1536, 2048, 2560
P_COLS = 3072

BF16 = jnp.bfloat16
F32 = jnp.float32


def _sigmoid(x):
    return 0.5 * jnp.tanh(0.5 * x) + 0.5


def _silu(x):
    hx = 0.5 * x
    return hx + hx * jnp.tanh(hx)


def _log_sigmoid(x):
    return jnp.minimum(x, 0.0) - jnp.log1p(jnp.exp(-jnp.abs(x)))


def _scan_lanes(x, op, identity):
    n = x.shape[1]
    lanes = lax.broadcasted_iota(jnp.int32, x.shape, 1)
    k = 1
    while k < n:
        x = op(x, jnp.where(lanes >= k, pltpu.roll(x, k, axis=1), identity))
        k *= 2
    return x


def _rmsnorm(x, g):
    ms = jnp.mean(x * x, axis=-1, keepdims=True)
    return (x * lax.rsqrt(ms + EPS)) * g


def _stage_a(x, first, prev, cur, ng_ref, w_ref, wg_ref, h_ref):
    T = SEQ_TILE
    qk_prev, u_prev, _, _ = prev
    qk_cur, u_cur, proj_cur, g_cur = cur
    qk_cur[0:CONV_HIST, :] = jnp.where(first, 0.0, qk_prev[T:T + CONV_HIST, :])
    u_cur[0:POOL_HIST, :] = jnp.where(first, 0.0, u_prev[T:T + POOL_HIST, :])
    h_ref[...] = _rmsnorm(x, ng_ref[...]).astype(BF16)
    hb = h_ref[...]

    def proj(c0, n):
        return jnp.dot(hb, w_ref[:, c0:c0 + n], preferred_element_type=F32)

    qk_cur[CONV_HIST:CONV_HIST + T, 0:GROUP_W] = proj(Q0, GROUP_W)
    qk_cur[CONV_HIST:CONV_HIST + T, GROUP_W:2 * GROUP_W] = proj(K0, GROUP_W)
    u_cur[POOL_HIST:POOL_HIST + T, :] = proj(U0, GROUP_W)
    for src, dst in ((V0, P_V), (O0, P_O), (ZA0, P_ZA), (ZB0, P_ZB), (QC0, P_QC), (ZC0, P_ZC)):
        proj_cur[:, dst:dst + GROUP_W] = proj(src, GROUP_W)
    g_cur[...] = lax.dot_general(wg_ref[...], hb, (((1,), (1,)), ((), ())), preferred_element_type=F32)


def _stage_b(x_ref, x_row, first, t_seq, cur, o_ref, kmt_ref, vmx_ref, bg_ref, cw_ref, cb_ref, mng_ref,
             pw_ref, ps_ref, wo_ref, fng_ref, q_ref, k_ref, vext_ref, cn_ref, m_ref, y_ref, final):
    T, L = SEQ_TILE, CHUNK
    qk_ext, u_ext, proj_ref, g_ref = cur

    def conv(c0):
        acc = cb_ref[:, c0:c0 + GROUP_W]
        for j in range(CONV_K):
            acc = acc + (cw_ref[j:j + 1, c0:c0 + GROUP_W]
                         * qk_ext[pl.ds(CONV_HIST - (CONV_K - 1) + j, T), c0:c0 + GROUP_W])
        return _silu(acc)

    q_ref[...] = conv(0)
    k_ref[...] = conv(GROUP_W) * (HEAD_D ** -0.5)
    for hd in range(N_HEADS):
        vext_ref[:, hd * 2 * HEAD_D:hd * 2 * HEAD_D + HEAD_D] = (
            proj_ref[:, P_V + hd * HEAD_D:P_V + (hd + 1) * HEAD_D].astype(BF16))

    rowi = lax.broadcasted_iota(jnp.int32, (L, L), 0)
    coli = lax.broadcasted_iota(jnp.int32, (L, L), 1)
    causal = coli <= rowi
    for c in range(T // L):
        r0 = c * L
        g8 = g_ref[0:2 * N_HEADS, r0:r0 + L] + bg_ref[...]
        gi = g8
        logf = _log_sigmoid(pltpu.roll(g8, N_HEADS, axis=0))
        b = _scan_lanes(logf, jnp.add, 0.0)
        r = gi - b
        m_prev = m_ref[...]
        if c == 0:
            m_prev = jnp.where(first, 0.0, m_prev)
        mm = jnp.maximum(_scan_lanes(r, jnp.maximum, -jnp.inf), m_prev)
        a = jnp.exp(m_prev - mm)
        e = jnp.exp(-(mm + b))
        m_ref[...] = jnp.broadcast_to((mm + b)[:, L - 1:L], m_ref.shape)
        cols = jnp.concatenate([mm, a, e, jnp.zeros((LANES - 6 * N_HEADS, L), F32)], axis=0).T
        mm_c, a_c, e_c = (cols[:, i * 2 * N_HEADS:(i + 1) * 2 * N_HEADS] for i in range(3))
        for hd in range(N_HEADS):
            hs = slice(hd * HEAD_D, (hd + 1) * HEAD_D)
            arg = r[hd:hd + 1, :] - mm_c[:, hd:hd + 1]
            p = jnp.exp(jnp.where(causal, arg, -1e30))
            qh = q_ref[r0:r0 + L, hs]
            k_t = k_ref[r0:r0 + L, hs].T
            s = jnp.dot(qh.astype(BF16), k_t.astype(BF16), preferred_element_type=F32) * p
            vx = vext_ref[r0:r0 + L, hd * 2 * HEAD_D:(hd + 1) * 2 * HEAD_D]
            cn = cn_ref[hd]
            if c == 0:
                cn = jnp.where(first, 0.0, cn)
            lhs = jnp.concatenate([(a_c[:, hd:hd + 1] * qh).astype(BF16), s.astype(BF16)], axis=1)
            rhs = jnp.concatenate([cn.astype(BF16), vx], axis=0)
            nd = jnp.dot(lhs, rhs, preferred_element_type=F32)
            hh = nd[:, 0:HEAD_D] / jnp.maximum(jnp.abs(nd[:, HEAD_D:HEAD_D + 1]), e_c[:, hd:hd + 1])
            k_tw = (k_t * p[L - 1:L, :]).astype(BF16)
            cn_ref[hd] = (a[hd:hd + 1, L - 1:L] * cn
                          + jnp.dot(k_tw, vx, preferred_element_type=F32))
            hg = _sigmoid(proj_ref[r0:r0 + L, P_O + hd * HEAD_D:P_O + (hd + 1) * HEAD_D]) * hh
            hg = hg * lax.rsqrt(jnp.mean(hg * hg, axis=-1, keepdims=True) + EPS)
            hg = hg * mng_ref[:, hs]
            ya = hg * _silu(proj_ref[r0:r0 + L, P_ZA + hd * HEAD_D:P_ZA + (hd + 1) * HEAD_D])
            y_ref[r0:r0 + L, hs] = ya.astype(BF16)

    tg = (t_seq * T + lax.broadcasted_iota(jnp.int32, (T, LANES), 0) + 1).astype(F32)
    zero_w = jnp.zeros((LANES, LANES), BF16)
    for g0 in range(0, len(POOL_WINDOWS), 2):
        pooled = []
        for g in (g0, g0 + 1):
            win = POOL_WINDOWS[g]
            gs = slice(g * LANES, (g + 1) * LANES)
            ssum = u_ext[:, gs]
            sh = 1
            while sh < win:
                ssum = ssum + pltpu.roll(ssum, sh, axis=0)
                sh *= 2
            u = u_ext[POOL_HIST:POOL_HIST + T, gs]
            pooled.append((ssum[POOL_HIST:POOL_HIST + T] / jnp.minimum(tg, float(win)) - u).astype(BF16))
        w_a, w_b = pw_ref[g0].astype(BF16), pw_ref[g0 + 1].astype(BF16)
        w_pair = jnp.concatenate([jnp.concatenate([w_a, zero_w], axis=1),
                                  jnp.concatenate([zero_w, w_b], axis=1)], axis=0)
        yb = jnp.dot(jnp.concatenate(pooled, axis=1), w_pair, preferred_element_type=F32)
        yb = (yb * ps_ref[:, g0 * LANES:(g0 + 2) * LANES]
              * _silu(proj_ref[:, P_ZB + g0 * LANES:P_ZB + (g0 + 2) * LANES]))
        y_ref[:, GROUP_W + g0 * LANES:GROUP_W + (g0 + 2) * LANES] = yb.astype(BF16)

    for hd in range(N_HEADS):
        hs = slice(hd * HEAD_D, (hd + 1) * HEAD_D)
        qc = proj_ref[:, P_QC + hd * HEAD_D:P_QC + (hd + 1) * HEAD_D].astype(BF16)
        s = jnp.dot(qc, kmt_ref[hs, :], preferred_element_type=F32) * (HEAD_D ** -0.5)
        pexp = jnp.exp(s - jnp.max(s, axis=-1, keepdims=True))
        pv = jnp.dot(pexp.astype(BF16), vmx_ref[hd], preferred_element_type=F32)
        yc = pv[:, 0:HEAD_D] / pv[:, HEAD_D:HEAD_D + 1]
        yc = yc * _silu(proj_ref[:, P_ZC + hd * HEAD_D:P_ZC + (hd + 1) * HEAD_D])
        y_ref[:, 2 * GROUP_W + hd * HEAD_D:2 * GROUP_W + (hd + 1) * HEAD_D] = yc.astype(BF16)

    out = x_ref[x_row:x_row + T, :] + jnp.dot(y_ref[...], wo_ref[...], preferred_element_type=F32)
    if final:
        out = _rmsnorm(out, fng_ref[...])
    o_ref[x_row:x_row + T, :] = out


def _layer_kernel(xp_ref, xn_ref, kmt_ref, vmx_ref, ng_ref, w_ref, wg_ref, bg_ref, cw_ref, cb_ref, mng_ref,
                  pw_ref, ps_ref, wo_ref, fng_ref, o_ref,
                  h_ref, qk0, u0, proj0, g0, qk1, u1, proj1, g1, q_ref, k_ref, vext_ref, cn_ref, m_ref, y_ref,
                  *, final, tiles_per_seq):
    T = SEQ_TILE
    k = pl.program_id(0)
    set0, set1 = (qk0, u0, proj0, g0), (qk1, u1, proj1, g1)
    a_args = (ng_ref, w_ref, wg_ref, h_ref)
    b_args = (kmt_ref, vmx_ref, bg_ref, cw_ref, cb_ref, mng_ref, pw_ref, ps_ref, wo_ref, fng_ref,
              q_ref, k_ref, vext_ref, cn_ref, m_ref, y_ref, final)

    @pl.when(k == 0)
    def _prologue():
        col = lax.broadcasted_iota(jnp.int32, vext_ref.shape, 1)
        vext_ref[...] = jnp.where(col % (2 * HEAD_D) == HEAD_D, 1.0, 0.0).astype(BF16)
        cn_ref[...] = jnp.zeros(cn_ref.shape, F32)
        m_ref[...] = jnp.zeros(m_ref.shape, F32)
        qk1[T:T + CONV_HIST, :] = jnp.zeros((CONV_HIST, 2 * GROUP_W), F32)
        u1[T:T + POOL_HIST, :] = jnp.zeros((POOL_HIST, GROUP_W), F32)
        _stage_a(xp_ref[0:T, :], True, set1, set0, *a_args)

    t_even = (2 * k) % tiles_per_seq
    first_even = t_even == 0
    first_next = (2 * k + 2) % tiles_per_seq == 0

    _stage_a(xp_ref[T:2 * T, :], False, set0, set1, *a_args)
    _stage_b(xp_ref, 0, first_even, t_even, set0, o_ref, *b_args)
    _stage_a(xn_ref[...], first_next, set1, set0, *a_args)
    _stage_b(xp_ref, T, False, t_even + 1, set1, o_ref, *b_args)


def _mem_kv_kernel(mem_ref, g_ref, w_ref, kmt_ref, vmx_ref):
    mn = _rmsnorm(mem_ref[...], g_ref[...]).astype(BF16)
    kv = jnp.dot(mn, w_ref[...].astype(BF16), preferred_element_type=F32)
    col = lax.broadcasted_iota(jnp.int32, (N_MEM, HEAD_D), 1)
    ones_col = jnp.where(col == 0, 1.0, 0.0).astype(BF16)
    for hd in range(N_HEADS):
        hs = slice(hd * HEAD_D, (hd + 1) * HEAD_D)
        kmt_ref[hs, :] = kv[:, hs].T.astype(BF16)
        vmx_ref[hd, :, 0:HEAD_D] = kv[:, GROUP_W + hd * HEAD_D:GROUP_W + (hd + 1) * HEAD_D].astype(BF16)
        vmx_ref[hd, :, HEAD_D:2 * HEAD_D] = ones_col


def _prep_kernel(win_ref, wout_ref, wall_ref, wg_ref, wo_ref):
    w = win_ref[...]
    wall_ref[:, 0:GATE0] = w[:, 0:GATE0].astype(BF16)
    wall_ref[:, GATE0:N_PROJ] = w[:, GATE0 + 2 * N_HEADS:N_IN].astype(BF16)
    g_t = w[:, GATE0:GATE0 + LANES].T[0:GATE_ROWS, :]
    row = lax.broadcasted_iota(jnp.int32, g_t.shape, 0)
    wg_ref[...] = jnp.where(row < 2 * N_HEADS, g_t, 0.0).astype(BF16)
    wo_ref[...] = wout_ref[...].astype(BF16)


def _resident(shape, layer):
    nd = len(shape)
    return pl.BlockSpec((None,) + shape, lambda k: (layer,) + (0,) * nd, pipeline_mode=pl.Buffered(1))


def _prep_weights(w_in, w_out):
    depth = w_in.shape[0]
    steps = 4
    rb_in, rb_out = D_MODEL // steps, 3 * GROUP_W // steps
    return pl.pallas_call(
        _prep_kernel,
        grid=(depth, steps),
        in_specs=[
            pl.BlockSpec((None, rb_in, N_IN), lambda l, i: (l, i, 0)),
            pl.BlockSpec((None, rb_out, D_MODEL), lambda l, i: (l, i, 0)),
        ],
        out_specs=[
            pl.BlockSpec((None, rb_in, N_PROJ), lambda l, i: (l, i, 0)),
            pl.BlockSpec((None, GATE_ROWS, rb_in), lambda l, i: (l, 0, i)),
            pl.BlockSpec((None, rb_out, D_MODEL), lambda l, i: (l, i, 0)),
        ],
        out_shape=[
            jax.ShapeDtypeStruct((depth, D_MODEL, N_PROJ), BF16),
            jax.ShapeDtypeStruct((depth, GATE_ROWS, D_MODEL), BF16),
            jax.ShapeDtypeStruct((depth, 3 * GROUP_W, D_MODEL), BF16),
        ],
        compiler_params=pltpu.CompilerParams(
            dimension_semantics=("arbitrary", "arbitrary"), vmem_limit_bytes=VMEM_LIMIT_BYTES),
        name="prep_weights",
    )(w_in, w_out)


def _mem_kv(mem, mem_norm_g, w_kv):
    depth, batch = w_kv.shape[0], mem.shape[0]
    return pl.pallas_call(
        _mem_kv_kernel,
        grid=(depth, batch),
        in_specs=[
            pl.BlockSpec((None, N_MEM, D_MODEL), lambda l, b: (b, 0, 0)),
            pl.BlockSpec((None, 1, D_MODEL), lambda l, b: (l, 0, 0)),
            pl.BlockSpec((None, D_MODEL, 2 * GROUP_W), lambda l, b: (l, 0, 0)),
        ],
        out_specs=[
            pl.BlockSpec((None, None, GROUP_W, N_MEM), lambda l, b: (l, b, 0, 0)),
            pl.BlockSpec((None, None, N_HEADS, N_MEM, 2 * HEAD_D), lambda l, b: (l, b, 0, 0, 0)),
        ],
        out_shape=[
            jax.ShapeDtypeStruct((depth, batch, GROUP_W, N_MEM), BF16),
            jax.ShapeDtypeStruct((depth, batch, N_HEADS, N_MEM, 2 * HEAD_D), BF16),
        ],
        compiler_params=pltpu.CompilerParams(
            dimension_semantics=("arbitrary", "arbitrary"), vmem_limit_bytes=VMEM_LIMIT_BYTES),
        name="mem_kv",
    )(mem, mem_norm_g.reshape(depth, 1, D_MODEL), w_kv)


def _layer(layer, final, x, kmt, vmx, ng, w_all, w_g, b_g, conv_w, conv_b, mng, pool_w, pool_scale, w_out, fng):
    batch, seq, _ = x.shape
    T = SEQ_TILE
    assert seq % (2 * T) == 0 and T % CHUNK == 0
    tiles_per_seq = seq // T
    pairs_per_seq = tiles_per_seq // 2
    last_tile = batch * tiles_per_seq - 1

    def next_tile(k):
        nt = jnp.minimum(2 * k + 2, last_tile)
        return nt // tiles_per_seq, nt % tiles_per_seq, 0

    kern = functools.partial(_layer_kernel, final=final, tiles_per_seq=tiles_per_seq)
    set_scratch = [
        pltpu.VMEM((CONV_HIST + T, 2 * GROUP_W), F32),
        pltpu.VMEM((POOL_HIST + T, GROUP_W), F32),
        pltpu.VMEM((T, P_COLS), F32),
        pltpu.VMEM((GATE_ROWS, T), F32),
    ]
    return pl.pallas_call(
        kern,
        grid=(batch * pairs_per_seq,),
        in_specs=[
            pl.BlockSpec((None, 2 * T, D_MODEL), lambda k: (k // pairs_per_seq, k % pairs_per_seq, 0)),
            pl.BlockSpec((None, T, D_MODEL), next_tile),
            pl.BlockSpec((None, None, GROUP_W, N_MEM), lambda k: (layer, k // pairs_per_seq, 0, 0)),
            pl.BlockSpec((None, None, N_HEADS, N_MEM, 2 * HEAD_D),
                         lambda k: (layer, k // pairs_per_seq, 0, 0, 0)),
            _resident((1, D_MODEL), layer),
            _resident((D_MODEL, N_PROJ), layer),
            _resident((GATE_ROWS, D_MODEL), layer),
            _resident((2 * N_HEADS, LANES), layer),
            _resident((CONV_K, 2 * GROUP_W), layer),
            _resident((1, 2 * GROUP_W), layer),
            _resident((1, GROUP_W), layer),
            _resident((len(POOL_WINDOWS), LANES, LANES), layer),
            _resident((1, GROUP_W), layer),
            _resident((3 * GROUP_W, D_MODEL), layer),
            pl.BlockSpec((1, D_MODEL), lambda k: (0, 0), pipeline_mode=pl.Buffered(1)),
        ],
        out_specs=pl.BlockSpec((None, 2 * T, D_MODEL), lambda k: (k // pairs_per_seq, k % pairs_per_seq, 0)),
        out_shape=jax.ShapeDtypeStruct(x.shape, F32),
        scratch_shapes=[pltpu.VMEM((T, D_MODEL), BF16)] + set_scratch + set_scratch + [
            pltpu.VMEM((T, GROUP_W), F32),
            pltpu.VMEM((T, GROUP_W), F32),
            pltpu.VMEM((T, N_HEADS * 2 * HEAD_D), BF16),
            pltpu.VMEM((N_HEADS, HEAD_D, 2 * HEAD_D), F32),
            pltpu.VMEM((2 * N_HEADS, LANES), F32),
            pltpu.VMEM((T, 3 * GROUP_W), BF16),
        ],
        compiler_params=pltpu.CompilerParams(
            dimension_semantics=("arbitrary",), vmem_limit_bytes=VMEM_LIMIT_BYTES),
        name="trunk_layer_final" if final else "trunk_layer",
    )(x, x, kmt, vmx, ng, w_all, w_g, b_g, conv_w, conv_b, mng, pool_w, pool_scale, w_out, fng)


def kernel(x, mem, norm_g, w_in, b_gates, conv_w, conv_b, mlstm_norm_g, pool_w, pool_scale,
           mem_norm_g, w_mem_kv, w_out, final_norm_g):
    depth = w_in.shape[0]
    w_all, w_g, w_out_b = _prep_weights(w_in, w_out)
    kmt, vmx = _mem_kv(mem, mem_norm_g, w_mem_kv)
    b_g = jnp.broadcast_to(b_gates[:, :, None], (depth, 2 * N_HEADS, LANES))
    args = (kmt, vmx, norm_g.reshape(depth, 1, D_MODEL), w_all, w_g, b_g,
            conv_w, conv_b.reshape(depth, 1, 2 * GROUP_W), mlstm_norm_g.reshape(depth, 1, GROUP_W),
            pool_w, pool_scale.reshape(depth, 1, GROUP_W), w_out_b, final_norm_g.reshape(1, D_MODEL))
    for l in range(depth):
        x = _layer(l, l == depth - 1, x, *args)
    return x
```

```python
import functools

import jax
import jax.numpy as jnp
from jax import lax
from jax.experimental import pallas as pl
from jax.experimental.pallas import tpu as pltpu

D_MODEL = 1024
GROUP_W = 512
N_HEADS = 4
HEAD_D = 128
CONV_K = 4
POOL_WINDOWS = (2, 4, 8, 16)
N_MEM = 256
EPS = 1e-6

SEQ_TILE = 256
CHUNK = 128
CONV_HIST = 8
POOL_HIST = 16
LANES = 128
VMEM_LIMIT_BYTES = 52 * 1024 * 1024

N_IN = 2 * GROUP_W + 3 * GROUP_W + 2 * N_HEADS + 4 * GROUP_W
GATE0 = 5 * GROUP_W
Q0, K0, V0, O0, ZA0, U0, ZB0, QC0, ZC0 = 0, 512, 1024, 1536, 2048, 2560, 3072, 3584, 4096
N_PROJ = 4608
GATE_ROWS = 16
P_V, P_O, P_ZA, P_ZB, P_QC, P_ZC = 0, 512, 1024, 1536, 2048, 2560
P_COLS = 3072

BF16 = jnp.bfloat16
F32 = jnp.float32


def _sigmoid(x):
    return 0.5 * jnp.tanh(0.5 * x) + 0.5


def _silu(x):
    hx = 0.5 * x
    return hx + hx * jnp.tanh(hx)


def _log_sigmoid(x):
    return jnp.minimum(x, 0.0) - jnp.log1p(jnp.exp(-jnp.abs(x)))


def _scan_lanes(x, op, identity):
    n = x.shape[1]
    lanes = lax.broadcasted_iota(jnp.int32, x.shape, 1)
    k = 1
    while k < n:
        x = op(x, jnp.where(lanes >= k, pltpu.roll(x, k, axis=1), identity))
        k *= 2
    return x


def _rmsnorm(x, g):
    ms = jnp.mean(x * x, axis=-1, keepdims=True)
    return (x * lax.rsqrt(ms + EPS)) * g


def _stage_a(x_ref, x_row, first, prev, cur, ng_ref, w_ref, wg_ref, h_ref):
    T = SEQ_TILE
    qk_prev, u_prev, _, _ = prev
    qk_cur, u_cur, proj_cur, g_cur = cur

    def head():
        qk_cur[0:CONV_HIST, :] = jnp.where(first, 0.0, qk_prev[T:T + CONV_HIST, :])
        u_cur[0:POOL_HIST, :] = jnp.where(first, 0.0, u_prev[T:T + POOL_HIST, :])
        h_ref[...] = _rmsnorm(x_ref[x_row:x_row + T, :], ng_ref[...]).astype(BF16)

    def proj(c0):
        return jnp.dot(h_ref[...], w_ref[:, c0:c0 + GROUP_W], preferred_element_type=F32)

    def dot_q():
        qk_cur[CONV_HIST:CONV_HIST + T, 0:GROUP_W] = proj(Q0)

    def dot_k():
        qk_cur[CONV_HIST:CONV_HIST + T, GROUP_W:2 * GROUP_W] = proj(K0)

    def dot_u():
        u_cur[POOL_HIST:POOL_HIST + T, :] = proj(U0)

    def dot_to(src, dst):
        def run():
            proj_cur[:, dst:dst + GROUP_W] = proj(src)
        return run

    def dot_gates():
        g_cur[...] = lax.dot_general(wg_ref[...], h_ref[...], (((1,), (1,)), ((), ())),
                                     preferred_element_type=F32)

    return [head, dot_q, dot_k, dot_u, dot_to(V0, P_V), dot_to(O0, P_O), dot_to(ZA0, P_ZA),
            dot_to(ZB0, P_ZB), dot_to(QC0, P_QC), dot_to(ZC0, P_ZC), dot_gates]


def _stage_b(x_ref, x_row, first, t_seq, cur, o_ref, kmt_ref, vmx_ref, bg_ref, cw_ref, cb_ref, mng_ref,
             pw_ref, ps_ref, wo_ref, fng_ref, q_ref, k_ref, vext_ref, cn_ref, m_ref, y_ref, final):
    T, L = SEQ_TILE, CHUNK
    qk_ext, u_ext, proj_ref, g_ref = cur

    def conv(c0):
        xs = qk_ext[:, c0:c0 + GROUP_W]
        acc = cb_ref[:, c0:c0 + GROUP_W] + cw_ref[CONV_K - 1:CONV_K, c0:c0 + GROUP_W] * xs[CONV_HIST:]
        for j in range(CONV_K - 1):
            back = pltpu.roll(xs, CONV_K - 1 - j, axis=0)[CONV_HIST:]
            acc = acc + cw_ref[j:j + 1, c0:c0 + GROUP_W] * back
        return _silu(acc)

    def conv_q():
        q_ref[...] = conv(0)

    def conv_k():
        k_ref[...] = conv(GROUP_W) * (HEAD_D ** -0.5)
        for hd in range(N_HEADS):
            vext_ref[:, hd * 2 * HEAD_D:hd * 2 * HEAD_D + HEAD_D] = (
                proj_ref[:, P_V + hd * HEAD_D:P_V + (hd + 1) * HEAD_D].astype(BF16))

    def mlstm_chunk(c):
        def run():
            rowi = lax.broadcasted_iota(jnp.int32, (L, L), 0)
            coli = lax.broadcasted_iota(jnp.int32, (L, L), 1)
            causal = coli <= rowi
            r0 = c * L
            g8 = g_ref[0:2 * N_HEADS, r0:r0 + L] + bg_ref[...]
            gi = g8
            logf = _log_sigmoid(pltpu.roll(g8, N_HEADS, axis=0))
            b = _scan_lanes(logf, jnp.add, 0.0)
            r = gi - b
            m_prev = m_ref[...]
            if c == 0:
                m_prev = jnp.where(first, 0.0, m_prev)
            mm = jnp.maximum(_scan_lanes(r, jnp.maximum, -jnp.inf), m_prev)
            a = jnp.exp(m_prev - mm)
            e = jnp.exp(-(mm + b))
            m_ref[...] = jnp.broadcast_to((mm + b)[:, L - 1:L], m_ref.shape)
            cols = jnp.concatenate([mm, a, e, jnp.zeros((LANES - 6 * N_HEADS, L), F32)], axis=0).T
            mm_c, a_c, e_c = (cols[:, i * 2 * N_HEADS:(i + 1) * 2 * N_HEADS] for i in range(3))
            for hd in range(N_HEADS):
                hs = slice(hd * HEAD_D, (hd + 1) * HEAD_D)
                arg = r[hd:hd + 1, :] - mm_c[:, hd:hd + 1]
                p = jnp.exp(jnp.where(causal, arg, -1e30))
                qh = q_ref[r0:r0 + L, hs]
                k_t = k_ref[r0:r0 + L, hs].T
                s = jnp.dot(qh.astype(BF16), k_t.astype(BF16), preferred_element_type=F32) * p
                vx = vext_ref[r0:r0 + L, hd * 2 * HEAD_D:(hd + 1) * 2 * HEAD_D]
                cn = cn_ref[hd]
                if c == 0:
                    cn = jnp.where(first, 0.0, cn)
                lhs = jnp.concatenate([(a_c[:, hd:hd + 1] * qh).astype(BF16), s.astype(BF16)], axis=1)
                rhs = jnp.concatenate([cn.astype(BF16), vx], axis=0)
                nd = jnp.dot(lhs, rhs, preferred_element_type=F32)
                hh = nd[:, 0:HEAD_D] / jnp.maximum(jnp.abs(nd[:, HEAD_D:HEAD_D + 1]), e_c[:, hd:hd + 1])
                k_tw = (k_t * p[L - 1:L, :]).astype(BF16)
                cn_ref[hd] = (a[hd:hd + 1, L - 1:L] * cn
                              + jnp.dot(k_tw, vx, preferred_element_type=F32))
                hg = _sigmoid(proj_ref[r0:r0 + L, P_O + hd * HEAD_D:P_O + (hd + 1) * HEAD_D]) * hh
                hg = hg * lax.rsqrt(jnp.mean(hg * hg, axis=-1, keepdims=True) + EPS)
                hg = hg * mng_ref[:, hs]
                ya = hg * _silu(proj_ref[r0:r0 + L, P_ZA + hd * HEAD_D:P_ZA + (hd + 1) * HEAD_D])
                y_ref[r0:r0 + L, hs] = ya
        return run

    def pool():
        tg = (t_seq * T + lax.broadcasted_iota(jnp.int32, (T, LANES), 0) + 1).astype(F32)
        zero_w = jnp.zeros((LANES, LANES), BF16)
        for g0 in range(0, len(POOL_WINDOWS), 2):
            pooled = []
            for g in (g0, g0 + 1):
                win = POOL_WINDOWS[g]
                gs = slice(g * LANES, (g + 1) * LANES)
                ssum = u_ext[:, gs]
                sh = 1
                while sh < win:
                    ssum = ssum + pltpu.roll(ssum, sh, axis=0)
                    sh *= 2
                u = u_ext[POOL_HIST:POOL_HIST + T, gs]
                pooled.append((ssum[POOL_HIST:POOL_HIST + T] / jnp.minimum(tg, float(win)) - u).astype(BF16))
            w_a, w_b = pw_ref[g0].astype(BF16), pw_ref[g0 + 1].astype(BF16)
            w_pair = jnp.concatenate([jnp.concatenate([w_a, zero_w], axis=1),
                                      jnp.concatenate([zero_w, w_b], axis=1)], axis=0)
            yb = jnp.dot(jnp.concatenate(pooled, axis=1), w_pair, preferred_element_type=F32)
            yb = (yb * ps_ref[:, g0 * LANES:(g0 + 2) * LANES]
                  * _silu(proj_ref[:, P_ZB + g0 * LANES:P_ZB + (g0 + 2) * LANES]))
            y_ref[:, GROUP_W + g0 * LANES:GROUP_W + (g0 + 2) * LANES] = yb

    def attention():
        for hd in range(N_HEADS):
            hs = slice(hd * HEAD_D, (hd + 1) * HEAD_D)
            qc = proj_ref[:, P_QC + hd * HEAD_D:P_QC + (hd + 1) * HEAD_D].astype(BF16)
            s = jnp.dot(qc, kmt_ref[hs, :], preferred_element_type=F32) * (HEAD_D ** -0.5)
            pexp = jnp.exp(s - jnp.max(s, axis=-1, keepdims=True))
            pv = jnp.dot(pexp.astype(BF16), vmx_ref[hd], preferred_element_type=F32)
            yc = pv[:, 0:HEAD_D] / pv[:, HEAD_D:HEAD_D + 1]
            yc = yc * _silu(proj_ref[:, P_ZC + hd * HEAD_D:P_ZC + (hd + 1) * HEAD_D])
            y_ref[:, 2 * GROUP_W + hd * HEAD_D:2 * GROUP_W + (hd + 1) * HEAD_D] = yc

    def out_proj():
        out = x_ref[x_row:x_row + T, :]
        for g0 in range(0, 3 * GROUP_W, GROUP_W):
            out = out + jnp.dot(y_ref[:, g0:g0 + GROUP_W].astype(BF16), wo_ref[g0:g0 + GROUP_W, :],
                                preferred_element_type=F32)
        if final:
            out = _rmsnorm(out, fng_ref[...])
        o_ref[x_row:x_row + T, :] = out

    return dict(conv_q=conv_q, conv_k=conv_k, chunks=[mlstm_chunk(c) for c in range(T // L)],
                pool=pool, attention=attention, out_proj=out_proj)


def _layer_kernel(xp_ref, xn_ref, kmt_ref, vmx_ref, ng_ref, w_ref, wg_ref, bg_ref, cw_ref, cb_ref, mng_ref,
                  pw_ref, ps_ref, wo_ref, fng_ref, o_ref,
                  h_ref, qk0, u0, proj0, g0, qk1, u1, proj1, g1, q_ref, k_ref, vext_ref, cn_ref, m_ref, y_ref,
                  *, final, tiles_per_seq):
    T = SEQ_TILE
    k = pl.program_id(0)
    set0, set1 = (qk0, u0, proj0, g0), (qk1, u1, proj1, g1)
    a_args = (ng_ref, w_ref, wg_ref, h_ref)
    b_args = (kmt_ref, vmx_ref, bg_ref, cw_ref, cb_ref, mng_ref, pw_ref, ps_ref, wo_ref, fng_ref,
              q_ref, k_ref, vext_ref, cn_ref, m_ref, y_ref, final)

    @pl.when(k == 0)
    def _prologue():
        col = lax.broadcasted_iota(jnp.int32, vext_ref.shape, 1)
        vext_ref[...] = jnp.where(col % (2 * HEAD_D) == HEAD_D, 1.0, 0.0).astype(BF16)
        cn_ref[...] = jnp.zeros(cn_ref.shape, F32)
        m_ref[...] = jnp.zeros(m_ref.shape, F32)
        qk1[T:T + CONV_HIST, :] = jnp.zeros((CONV_HIST, 2 * GROUP_W), F32)
        u1[T:T + POOL_HIST, :] = jnp.zeros((POOL_HIST, GROUP_W), F32)
        for piece in _stage_a(xp_ref, 0, True, set1, set0, *a_args):
            piece()

    t_even = (2 * k) % tiles_per_seq
    first_even = t_even == 0
    first_next = (2 * k + 2) % tiles_per_seq == 0

    def half_step(x_row, first, t_seq, cur, nxt, xa_ref, xa_row, first_a):
        for piece in _stage_a(xa_ref, xa_row, first_a, cur, nxt, *a_args):
            piece()
        b = _stage_b(xp_ref, x_row, first, t_seq, cur, o_ref, *b_args)
        for phase in (b["conv_q"], b["conv_k"], *b["chunks"], b["pool"], b["attention"], b["out_proj"]):
            phase()

    half_step(0, first_even, t_even, set0, set1, xp_ref, T, False)
    half_step(T, False, t_even + 1, set1, set0, xn_ref, 0, first_next)


def _mem_kv_kernel(mem_ref, g_ref, w_ref, kmt_ref, vmx_ref):
    mn = _rmsnorm(mem_ref[...], g_ref[...]).astype(BF16)
    kv = jnp.dot(mn, w_ref[...].astype(BF16), preferred_element_type=F32)
    col = lax.broadcasted_iota(jnp.int32, (N_MEM, HEAD_D), 1)
    ones_col = jnp.where(col == 0, 1.0, 0.0).astype(BF16)
    for hd in range(N_HEADS):
        hs = slice(hd * HEAD_D, (hd + 1) * HEAD_D)
        kmt_ref[hs, :] = kv[:, hs].T.astype(BF16)
        vmx_ref[hd, :, 0:HEAD_D] = kv[:, GROUP_W + hd * HEAD_D:GROUP_W + (hd + 1) * HEAD_D].astype(BF16)
        vmx_ref[hd, :, HEAD_D:2 * HEAD_D] = ones_col


def _prep_kernel(wint_ref, wout_ref, wall_ref, wg_ref, wo_ref):
    wall_ref[:, 0:GATE0] = wint_ref[0:GATE0, :].T.astype(BF16)
    wall_ref[:, GATE0:N_PROJ] = wint_ref[GATE0 + 2 * N_HEADS:N_IN, :].T.astype(BF16)
    wg_ref[0:2 * N_HEADS, :] = wint_ref[GATE0:GATE0 + 2 * N_HEADS, :].astype(BF16)
    wg_ref[2 * N_HEADS:GATE_ROWS, :] = jnp.zeros((GATE_ROWS - 2 * N_HEADS, wg_ref.shape[1]), BF16)
    wo_ref[...] = wout_ref[...].astype(BF16)


def _resident(shape, layer):
    nd = len(shape)
    return pl.BlockSpec((None,) + shape, lambda k: (layer,) + (0,) * nd, pipeline_mode=pl.Buffered(1))


def _prep_weights(w_in, w_out):
    depth = w_in.shape[0]
    steps = 4
    rb_in, rb_out = D_MODEL // steps, 3 * GROUP_W // steps
    return pl.pallas_call(
        _prep_kernel,
        grid=(depth, steps),
        in_specs=[
            pl.BlockSpec((None, N_IN, rb_in), lambda l, i: (l, 0, i)),
            pl.BlockSpec((None, rb_out, D_MODEL), lambda l, i: (l, i, 0)),
        ],
        out_specs=[
            pl.BlockSpec((None, rb_in, N_PROJ), lambda l, i: (l, i, 0)),
            pl.BlockSpec((None, GATE_ROWS, rb_in), lambda l, i: (l, 0, i)),
            pl.BlockSpec((None, rb_out, D_MODEL), lambda l, i: (l, i, 0)),
        ],
        out_shape=[
            jax.ShapeDtypeStruct((depth, D_MODEL, N_PROJ), BF16),
            jax.ShapeDtypeStruct((depth, GATE_ROWS, D_MODEL), BF16),
            jax.ShapeDtypeStruct((depth, 3 * GROUP_W, D_MODEL), BF16),
        ],
        compiler_params=pltpu.CompilerParams(
            dimension_semantics=("arbitrary", "arbitrary"), vmem_limit_bytes=VMEM_LIMIT_BYTES),
        name="prep_weights",
    )(jnp.swapaxes(w_in, 1, 2), w_out)


def _mem_kv(mem, mem_norm_g, w_kv):
    depth, batch = w_kv.shape[0], mem.shape[0]
    return pl.pallas_call(
        _mem_kv_kernel,
        grid=(depth, batch),
        in_specs=[
            pl.BlockSpec((None, N_MEM, D_MODEL), lambda l, b: (b, 0, 0)),
            pl.BlockSpec((None, 1, D_MODEL), lambda l, b: (l, 0, 0)),
            pl.BlockSpec((None, D_MODEL, 2 * GROUP_W), lambda l, b: (l, 0, 0)),
        ],
        out_specs=[
            pl.BlockSpec((None, None, GROUP_W, N_MEM), lambda l, b: (l, b, 0, 0)),
            pl.BlockSpec((None, None, N_HEADS, N_MEM, 2 * HEAD_D), lambda l, b: (l, b, 0, 0, 0)),
        ],
        out_shape=[
            jax.ShapeDtypeStruct((depth, batch, GROUP_W, N_MEM), BF16),
            jax.ShapeDtypeStruct((depth, batch, N_HEADS, N_MEM, 2 * HEAD_D), BF16),
        ],
        compiler_params=pltpu.CompilerParams(
            dimension_semantics=("arbitrary", "arbitrary"), vmem_limit_bytes=VMEM_LIMIT_BYTES),
        name="mem_kv",
    )(mem, mem_norm_g.reshape(depth, 1, D_MODEL), w_kv)


def _layer(layer, final, x, kmt, vmx, ng, w_all, w_g, b_g, conv_w, conv_b, mng, pool_w, pool_scale, w_out, fng):
    batch, seq, _ = x.shape
    T = SEQ_TILE
    assert seq % (2 * T) == 0 and T % CHUNK == 0
    tiles_per_seq = seq // T
    pairs_per_seq = tiles_per_seq // 2
    last_tile = batch * tiles_per_seq - 1

    def next_tile(k):
        nt = jnp.minimum(2 * k + 2, last_tile)
        return nt // tiles_per_seq, nt % tiles_per_seq, 0

    kern = functools.partial(_layer_kernel, final=final, tiles_per_seq=tiles_per_seq)
    set_scratch = [
        pltpu.VMEM((CONV_HIST + T, 2 * GROUP_W), F32),
        pltpu.VMEM((POOL_HIST + T, GROUP_W), F32),
        pltpu.VMEM((T, P_COLS), F32),
        pltpu.VMEM((GATE_ROWS, T), F32),
    ]
    return pl.pallas_call(
        kern,
        grid=(batch * pairs_per_seq,),
        in_specs=[
            pl.BlockSpec((None, 2 * T, D_MODEL), lambda k: (k // pairs_per_seq, k % pairs_per_seq, 0)),
            pl.BlockSpec((None, T, D_MODEL), next_tile),
            pl.BlockSpec((None, None, GROUP_W, N_MEM), lambda k: (layer, k // pairs_per_seq, 0, 0)),
            pl.BlockSpec((None, None, N_HEADS, N_MEM, 2 * HEAD_D),
                         lambda k: (layer, k // pairs_per_seq, 0, 0, 0)),
            _resident((1, D_MODEL), layer),
            _resident((D_MODEL, N_PROJ), layer),
            _resident((GATE_ROWS, D_MODEL), layer),
            _resident((2 * N_HEADS, LANES), layer),
            _resident((CONV_K, 2 * GROUP_W), layer),
            _resident((1, 2 * GROUP_W), layer),
            _resident((1, GROUP_W), layer),
            _resident((len(POOL_WINDOWS), LANES, LANES), layer),
            _resident((1, GROUP_W), layer),
            _resident((3 * GROUP_W, D_MODEL), layer),
            pl.BlockSpec((1, D_MODEL), lambda k: (0, 0), pipeline_mode=pl.Buffered(1)),
        ],
        out_specs=pl.BlockSpec((None, 2 * T, D_MODEL), lambda k: (k // pairs_per_seq, k % pairs_per_seq, 0)),
        out_shape=jax.ShapeDtypeStruct(x.shape, F32),
        scratch_shapes=[pltpu.VMEM((T, D_MODEL), BF16)] + set_scratch + set_scratch + [
            pltpu.VMEM((T, GROUP_W), F32),
            pltpu.VMEM((T, GROUP_W), F32),
            pltpu.VMEM((T, N_HEADS * 2 * HEAD_D), BF16),
            pltpu.VMEM((N_HEADS, HEAD_D, 2 * HEAD_D), F32),
            pltpu.VMEM((2 * N_HEADS, LANES), F32),
            pltpu.VMEM((T, 3 * GROUP_W), F32),
        ],
        compiler_params=pltpu.CompilerParams(
            dimension_semantics=("arbitrary",), vmem_limit_bytes=VMEM_LIMIT_BYTES),
        name="trunk_layer_final" if final else "trunk_layer",
    )(x, x, kmt, vmx, ng, w_all, w_g, b_g, conv_w, conv_b, mng, pool_w, pool_scale, w_out, fng)


def kernel(x, mem, norm_g, w_in, b_gates, conv_w, conv_b, mlstm_norm_g, pool_w, pool_scale,
           mem_norm_g, w_mem_kv, w_out, final_norm_g):
    depth = w_in.shape[0]
    w_all, w_g, w_out_b = _prep_weights(w_in, w_out)
    kmt, vmx = _mem_kv(mem, mem_norm_g, w_mem_kv)
    b_g = jnp.broadcast_to(b_gates[:, :, None], (depth, 2 * N_HEADS, LANES))
    args = (kmt, vmx, norm_g.reshape(depth, 1, D_MODEL), w_all, w_g, b_g,
            conv_w, conv_b.reshape(depth, 1, 2 * GROUP_W), mlstm_norm_g.reshape(depth, 1, GROUP_W),
            pool_w, pool_scale.reshape(depth, 1, GROUP_W), w_out_b, final_norm_g.reshape(1, D_MODEL))
    for l in range(depth):
        x = _layer(l, l == depth - 1, x, *args)
    return x
```

```python
import functools

import jax
import jax.numpy as jnp
from jax import lax
from jax.experimental import pallas as pl
from jax.experimental.pallas import tpu as pltpu

D_MODEL = 1024
GROUP_W = 512
N_HEADS = 4
HEAD_D = 128
CONV_K = 4
POOL_WINDOWS = (2, 4, 8, 16)
N_MEM = 256
EPS = 1e-6

SEQ_TILE = 256
CHUNK = 128
CONV_HIST = 8
POOL_HIST = 16
LANES = 128
VMEM_LIMIT_BYTES = 52 * 1024 * 1024

N_IN = 2 * GROUP_W + 3 * GROUP_W + 2 * N_HEADS + 4 * GROUP_W
GATE0 = 5 * GROUP_W
Q0, K0, V0, O0, ZA0, U0, ZB0, QC0, ZC0 = 0, 512, 1024, 1536, 2048, 2560, 3072, 3584, 4096
N_PROJ = 4608
GATE_ROWS = 16
P_V, P_O, P_ZA, P_ZB, P_QC, P_ZC = 0, 512, 1024, 1536, 2048, 2560
P_COLS = 3072

BF16 = jnp.bfloat16
F32 = jnp.float32


def _sigmoid(x):
    return 0.5 * jnp.tanh(0.5 * x) + 0.5


def _silu(x):
    hx = 0.5 * x
    return hx + hx * jnp.tanh(hx)


def _log_sigmoid(x):
    return jnp.minimum(x, 0.0) - jnp.log1p(jnp.exp(-jnp.abs(x)))


def _scan_lanes(x, op, identity):
    n = x.shape[1]
    lanes = lax.broadcasted_iota(jnp.int32, x.shape, 1)
    k = 1
    while k < n:
        x = op(x, jnp.where(lanes >= k, pltpu.roll(x, k, axis=1), identity))
        k *= 2
    return x


def _rmsnorm(x, g):
    ms = jnp.mean(x * x, axis=-1, keepdims=True)
    return (x * lax.rsqrt(ms + EPS)) * g


def _stage_a(x_ref, x_row, first, prev, cur, ng_ref, w_ref, wg_ref, h_ref):
    T = SEQ_TILE
    qk_prev, u_prev, _, _ = prev
    qk_cur, u_cur, proj_cur, g_cur = cur

    def head():
        qk_cur[0:CONV_HIST, :] = jnp.where(first, 0.0, qk_prev[T:T + CONV_HIST, :])
        u_cur[0:POOL_HIST, :] = jnp.where(first, 0.0, u_prev[T:T + POOL_HIST, :])
        h_ref[...] = _rmsnorm(x_ref[x_row:x_row + T, :], ng_ref[...]).astype(BF16)

    def proj(c0):
        return jnp.dot(h_ref[...], w_ref[:, c0:c0 + GROUP_W], preferred_element_type=F32)

    def dot_q():
        qk_cur[CONV_HIST:CONV_HIST + T, 0:GROUP_W] = proj(Q0)

    def dot_k():
        qk_cur[CONV_HIST:CONV_HIST + T, GROUP_W:2 * GROUP_W] = proj(K0)

    def dot_u():
        u_cur[POOL_HIST:POOL_HIST + T, :] = proj(U0)

    def dot_to(src, dst):
        def run():
            proj_cur[:, dst:dst + GROUP_W] = proj(src)
        return run

    def dot_gates():
        g_cur[...] = lax.dot_general(wg_ref[...], h_ref[...], (((1,), (1,)), ((), ())),
                                     preferred_element_type=F32)

    return [head, dot_q, dot_k, dot_u, dot_to(V0, P_V), dot_to(O0, P_O), dot_to(ZA0, P_ZA),
            dot_to(ZB0, P_ZB), dot_to(QC0, P_QC), dot_to(ZC0, P_ZC), dot_gates]


def _stage_b(x_ref, x_row, first, t_seq, cur, o_ref, kmt_ref, vmx_ref, bg_ref, cw_ref, cb_ref, mng_ref,
             pw_ref, ps_ref, wo_ref, fng_ref, q_ref, k_ref, vext_ref, cn_ref, m_ref, y_ref, final):
    T, L = SEQ_TILE, CHUNK
    qk_ext, u_ext, proj_ref, g_ref = cur

    def conv(c0):
        xs = qk_ext[:, c0:c0 + GROUP_W]
        acc = cb_ref[:, c0:c0 + GROUP_W] + cw_ref[CONV_K - 1:CONV_K, c0:c0 + GROUP_W] * xs[CONV_HIST:]
        for j in range(CONV_K - 1):
            back = pltpu.roll(xs, CONV_K - 1 - j, axis=0)[CONV_HIST:]
            acc = acc + cw_ref[j:j + 1, c0:c0 + GROUP_W] * back
        return _silu(acc)

    def conv_q():
        q_ref[...] = conv(0)

    def conv_k():
        k_ref[...] = conv(GROUP_W) * (HEAD_D ** -0.5)
        for hd in range(N_HEADS):
            vext_ref[:, hd * 2 * HEAD_D:hd * 2 * HEAD_D + HEAD_D] = (
                proj_ref[:, P_V + hd * HEAD_D:P_V + (hd + 1) * HEAD_D].astype(BF16))

    def mlstm_chunk(c):
        def run():
            rowi = lax.broadcasted_iota(jnp.int32, (L, L), 0)
            coli = lax.broadcasted_iota(jnp.int32, (L, L), 1)
            causal = coli <= rowi
            r0 = c * L
            g8 = g_ref[0:2 * N_HEADS, r0:r0 + L] + bg_ref[...]
            gi = g8
            logf = _log_sigmoid(pltpu.roll(g8, N_HEADS, axis=0))
            b = _scan_lanes(logf, jnp.add, 0.0)
            r = gi - b
            m_prev = m_ref[...]
            if c == 0:
                m_prev = jnp.where(first, 0.0, m_prev)
            mm = jnp.maximum(_scan_lanes(r, jnp.maximum, -jnp.inf), m_prev)
            a = jnp.exp(m_prev - mm)
            e = jnp.exp(-(mm + b))
            m_ref[...] = jnp.broadcast_to((mm + b)[:, L - 1:L], m_ref.shape)
            cols = jnp.concatenate([mm, a, e, jnp.zeros((LANES - 6 * N_HEADS, L), F32)], axis=0).T
            mm_c, a_c, e_c = (cols[:, i * 2 * N_HEADS:(i + 1) * 2 * N_HEADS] for i in range(3))
            for hd in range(N_HEADS):
                hs = slice(hd * HEAD_D, (hd + 1) * HEAD_D)
                arg = r[hd:hd + 1, :] - mm_c[:, hd:hd + 1]
                p = jnp.exp(jnp.where(causal, arg, -1e30))
                qh = q_ref[r0:r0 + L, hs]
                k_t = k_ref[r0:r0 + L, hs].T
                s = jnp.dot(qh.astype(BF16), k_t.astype(BF16), preferred_element_type=F32) * p
                vx = vext_ref[r0:r0 + L, hd * 2 * HEAD_D:(hd + 1) * 2 * HEAD_D]
                cn = cn_ref[hd]
                if c == 0:
                    cn = jnp.where(first, 0.0, cn)
                lhs = jnp.concatenate([(a_c[:, hd:hd + 1] * qh).astype(BF16), s.astype(BF16)], axis=1)
                rhs = jnp.concatenate([cn.astype(BF16), vx], axis=0)
                nd = jnp.dot(lhs, rhs, preferred_element_type=F32)
                hh = nd[:, 0:HEAD_D] / jnp.maximum(jnp.abs(nd[:, HEAD_D:HEAD_D + 1]), e_c[:, hd:hd + 1])
                k_tw = (k_t * p[L - 1:L, :]).astype(BF16)
                cn_ref[hd] = (a[hd:hd + 1, L - 1:L] * cn
                              + jnp.dot(k_tw, vx, preferred_element_type=F32))
                hg = _sigmoid(proj_ref[r0:r0 + L, P_O + hd * HEAD_D:P_O + (hd + 1) * HEAD_D]) * hh
                hg = hg * lax.rsqrt(jnp.mean(hg * hg, axis=-1, keepdims=True) + EPS)
                hg = hg * mng_ref[:, hs]
                ya = hg * _silu(proj_ref[r0:r0 + L, P_ZA + hd * HEAD_D:P_ZA + (hd + 1) * HEAD_D])
                y_ref[r0:r0 + L, hs] = ya.astype(BF16)
        return run

    def pool():
        tg = (t_seq * T + lax.broadcasted_iota(jnp.int32, (T, LANES), 0) + 1).astype(F32)
        zero_w = jnp.zeros((LANES, LANES), BF16)
        for g0 in range(0, len(POOL_WINDOWS), 2):
            pooled = []
            for g in (g0, g0 + 1):
                win = POOL_WINDOWS[g]
                gs = slice(g * LANES, (g + 1) * LANES)
                ssum = u_ext[:, gs]
                sh = 1
                while sh < win:
                    ssum = ssum + pltpu.roll(ssum, sh, axis=0)
                    sh *= 2
                u = u_ext[POOL_HIST:POOL_HIST + T, gs]
                pooled.append((ssum[POOL_HIST:POOL_HIST + T] / jnp.minimum(tg, float(win)) - u).astype(BF16))
            w_a, w_b = pw_ref[g0].astype(BF16), pw_ref[g0 + 1].astype(BF16)
            w_pair = jnp.concatenate([jnp.concatenate([w_a, zero_w], axis=1),
                                      jnp.concatenate([zero_w, w_b], axis=1)], axis=0)
            yb = jnp.dot(jnp.concatenate(pooled, axis=1), w_pair, preferred_element_type=F32)
            yb = (yb * ps_ref[:, g0 * LANES:(g0 + 2) * LANES]
                  * _silu(proj_ref[:, P_ZB + g0 * LANES:P_ZB + (g0 + 2) * LANES]))
            y_ref[:, GROUP_W + g0 * LANES:GROUP_W + (g0 + 2) * LANES] = yb.astype(BF16)

    def attention():
        for hd in range(N_HEADS):
            hs = slice(hd * HEAD_D, (hd + 1) * HEAD_D)
            qc = proj_ref[:, P_QC + hd * HEAD_D:P_QC + (hd + 1) * HEAD_D].astype(BF16)
            s = jnp.dot(qc, kmt_ref[hs, :], preferred_element_type=F32) * (HEAD_D ** -0.5)
            pexp = jnp.exp(s - jnp.max(s, axis=-1, keepdims=True))
            pv = jnp.dot(pexp.astype(BF16), vmx_ref[hd], preferred_element_type=F32)
            yc = pv[:, 0:HEAD_D] / pv[:, HEAD_D:HEAD_D + 1]
            yc = yc * _silu(proj_ref[:, P_ZC + hd * HEAD_D:P_ZC + (hd + 1) * HEAD_D])
            y_ref[:, 2 * GROUP_W + hd * HEAD_D:2 * GROUP_W + (hd + 1) * HEAD_D] = yc.astype(BF16)

    def out_proj():
        rows = slice(x_row, x_row + T)
        for i in range(D_MODEL // GROUP_W):
            cs = slice(i * GROUP_W, (i + 1) * GROUP_W)
            o_ref[rows, cs] = x_ref[rows, cs] + jnp.dot(y_ref[...], wo_ref[i], preferred_element_type=F32)
        if final:
            o_ref[rows, :] = _rmsnorm(o_ref[rows, :], fng_ref[...])

    return dict(conv_q=conv_q, conv_k=conv_k, chunks=[mlstm_chunk(c) for c in range(T // L)],
                pool=pool, attention=attention, out_proj=out_proj)


def _layer_kernel(xp_ref, xn_ref, kmt_ref, vmx_ref, ng_ref, w_ref, wg_ref, bg_ref, cw_ref, cb_ref, mng_ref,
                  pw_ref, ps_ref, wo_ref, fng_ref, o_ref,
                  h_ref, qk0, u0, proj0, g0, qk1, u1, proj1, g1, q_ref, k_ref, vext_ref, cn_ref, m_ref, y_ref,
                  *, final, tiles_per_seq):
    T = SEQ_TILE
    k = pl.program_id(0)
    set0, set1 = (qk0, u0, proj0, g0), (qk1, u1, proj1, g1)
    a_args = (ng_ref, w_ref, wg_ref, h_ref)
    b_args = (kmt_ref, vmx_ref, bg_ref, cw_ref, cb_ref, mng_ref, pw_ref, ps_ref, wo_ref, fng_ref,
              q_ref, k_ref, vext_ref, cn_ref, m_ref, y_ref, final)

    @pl.when(k == 0)
    def _prologue():
        col = lax.broadcasted_iota(jnp.int32, vext_ref.shape, 1)
        vext_ref[...] = jnp.where(col % (2 * HEAD_D) == HEAD_D, 1.0, 0.0).astype(BF16)
        cn_ref[...] = jnp.zeros(cn_ref.shape, F32)
        m_ref[...] = jnp.zeros(m_ref.shape, F32)
        qk1[T:T + CONV_HIST, :] = jnp.zeros((CONV_HIST, 2 * GROUP_W), F32)
        u1[T:T + POOL_HIST, :] = jnp.zeros((POOL_HIST, GROUP_W), F32)
        for piece in _stage_a(xp_ref, 0, True, set1, set0, *a_args):
            piece()

    t_even = (2 * k) % tiles_per_seq
    first_even = t_even == 0
    first_next = (2 * k + 2) % tiles_per_seq == 0

    def half_step(x_row, first, t_seq, cur, nxt, xa_ref, xa_row, first_a):
        for piece in _stage_a(xa_ref, xa_row, first_a, cur, nxt, *a_args):
            piece()
        b = _stage_b(xp_ref, x_row, first, t_seq, cur, o_ref, *b_args)
        for phase in (b["conv_q"], b["conv_k"], *b["chunks"], b["pool"], b["attention"], b["out_proj"]):
            phase()

    half_step(0, first_even, t_even, set0, set1, xp_ref, T, False)
    half_step(T, False, t_even + 1, set1, set0, xn_ref, 0, first_next)


def _mem_kv_kernel(mem_ref, g_ref, w_ref, kmt_ref, vmx_ref, wb_ref):
    @pl.when(pl.program_id(0) == 0)
    def _cast_weights():
        wb_ref[...] = w_ref[...].astype(BF16)

    mem = mem_ref[...]
    ms = jnp.mean(mem * mem, axis=-1, keepdims=True)
    nrm = mem * lax.rsqrt(ms + EPS)
    col = lax.broadcasted_iota(jnp.int32, (N_MEM, HEAD_D), 1)
    ones_col = jnp.where(col == 0, 1.0, 0.0).astype(BF16)
    for l in range(wb_ref.shape[0]):
        mn = (nrm * g_ref[l]).astype(BF16)
        kv = jnp.dot(mn, wb_ref[l], preferred_element_type=F32)
        for hd in range(N_HEADS):
            hs = slice(hd * HEAD_D, (hd + 1) * HEAD_D)
            kmt_ref[l, hs, :] = kv[:, hs].T.astype(BF16)
            vmx_ref[l, hd, :, 0:HEAD_D] = kv[:, GROUP_W + hd * HEAD_D:GROUP_W + (hd + 1) * HEAD_D].astype(BF16)
            vmx_ref[l, hd, :, HEAD_D:2 * HEAD_D] = ones_col


def _prep_kernel(wint_ref, wout_ref, wall_ref, wg_ref, wo_ref):
    wall_ref[:, 0:GATE0] = wint_ref[0:GATE0, :].T.astype(BF16)
    wall_ref[:, GATE0:N_PROJ] = wint_ref[GATE0 + 2 * N_HEADS:N_IN, :].T.astype(BF16)
    wg_ref[0:2 * N_HEADS, :] = wint_ref[GATE0:GATE0 + 2 * N_HEADS, :].astype(BF16)
    wg_ref[2 * N_HEADS:GATE_ROWS, :] = jnp.zeros((GATE_ROWS - 2 * N_HEADS, wg_ref.shape[1]), BF16)
    for i in range(D_MODEL // GROUP_W):
        wo_ref[i] = wout_ref[:, i * GROUP_W:(i + 1) * GROUP_W].astype(BF16)


def _resident(shape, layer):
    nd = len(shape)
    return pl.BlockSpec((None,) + shape, lambda k: (layer,) + (0,) * nd, pipeline_mode=pl.Buffered(1))


def _prep_weights(w_in, w_out):
    depth = w_in.shape[0]
    steps = 4
    rb_in, rb_out = D_MODEL // steps, 3 * GROUP_W // steps
    return pl.pallas_call(
        _prep_kernel,
        grid=(depth, steps),
        in_specs=[
            pl.BlockSpec((None, N_IN, rb_in), lambda l, i: (l, 0, i)),
            pl.BlockSpec((None, rb_out, D_MODEL), lambda l, i: (l, i, 0)),
        ],
        out_specs=[
            pl.BlockSpec((None, rb_in, N_PROJ), lambda l, i: (l, i, 0)),
            pl.BlockSpec((None, GATE_ROWS, rb_in), lambda l, i: (l, 0, i)),
            pl.BlockSpec((None, D_MODEL // GROUP_W, rb_out, GROUP_W), lambda l, i: (l, 0, i, 0)),
        ],
        out_shape=[
            jax.ShapeDtypeStruct((depth, D_MODEL, N_PROJ), BF16),
            jax.ShapeDtypeStruct((depth, GATE_ROWS, D_MODEL), BF16),
            jax.ShapeDtypeStruct((depth, D_MODEL // GROUP_W, 3 * GROUP_W, GROUP_W), BF16),
        ],
        compiler_params=pltpu.CompilerParams(
            dimension_semantics=("arbitrary", "arbitrary"), vmem_limit_bytes=VMEM_LIMIT_BYTES),
        name="prep_weights",
    )(jnp.swapaxes(w_in, 1, 2), w_out)


def _mem_kv(mem, mem_norm_g, w_kv):
    depth, batch = w_kv.shape[0], mem.shape[0]
    return pl.pallas_call(
        _mem_kv_kernel,
        grid=(batch,),
        in_specs=[
            pl.BlockSpec((None, N_MEM, D_MODEL), lambda b: (b, 0, 0)),
            pl.BlockSpec((depth, 1, D_MODEL), lambda b: (0, 0, 0)),
            pl.BlockSpec((depth, D_MODEL, 2 * GROUP_W), lambda b: (0, 0, 0), pipeline_mode=pl.Buffered(1)),
        ],
        out_specs=[
            pl.BlockSpec((depth, None, GROUP_W, N_MEM), lambda b: (0, b, 0, 0)),
            pl.BlockSpec((depth, None, N_HEADS, N_MEM, 2 * HEAD_D), lambda b: (0, b, 0, 0, 0)),
        ],
        out_shape=[
            jax.ShapeDtypeStruct((depth, batch, GROUP_W, N_MEM), BF16),
            jax.ShapeDtypeStruct((depth, batch, N_HEADS, N_MEM, 2 * HEAD_D), BF16),
        ],
        scratch_shapes=[pltpu.VMEM((depth, D_MODEL, 2 * GROUP_W), BF16)],
        compiler_params=pltpu.CompilerParams(
            dimension_semantics=("arbitrary",), vmem_limit_bytes=VMEM_LIMIT_BYTES),
        name="mem_kv",
    )(mem, mem_norm_g.reshape(depth, 1, D_MODEL), w_kv)


def _layer(layer, final, x, kmt, vmx, ng, w_all, w_g, b_g, conv_w, conv_b, mng, pool_w, pool_scale, w_out, fng):
    batch, seq, _ = x.shape
    T = SEQ_TILE
    assert seq % (2 * T) == 0 and T % CHUNK == 0
    tiles_per_seq = seq // T
    pairs_per_seq = tiles_per_seq // 2
    last_tile = batch * tiles_per_seq - 1

    def next_tile(k):
        nt = jnp.minimum(2 * k + 2, last_tile)
        return nt // tiles_per_seq, nt % tiles_per_seq, 0

    kern = functools.partial(_layer_kernel, final=final, tiles_per_seq=tiles_per_seq)
    set_scratch = [
        pltpu.VMEM((CONV_HIST + T, 2 * GROUP_W), F32),
        pltpu.VMEM((POOL_HIST + T, GROUP_W), F32),
        pltpu.VMEM((T, P_COLS), F32),
        pltpu.VMEM((GATE_ROWS, T), F32),
    ]
    return pl.pallas_call(
        kern,
        grid=(batch * pairs_per_seq,),
        in_specs=[
            pl.BlockSpec((None, 2 * T, D_MODEL), lambda k: (k // pairs_per_seq, k % pairs_per_seq, 0)),
            pl.BlockSpec((None, T, D_MODEL), next_tile),
            pl.BlockSpec((None, None, GROUP_W, N_MEM), lambda k: (layer, k // pairs_per_seq, 0, 0)),
            pl.BlockSpec((None, None, N_HEADS, N_MEM, 2 * HEAD_D),
                         lambda k: (layer, k // pairs_per_seq, 0, 0, 0)),
            _resident((1, D_MODEL), layer),
            _resident((D_MODEL, N_PROJ), layer),
            _resident((GATE_ROWS, D_MODEL), layer),
            _resident((2 * N_HEADS, LANES), layer),
            _resident((CONV_K, 2 * GROUP_W), layer),
            _resident((1, 2 * GROUP_W), layer),
            _resident((1, GROUP_W), layer),
            _resident((len(POOL_WINDOWS), LANES, LANES), layer),
            _resident((1, GROUP_W), layer),
            _resident((D_MODEL // GROUP_W, 3 * GROUP_W, GROUP_W), layer),
            pl.BlockSpec((1, D_MODEL), lambda k: (0, 0), pipeline_mode=pl.Buffered(1)),
        ],
        out_specs=pl.BlockSpec((None, 2 * T, D_MODEL), lambda k: (k // pairs_per_seq, k % pairs_per_seq, 0)),
        out_shape=jax.ShapeDtypeStruct(x.shape, F32),
        scratch_shapes=[pltpu.VMEM((T, D_MODEL), BF16)] + set_scratch + set_scratch + [
            pltpu.VMEM((T, GROUP_W), F32),
            pltpu.VMEM((T, GROUP_W), F32),
            pltpu.VMEM((T, N_HEADS * 2 * HEAD_D), BF16),
            pltpu.VMEM((N_HEADS, HEAD_D, 2 * HEAD_D), F32),
            pltpu.VMEM((2 * N_HEADS, LANES), F32),
            pltpu.VMEM((T, 3 * GROUP_W), BF16),
        ],
        compiler_params=pltpu.CompilerParams(
            dimension_semantics=("arbitrary",), vmem_limit_bytes=VMEM_LIMIT_BYTES),
        name="trunk_layer_final" if final else "trunk_layer",
    )(x, x, kmt, vmx, ng, w_all, w_g, b_g, conv_w, conv_b, mng, pool_w, pool_scale, w_out, fng)


def kernel(x, mem, norm_g, w_in, b_gates, conv_w, conv_b, mlstm_norm_g, pool_w, pool_scale,
           mem_norm_g, w_mem_kv, w_out, final_norm_g):
    depth = w_in.shape[0]
    w_all, w_g, w_out_b = _prep_weights(w_in, w_out)
    kmt, vmx = _mem_kv(mem, mem_norm_g, w_mem_kv)
    b_g = jnp.broadcast_to(b_gates[:, :, None], (depth, 2 * N_HEADS, LANES))
    args = (kmt, vmx, norm_g.reshape(depth, 1, D_MODEL), w_all, w_g, b_g,
            conv_w, conv_b.reshape(depth, 1, 2 * GROUP_W), mlstm_norm_g.reshape(depth, 1, GROUP_W),
            pool_w, pool_scale.reshape(depth, 1, GROUP_W), w_out_b, final_norm_g.reshape(1, D_MODEL))
    for l in range(depth):
        x = _layer(l, l == depth - 1, x, *args)
    return x
```

```python
import functools

import jax
import jax.numpy as jnp
from jax import lax
from jax.experimental import pallas as pl
from jax.experimental.pallas import tpu as pltpu

D_MODEL = 1024
GROUP_W = 512
N_HEADS = 4
HEAD_D = 128
CONV_K = 4
POOL_WINDOWS = (2, 4, 8, 16)
N_MEM = 256
EPS = 1e-6

SEQ_TILE = 256
CHUNK = 256
CONV_HIST = 8
POOL_HIST = 16
LANES = 128
VMEM_LIMIT_BYTES = 52 * 1024 * 1024

N_IN = 2 * GROUP_W + 3 * GROUP_W + 2 * N_HEADS + 4 * GROUP_W
GATE0 = 5 * GROUP_W
Q0, K0, V0, O0, ZA0, U0, ZB0, QC0, ZC0 = 0, 512, 1024, 1536, 2048, 2560, 3072, 3584, 4096
N_PROJ = 4608
GATE_ROWS = 16
P_V, P_O, P_ZA, P_ZB, P_QC, P_ZC = 0, 512, 1024, 1536, 2048, 2560
P_COLS = 3072

BF16 = jnp.bfloat16
F32 = jnp.float32


def _sigmoid(x):
    return 0.5 * jnp.tanh(0.5 * x) + 0.5


def _silu(x):
    hx = 0.5 * x
    return hx + hx * jnp.tanh(hx)


def _log_sigmoid(x):
    return jnp.minimum(x, 0.0) - jnp.log1p(jnp.exp(-jnp.abs(x)))


def _scan_lanes(x, op, identity):
    n = x.shape[1]
    lanes = lax.broadcasted_iota(jnp.int32, x.shape, 1)
    k = 1
    while k < n:
        x = op(x, jnp.where(lanes >= k, pltpu.roll(x, k, axis=1), identity))
        k *= 2
    return x


def _rmsnorm(x, g):
    ms = jnp.mean(x * x, axis=-1, keepdims=True)
    return (x * lax.rsqrt(ms + EPS)) * g


def _stage_a(x_ref, x_row, first, prev, cur, ng_ref, w_ref, wg_ref, h_ref):
    T = SEQ_TILE
    qk_prev, u_prev, _, _ = prev
    qk_cur, u_cur, proj_cur, g_cur = cur

    def head():
        qk_cur[0:CONV_HIST, :] = jnp.where(first, 0.0, qk_prev[T:T + CONV_HIST, :])
        u_cur[0:POOL_HIST, :] = jnp.where(first, 0.0, u_prev[T:T + POOL_HIST, :])
        h_ref[...] = _rmsnorm(x_ref[x_row:x_row + T, :], ng_ref[...]).astype(BF16)

    def proj(c0):
        return jnp.dot(h_ref[...], w_ref[:, c0:c0 + GROUP_W], preferred_element_type=F32)

    def dot_q():
        qk_cur[CONV_HIST:CONV_HIST + T, 0:GROUP_W] = proj(Q0)

    def dot_k():
        qk_cur[CONV_HIST:CONV_HIST + T, GROUP_W:2 * GROUP_W] = proj(K0)

    def dot_u():
        u_cur[POOL_HIST:POOL_HIST + T, :] = proj(U0)

    def dot_to(src, dst):
        def run():
            proj_cur[:, dst:dst + GROUP_W] = proj(src)
        return run

    def dot_gates():
        g_cur[...] = lax.dot_general(wg_ref[...], h_ref[...], (((1,), (1,)), ((), ())),
                                     preferred_element_type=F32)

    return [head, dot_q, dot_k, dot_u, dot_to(V0, P_V), dot_to(O0, P_O), dot_to(ZA0, P_ZA),
            dot_to(ZB0, P_ZB), dot_to(QC0, P_QC), dot_to(ZC0, P_ZC), dot_gates]


def _stage_b(x_ref, x_row, first, t_seq, cur, o_ref, kmt_ref, vmx_ref, bg_ref, cw_ref, cb_ref, mng_ref,
             pw_ref, ps_ref, wo_ref, fng_ref, q_ref, k_ref, vext_ref, cn_ref, m_ref, y_ref, final):
    T, L = SEQ_TILE, CHUNK
    qk_ext, u_ext, proj_ref, g_ref = cur

    def conv(c0):
        xs = qk_ext[:, c0:c0 + GROUP_W]
        acc = cb_ref[:, c0:c0 + GROUP_W] + cw_ref[CONV_K - 1:CONV_K, c0:c0 + GROUP_W] * xs[CONV_HIST:]
        for j in range(CONV_K - 1):
            back = pltpu.roll(xs, CONV_K - 1 - j, axis=0)[CONV_HIST:]
            acc = acc + cw_ref[j:j + 1, c0:c0 + GROUP_W] * back
        return _silu(acc)

    def conv_q():
        q_ref[...] = conv(0)

    def conv_k():
        k_ref[...] = conv(GROUP_W) * (HEAD_D ** -0.5)
        for hd in range(N_HEADS):
            vext_ref[:, hd * 2 * HEAD_D:hd * 2 * HEAD_D + HEAD_D] = (
                proj_ref[:, P_V + hd * HEAD_D:P_V + (hd + 1) * HEAD_D].astype(BF16))

    def mlstm_chunk(c):
        def run():
            rowi = lax.broadcasted_iota(jnp.int32, (L, L), 0)
            coli = lax.broadcasted_iota(jnp.int32, (L, L), 1)
            causal = coli <= rowi
            r0 = c * L
            g8 = g_ref[0:2 * N_HEADS, r0:r0 + L] + bg_ref[:, 0:1]
            gi = g8
            logf = _log_sigmoid(pltpu.roll(g8, N_HEADS, axis=0))
            b = _scan_lanes(logf, jnp.add, 0.0)
            r = gi - b
            m_prev = m_ref[:, 0:1]
            if c == 0:
                m_prev = jnp.where(first, 0.0, m_prev)
            mm = jnp.maximum(_scan_lanes(r, jnp.maximum, -jnp.inf), m_prev)
            a = jnp.exp(m_prev - mm)
            e = jnp.exp(-(mm + b))
            m_ref[...] = jnp.broadcast_to((mm + b)[:, L - 1:L], m_ref.shape)
            cols = jnp.concatenate([mm, a, e, jnp.zeros((LANES - 6 * N_HEADS, L), F32)], axis=0).T
            mm_c, a_c, e_c = (cols[:, i * 2 * N_HEADS:(i + 1) * 2 * N_HEADS] for i in range(3))
            for hd in range(N_HEADS):
                hs = slice(hd * HEAD_D, (hd + 1) * HEAD_D)
                arg = r[hd:hd + 1, :] - mm_c[:, hd:hd + 1]
                p = jnp.exp(jnp.where(causal, arg, -1e30))
                qh = q_ref[r0:r0 + L, hs]
                k_t = k_ref[r0:r0 + L, hs].T
                s = jnp.dot(qh.astype(BF16), k_t.astype(BF16), preferred_element_type=F32) * p
                vx = vext_ref[r0:r0 + L, hd * 2 * HEAD_D:(hd + 1) * 2 * HEAD_D]
                cn = cn_ref[hd]
                if c == 0:
                    cn = jnp.where(first, 0.0, cn)
                lhs = jnp.concatenate([(a_c[:, hd:hd + 1] * qh).astype(BF16), s.astype(BF16)], axis=1)
                rhs = jnp.concatenate([cn.astype(BF16), vx], axis=0)
                nd = jnp.dot(lhs, rhs, preferred_element_type=F32)
                hh = nd[:, 0:HEAD_D] / jnp.maximum(jnp.abs(nd[:, HEAD_D:HEAD_D + 1]), e_c[:, hd:hd + 1])
                k_tw = (k_t * p[L - 1:L, :]).astype(BF16)
                cn_ref[hd] = (a[hd:hd + 1, L - 1:L] * cn
                              + jnp.dot(k_tw, vx, preferred_element_type=F32))
                hg = _sigmoid(proj_ref[r0:r0 + L, P_O + hd * HEAD_D:P_O + (hd + 1) * HEAD_D]) * hh
                hg = hg * lax.rsqrt(jnp.mean(hg * hg, axis=-1, keepdims=True) + EPS)
                hg = hg * mng_ref[:, hs]
                ya = hg * _silu(proj_ref[r0:r0 + L, P_ZA + hd * HEAD_D:P_ZA + (hd + 1) * HEAD_D])
                y_ref[r0:r0 + L, hs] = ya.astype(BF16)
        return run

    def pool():
        tg = (t_seq * T + lax.broadcasted_iota(jnp.int32, (T, LANES), 0) + 1).astype(F32)
        zero_w = jnp.zeros((LANES, LANES), BF16)
        for g0 in range(0, len(POOL_WINDOWS), 2):
            pooled = []
            for g in (g0, g0 + 1):
                win = POOL_WINDOWS[g]
                gs = slice(g * LANES, (g + 1) * LANES)
                ssum = u_ext[:, gs]
                sh = 1
                while sh < win:
                    ssum = ssum + pltpu.roll(ssum, sh, axis=0)
                    sh *= 2
                u = u_ext[POOL_HIST:POOL_HIST + T, gs]
                pooled.append((ssum[POOL_HIST:POOL_HIST + T] / jnp.minimum(tg, float(win)) - u).astype(BF16))
            w_a, w_b = pw_ref[g0].astype(BF16), pw_ref[g0 + 1].astype(BF16)
            w_pair = jnp.concatenate([jnp.concatenate([w_a, zero_w], axis=1),
                                      jnp.concatenate([zero_w, w_b], axis=1)], axis=0)
            yb = jnp.dot(jnp.concatenate(pooled, axis=1), w_pair, preferred_element_type=F32)
            yb = (yb * ps_ref[:, g0 * LANES:(g0 + 2) * LANES]
                  * _silu(proj_ref[:, P_ZB + g0 * LANES:P_ZB + (g0 + 2) * LANES]))
            y_ref[:, GROUP_W + g0 * LANES:GROUP_W + (g0 + 2) * LANES] = yb.astype(BF16)

    def attention():
        for hd in range(N_HEADS):
            hs = slice(hd * HEAD_D, (hd + 1) * HEAD_D)
            qc = proj_ref[:, P_QC + hd * HEAD_D:P_QC + (hd + 1) * HEAD_D].astype(BF16)
            s = jnp.dot(qc, kmt_ref[hs, :], preferred_element_type=F32) * (HEAD_D ** -0.5)
            pexp = jnp.exp(s - jnp.max(s, axis=-1, keepdims=True))
            pv = jnp.dot(pexp.astype(BF16), vmx_ref[hd], preferred_element_type=F32)
            yc = pv[:, 0:HEAD_D] / pv[:, HEAD_D:HEAD_D + 1]
            yc = yc * _silu(proj_ref[:, P_ZC + hd * HEAD_D:P_ZC + (hd + 1) * HEAD_D])
            y_ref[:, 2 * GROUP_W + hd * HEAD_D:2 * GROUP_W + (hd + 1) * HEAD_D] = yc.astype(BF16)

    def out_proj():
        rows = slice(x_row, x_row + T)
        for i in range(D_MODEL // GROUP_W):
            cs = slice(i * GROUP_W, (i + 1) * GROUP_W)
            o_ref[rows, cs] = x_ref[rows, cs] + jnp.dot(y_ref[...], wo_ref[i], preferred_element_type=F32)
        if final:
            o_ref[rows, :] = _rmsnorm(o_ref[rows, :], fng_ref[...])

    return dict(conv_q=conv_q, conv_k=conv_k, chunks=[mlstm_chunk(c) for c in range(T // L)],
                pool=pool, attention=attention, out_proj=out_proj)


def _layer_kernel(xp_ref, xn_ref, kmt_ref, vmx_ref, ng_ref, w_ref, wg_ref, bg_ref, cw_ref, cb_ref, mng_ref,
                  pw_ref, ps_ref, wo_ref, fng_ref, o_ref,
                  h_ref, qk0, u0, proj0, g0, qk1, u1, proj1, g1, q_ref, k_ref, vext_ref, cn_ref, m_ref, y_ref,
                  *, final, tiles_per_seq):
    T = SEQ_TILE
    k = pl.program_id(0)
    set0, set1 = (qk0, u0, proj0, g0), (qk1, u1, proj1, g1)
    a_args = (ng_ref, w_ref, wg_ref, h_ref)
    b_args = (kmt_ref, vmx_ref, bg_ref, cw_ref, cb_ref, mng_ref, pw_ref, ps_ref, wo_ref, fng_ref,
              q_ref, k_ref, vext_ref, cn_ref, m_ref, y_ref, final)

    @pl.when(k == 0)
    def _prologue():
        col = lax.broadcasted_iota(jnp.int32, vext_ref.shape, 1)
        vext_ref[...] = jnp.where(col % (2 * HEAD_D) == HEAD_D, 1.0, 0.0).astype(BF16)
        cn_ref[...] = jnp.zeros(cn_ref.shape, F32)
        m_ref[...] = jnp.zeros(m_ref.shape, F32)
        qk1[T:T + CONV_HIST, :] = jnp.zeros((CONV_HIST, 2 * GROUP_W), F32)
        u1[T:T + POOL_HIST, :] = jnp.zeros((POOL_HIST, GROUP_W), F32)
        for piece in _stage_a(xp_ref, 0, True, set1, set0, *a_args):
            piece()

    t_even = (2 * k) % tiles_per_seq
    first_even = t_even == 0
    first_next = (2 * k + 2) % tiles_per_seq == 0

    def half_step(x_row, first, t_seq, cur, nxt, xa_ref, xa_row, first_a):
        for piece in _stage_a(xa_ref, xa_row, first_a, cur, nxt, *a_args):
            piece()
        b = _stage_b(xp_ref, x_row, first, t_seq, cur, o_ref, *b_args)
        for phase in (b["conv_q"], b["conv_k"], *b["chunks"], b["pool"], b["attention"], b["out_proj"]):
            phase()

    half_step(0, first_even, t_even, set0, set1, xp_ref, T, False)
    half_step(T, False, t_even + 1, set1, set0, xn_ref, 0, first_next)


def _mem_kv_kernel(mem_ref, g_ref, w_ref, kmt_ref, vmx_ref, wb_ref):
    @pl.when(pl.program_id(0) == 0)
    def _cast_weights():
        wb_ref[...] = w_ref[...].astype(BF16)

    mem = mem_ref[...]
    ms = jnp.mean(mem * mem, axis=-1, keepdims=True)
    nrm = mem * lax.rsqrt(ms + EPS)
    col = lax.broadcasted_iota(jnp.int32, (N_MEM, HEAD_D), 1)
    ones_col = jnp.where(col == 0, 1.0, 0.0).astype(BF16)
    for l in range(wb_ref.shape[0]):
        mn = (nrm * g_ref[l]).astype(BF16)
        kv = jnp.dot(mn, wb_ref[l], preferred_element_type=F32)
        for hd in range(N_HEADS):
            hs = slice(hd * HEAD_D, (hd + 1) * HEAD_D)
            kmt_ref[l, hs, :] = kv[:, hs].T.astype(BF16)
            vmx_ref[l, hd, :, 0:HEAD_D] = kv[:, GROUP_W + hd * HEAD_D:GROUP_W + (hd + 1) * HEAD_D].astype(BF16)
            vmx_ref[l, hd, :, HEAD_D:2 * HEAD_D] = ones_col


def _prep_kernel(wint_ref, wout_ref, wall_ref, wg_ref, wo_ref):
    wall_ref[:, 0:GATE0] = wint_ref[0:GATE0, :].T.astype(BF16)
    wall_ref[:, GATE0:N_PROJ] = wint_ref[GATE0 + 2 * N_HEADS:N_IN, :].T.astype(BF16)
    wg_ref[0:2 * N_HEADS, :] = wint_ref[GATE0:GATE0 + 2 * N_HEADS, :].astype(BF16)
    wg_ref[2 * N_HEADS:GATE_ROWS, :] = jnp.zeros((GATE_ROWS - 2 * N_HEADS, wg_ref.shape[1]), BF16)
    for i in range(D_MODEL // GROUP_W):
        wo_ref[i] = wout_ref[:, i * GROUP_W:(i + 1) * GROUP_W].astype(BF16)


def _resident(shape, layer):
    nd = len(shape)
    return pl.BlockSpec((None,) + shape, lambda k: (layer,) + (0,) * nd, pipeline_mode=pl.Buffered(1))


def _prep_weights(w_in, w_out):
    depth = w_in.shape[0]
    steps = 4
    rb_in, rb_out = D_MODEL // steps, 3 * GROUP_W // steps
    return pl.pallas_call(
        _prep_kernel,
        grid=(depth, steps),
        in_specs=[
            pl.BlockSpec((None, N_IN, rb_in), lambda l, i: (l, 0, i)),
            pl.BlockSpec((None, rb_out, D_MODEL), lambda l, i: (l, i, 0)),
        ],
        out_specs=[
            pl.BlockSpec((None, rb_in, N_PROJ), lambda l, i: (l, i, 0)),
            pl.BlockSpec((None, GATE_ROWS, rb_in), lambda l, i: (l, 0, i)),
            pl.BlockSpec((None, D_MODEL // GROUP_W, rb_out, GROUP_W), lambda l, i: (l, 0, i, 0)),
        ],
        out_shape=[
            jax.ShapeDtypeStruct((depth, D_MODEL, N_PROJ), BF16),
            jax.ShapeDtypeStruct((depth, GATE_ROWS, D_MODEL), BF16),
            jax.ShapeDtypeStruct((depth, D_MODEL // GROUP_W, 3 * GROUP_W, GROUP_W), BF16),
        ],
        compiler_params=pltpu.CompilerParams(
            dimension_semantics=("arbitrary", "arbitrary"), vmem_limit_bytes=VMEM_LIMIT_BYTES),
        name="prep_weights",
    )(jnp.swapaxes(w_in, 1, 2), w_out)


def _mem_kv(mem, mem_norm_g, w_kv):
    depth, batch = w_kv.shape[0], mem.shape[0]
    return pl.pallas_call(
        _mem_kv_kernel,
        grid=(batch,),
        in_specs=[
            pl.BlockSpec((None, N_MEM, D_MODEL), lambda b: (b, 0, 0)),
            pl.BlockSpec((depth, 1, D_MODEL), lambda b: (0, 0, 0)),
            pl.BlockSpec((depth, D_MODEL, 2 * GROUP_W), lambda b: (0, 0, 0), pipeline_mode=pl.Buffered(1)),
        ],
        out_specs=[
            pl.BlockSpec((depth, None, GROUP_W, N_MEM), lambda b: (0, b, 0, 0)),
            pl.BlockSpec((depth, None, N_HEADS, N_MEM, 2 * HEAD_D), lambda b: (0, b, 0, 0, 0)),
        ],
        out_shape=[
            jax.ShapeDtypeStruct((depth, batch, GROUP_W, N_MEM), BF16),
            jax.ShapeDtypeStruct((depth, batch, N_HEADS, N_MEM, 2 * HEAD_D), BF16),
        ],
        scratch_shapes=[pltpu.VMEM((depth, D_MODEL, 2 * GROUP_W), BF16)],
        compiler_params=pltpu.CompilerParams(
            dimension_semantics=("arbitrary",), vmem_limit_bytes=VMEM_LIMIT_BYTES),
        name="mem_kv",
    )(mem, mem_norm_g.reshape(depth, 1, D_MODEL), w_kv)


def _layer(layer, final, x, kmt, vmx, ng, w_all, w_g, b_g, conv_w, conv_b, mng, pool_w, pool_scale, w_out, fng):
    batch, seq, _ = x.shape
    T = SEQ_TILE
    assert seq % (2 * T) == 0 and T % CHUNK == 0
    tiles_per_seq = seq // T
    pairs_per_seq = tiles_per_seq // 2
    last_tile = batch * tiles_per_seq - 1

    def next_tile(k):
        nt = jnp.minimum(2 * k + 2, last_tile)
        return nt // tiles_per_seq, nt % tiles_per_seq, 0

    kern = functools.partial(_layer_kernel, final=final, tiles_per_seq=tiles_per_seq)
    set_scratch = [
        pltpu.VMEM((CONV_HIST + T, 2 * GROUP_W), F32),
        pltpu.VMEM((POOL_HIST + T, GROUP_W), F32),
        pltpu.VMEM((T, P_COLS), F32),
        pltpu.VMEM((GATE_ROWS, T), F32),
    ]
    return pl.pallas_call(
        kern,
        grid=(batch * pairs_per_seq,),
        in_specs=[
            pl.BlockSpec((None, 2 * T, D_MODEL), lambda k: (k // pairs_per_seq, k % pairs_per_seq, 0)),
            pl.BlockSpec((None, T, D_MODEL), next_tile),
            pl.BlockSpec((None, None, GROUP_W, N_MEM), lambda k: (layer, k // pairs_per_seq, 0, 0)),
            pl.BlockSpec((None, None, N_HEADS, N_MEM, 2 * HEAD_D),
                         lambda k: (layer, k // pairs_per_seq, 0, 0, 0)),
            _resident((1, D_MODEL), layer),
            _resident((D_MODEL, N_PROJ), layer),
            _resident((GATE_ROWS, D_MODEL), layer),
            _resident((2 * N_HEADS, LANES), layer),
            _resident((CONV_K, 2 * GROUP_W), layer),
            _resident((1, 2 * GROUP_W), layer),
            _resident((1, GROUP_W), layer),
            _resident((len(POOL_WINDOWS), LANES, LANES), layer),
            _resident((1, GROUP_W), layer),
            _resident((D_MODEL // GROUP_W, 3 * GROUP_W, GROUP_W), layer),
            pl.BlockSpec((1, D_MODEL), lambda k: (0, 0), pipeline_mode=pl.Buffered(1)),
        ],
        out_specs=pl.BlockSpec((None, 2 * T, D_MODEL), lambda k: (k // pairs_per_seq, k % pairs_per_seq, 0)),
        out_shape=jax.ShapeDtypeStruct(x.shape, F32),
        scratch_shapes=[pltpu.VMEM((T, D_MODEL), BF16)] + set_scratch + set_scratch + [
            pltpu.VMEM((T, GROUP_W), F32),
            pltpu.VMEM((T, GROUP_W), F32),
            pltpu.VMEM((T, N_HEADS * 2 * HEAD_D), BF16),
            pltpu.VMEM((N_HEADS, HEAD_D, 2 * HEAD_D), F32),
            pltpu.VMEM((2 * N_HEADS, LANES), F32),
            pltpu.VMEM((T, 3 * GROUP_W), BF16),
        ],
        compiler_params=pltpu.CompilerParams(
            dimension_semantics=("arbitrary",), vmem_limit_bytes=VMEM_LIMIT_BYTES),
        name="trunk_layer_final" if final else "trunk_layer",
    )(x, x, kmt, vmx, ng, w_all, w_g, b_g, conv_w, conv_b, mng, pool_w, pool_scale, w_out, fng)


def kernel(x, mem, norm_g, w_in, b_gates, conv_w, conv_b, mlstm_norm_g, pool_w, pool_scale,
           mem_norm_g, w_mem_kv, w_out, final_norm_g):
    depth = w_in.shape[0]
    w_all, w_g, w_out_b = _prep_weights(w_in, w_out)
    kmt, vmx = _mem_kv(mem, mem_norm_g, w_mem_kv)
    b_g = jnp.broadcast_to(b_gates[:, :, None], (depth, 2 * N_HEADS, LANES))
    args = (kmt, vmx, norm_g.reshape(depth, 1, D_MODEL), w_all, w_g, b_g,
            conv_w, conv_b.reshape(depth, 1, 2 * GROUP_W), mlstm_norm_g.reshape(depth, 1, GROUP_W),
            pool_w, pool_scale.reshape(depth, 1, GROUP_W), w_out_b, final_norm_g.reshape(1, D_MODEL))
    for l in range(depth):
        x = _layer(l, l == depth - 1, x, *args)
    return x
```

```python
import functools

import jax
import jax.numpy as jnp
from jax import lax
from jax.experimental import pallas as pl
from jax.experimental.pallas import tpu as pltpu

D_MODEL = 1024
GROUP_W = 512
N_HEADS = 4
HEAD_D = 128
CONV_K = 4
POOL_WINDOWS = (2, 4, 8, 16)
N_MEM = 256
EPS = 1e-6

SEQ_TILE = 256
CHUNK = 256
CONV_HIST = 8
POOL_HIST = 16
LANES = 128
VMEM_LIMIT_BYTES = 52 * 1024 * 1024

N_IN = 2 * GROUP_W + 3 * GROUP_W + 2 * N_HEADS + 4 * GROUP_W
GATE0 = 5 * GROUP_W
Q0, K0, V0, O0, ZA0, U0, ZB0, QC0, ZC0 = 0, 512, 1024, 1536, 2048, 2560, 3072, 3584, 4096
N_PROJ = 4608
GATE_ROWS = 16
P_V, P_O, P_ZA, P_ZB, P_QC, P_ZC = 0, 512, 1024, 1536, 2048, 2560
P_COLS = 3072

BF16 = jnp.bfloat16
F32 = jnp.float32


def _sigmoid(x):
    return 0.5 * jnp.tanh(0.5 * x) + 0.5


def _silu(x):
    hx = 0.5 * x
    return hx + hx * jnp.tanh(hx)


def _log_sigmoid(x):
    return jnp.minimum(x, 0.0) - jnp.log1p(jnp.exp(-jnp.abs(x)))


def _scan_lanes(x, op, identity):
    n = x.shape[1]
    lanes = lax.broadcasted_iota(jnp.int32, x.shape, 1)
    k = 1
    while k < n:
        x = op(x, jnp.where(lanes >= k, pltpu.roll(x, k, axis=1), identity))
        k *= 2
    return x


def _rmsnorm(x, g):
    ms = jnp.mean(x * x, axis=-1, keepdims=True)
    return (x * lax.rsqrt(ms + EPS)) * g


def _stage_a(x_ref, x_row, first, prev, cur, ng_ref, w_ref, wg_ref, h_ref):
    T = SEQ_TILE
    qk_prev, u_prev, _, _ = prev
    qk_cur, u_cur, proj_cur, g_cur = cur

    def head():
        qk_cur[0:CONV_HIST, :] = jnp.where(first, 0.0, qk_prev[T:T + CONV_HIST, :])
        u_cur[0:POOL_HIST, :] = jnp.where(first, 0.0, u_prev[T:T + POOL_HIST, :])
        h_ref[...] = _rmsnorm(x_ref[x_row:x_row + T, :], ng_ref[...]).astype(BF16)

    def proj(c0):
        return jnp.dot(h_ref[...], w_ref[:, c0:c0 + GROUP_W], preferred_element_type=F32)

    def dot_q():
        qk_cur[CONV_HIST:CONV_HIST + T, 0:GROUP_W] = proj(Q0)

    def dot_k():
        qk_cur[CONV_HIST:CONV_HIST + T, GROUP_W:2 * GROUP_W] = proj(K0)

    def dot_u():
        u_cur[POOL_HIST:POOL_HIST + T, :] = proj(U0)

    def dot_to(src, dst):
        def run():
            proj_cur[:, dst:dst + GROUP_W] = proj(src)
        return run

    def dot_gates():
        g_cur[...] = lax.dot_general(wg_ref[...], h_ref[...], (((1,), (1,)), ((), ())),
                                     preferred_element_type=F32)

    return [head, dot_q, dot_k, dot_u, dot_to(V0, P_V), dot_to(O0, P_O), dot_to(ZA0, P_ZA),
            dot_to(ZB0, P_ZB), dot_to(QC0, P_QC), dot_to(ZC0, P_ZC), dot_gates]


def _stage_b(x_ref, x_row, first, t_seq, cur, o_ref, kmt_ref, vmx_ref, bg_ref, cw_ref, cb_ref, mng_ref,
             pw_ref, ps_ref, wo_ref, fng_ref, q_ref, k_ref, vext_ref, cn_ref, m_ref, y_ref, final):
    T, L = SEQ_TILE, CHUNK
    qk_ext, u_ext, proj_ref, g_ref = cur

    def conv(c0):
        xs = qk_ext[:, c0:c0 + GROUP_W]
        acc = cb_ref[:, c0:c0 + GROUP_W] + cw_ref[CONV_K - 1:CONV_K, c0:c0 + GROUP_W] * xs[CONV_HIST:]
        for j in range(CONV_K - 1):
            back = pltpu.roll(xs, CONV_K - 1 - j, axis=0)[CONV_HIST:]
            acc = acc + cw_ref[j:j + 1, c0:c0 + GROUP_W] * back
        return _silu(acc)

    def conv_q():
        q_ref[...] = conv(0)

    def conv_k():
        k_ref[...] = conv(GROUP_W) * (HEAD_D ** -0.5)
        for hd in range(N_HEADS):
            vext_ref[:, hd * 2 * HEAD_D:hd * 2 * HEAD_D + HEAD_D] = (
                proj_ref[:, P_V + hd * HEAD_D:P_V + (hd + 1) * HEAD_D].astype(BF16))

    def mlstm_chunk(c):
        def run():
            rowi = lax.broadcasted_iota(jnp.int32, (L, L), 0)
            coli = lax.broadcasted_iota(jnp.int32, (L, L), 1)
            causal = coli <= rowi
            r0 = c * L
            g8 = g_ref[0:2 * N_HEADS, r0:r0 + L] + bg_ref[:, 0:1]
            gi = g8
            logf = _log_sigmoid(pltpu.roll(g8, N_HEADS, axis=0))
            b = _scan_lanes(logf, jnp.add, 0.0)
            r = gi - b
            m_prev = m_ref[:, 0:1]
            if c == 0:
                m_prev = jnp.where(first, 0.0, m_prev)
            mm = jnp.maximum(_scan_lanes(r, jnp.maximum, -jnp.inf), m_prev)
            a = jnp.exp(m_prev - mm)
            e = jnp.exp(-(mm + b))
            m_ref[...] = jnp.broadcast_to((mm + b)[:, L - 1:L], m_ref.shape)
            cols = jnp.concatenate([mm, a, e, jnp.zeros((LANES - 6 * N_HEADS, L), F32)], axis=0).T
            mm_c, a_c, e_c = (cols[:, i * 2 * N_HEADS:(i + 1) * 2 * N_HEADS] for i in range(3))
            heads = [slice(hd * HEAD_D, (hd + 1) * HEAD_D) for hd in range(N_HEADS)]
            qs = [q_ref[r0:r0 + L, hs] for hs in heads]
            vxs = [vext_ref[r0:r0 + L, hd * 2 * HEAD_D:(hd + 1) * 2 * HEAD_D] for hd in range(N_HEADS)]
            s_bf, k_tw = [], []
            for hd, hs in enumerate(heads):
                arg = r[hd:hd + 1, :] - mm_c[:, hd:hd + 1]
                p = jnp.exp(jnp.where(causal, arg, -1e30))
                k_t = k_ref[r0:r0 + L, hs].T
                s = jnp.dot(qs[hd].astype(BF16), k_t.astype(BF16), preferred_element_type=F32) * p
                s_bf.append(s.astype(BF16))
                k_tw.append((k_t * p[L - 1:L, :]).astype(BF16))
            nds = []
            for hd in range(N_HEADS):
                cn = cn_ref[hd]
                if c == 0:
                    cn = jnp.where(first, 0.0, cn)
                lhs = jnp.concatenate([(a_c[:, hd:hd + 1] * qs[hd]).astype(BF16), s_bf[hd]], axis=1)
                rhs = jnp.concatenate([cn.astype(BF16), vxs[hd]], axis=0)
                nds.append(jnp.dot(lhs, rhs, preferred_element_type=F32))
                cn_ref[hd] = (a[hd:hd + 1, L - 1:L] * cn
                              + jnp.dot(k_tw[hd], vxs[hd], preferred_element_type=F32))
            for hd, hs in enumerate(heads):
                nd = nds[hd]
                hh = nd[:, 0:HEAD_D] / jnp.maximum(jnp.abs(nd[:, HEAD_D:HEAD_D + 1]), e_c[:, hd:hd + 1])
                hg = _sigmoid(proj_ref[r0:r0 + L, P_O + hd * HEAD_D:P_O + (hd + 1) * HEAD_D]) * hh
                hg = hg * lax.rsqrt(jnp.mean(hg * hg, axis=-1, keepdims=True) + EPS)
                hg = hg * mng_ref[:, hs]
                ya = hg * _silu(proj_ref[r0:r0 + L, P_ZA + hd * HEAD_D:P_ZA + (hd + 1) * HEAD_D])
                y_ref[r0:r0 + L, hs] = ya.astype(BF16)
        return run

    def pool():
        tg = (t_seq * T + lax.broadcasted_iota(jnp.int32, (T, LANES), 0) + 1).astype(F32)
        zero_w = jnp.zeros((LANES, LANES), BF16)
        for g0 in range(0, len(POOL_WINDOWS), 2):
            pooled = []
            for g in (g0, g0 + 1):
                win = POOL_WINDOWS[g]
                gs = slice(g * LANES, (g + 1) * LANES)
                ssum = u_ext[:, gs]
                sh = 1
                while sh < win:
                    ssum = ssum + pltpu.roll(ssum, sh, axis=0)
                    sh *= 2
                u = u_ext[POOL_HIST:POOL_HIST + T, gs]
                pooled.append((ssum[POOL_HIST:POOL_HIST + T] / jnp.minimum(tg, float(win)) - u).astype(BF16))
            w_a, w_b = pw_ref[g0].astype(BF16), pw_ref[g0 + 1].astype(BF16)
            w_pair = jnp.concatenate([jnp.concatenate([w_a, zero_w], axis=1),
                                      jnp.concatenate([zero_w, w_b], axis=1)], axis=0)
            yb = jnp.dot(jnp.concatenate(pooled, axis=1), w_pair, preferred_element_type=F32)
            yb = (yb * ps_ref[:, g0 * LANES:(g0 + 2) * LANES]
                  * _silu(proj_ref[:, P_ZB + g0 * LANES:P_ZB + (g0 + 2) * LANES]))
            y_ref[:, GROUP_W + g0 * LANES:GROUP_W + (g0 + 2) * LANES] = yb.astype(BF16)

    def attention():
        for hd in range(N_HEADS):
            hs = slice(hd * HEAD_D, (hd + 1) * HEAD_D)
            qc = proj_ref[:, P_QC + hd * HEAD_D:P_QC + (hd + 1) * HEAD_D].astype(BF16)
            s = jnp.dot(qc, kmt_ref[hs, :], preferred_element_type=F32) * (HEAD_D ** -0.5)
            pexp = jnp.exp(s - jnp.max(s, axis=-1, keepdims=True))
            pv = jnp.dot(pexp.astype(BF16), vmx_ref[hd], preferred_element_type=F32)
            yc = pv[:, 0:HEAD_D] / pv[:, HEAD_D:HEAD_D + 1]
            yc = yc * _silu(proj_ref[:, P_ZC + hd * HEAD_D:P_ZC + (hd + 1) * HEAD_D])
            y_ref[:, 2 * GROUP_W + hd * HEAD_D:2 * GROUP_W + (hd + 1) * HEAD_D] = yc.astype(BF16)

    def out_proj():
        rows = slice(x_row, x_row + T)
        for i in range(D_MODEL // GROUP_W):
            cs = slice(i * GROUP_W, (i + 1) * GROUP_W)
            o_ref[rows, cs] = x_ref[rows, cs] + jnp.dot(y_ref[...], wo_ref[i], preferred_element_type=F32)
        if final:
            o_ref[rows, :] = _rmsnorm(o_ref[rows, :], fng_ref[...])

    return dict(conv_q=conv_q, conv_k=conv_k, chunks=[mlstm_chunk(c) for c in range(T // L)],
                pool=pool, attention=attention, out_proj=out_proj)


def _layer_kernel(xp_ref, xn_ref, kmt_ref, vmx_ref, ng_ref, w_ref, wg_ref, bg_ref, cw_ref, cb_ref, mng_ref,
                  pw_ref, ps_ref, wo_ref, fng_ref, o_ref,
                  h_ref, qk0, u0, proj0, g0, qk1, u1, proj1, g1, q_ref, k_ref, vext_ref, cn_ref, m_ref, y_ref,
                  *, final, tiles_per_seq):
    T = SEQ_TILE
    k = pl.program_id(0)
    set0, set1 = (qk0, u0, proj0, g0), (qk1, u1, proj1, g1)
    a_args = (ng_ref, w_ref, wg_ref, h_ref)
    b_args = (kmt_ref, vmx_ref, bg_ref, cw_ref, cb_ref, mng_ref, pw_ref, ps_ref, wo_ref, fng_ref,
              q_ref, k_ref, vext_ref, cn_ref, m_ref, y_ref, final)

    @pl.when(k == 0)
    def _prologue():
        col = lax.broadcasted_iota(jnp.int32, vext_ref.shape, 1)
        vext_ref[...] = jnp.where(col % (2 * HEAD_D) == HEAD_D, 1.0, 0.0).astype(BF16)
        cn_ref[...] = jnp.zeros(cn_ref.shape, F32)
        m_ref[...] = jnp.zeros(m_ref.shape, F32)
        qk1[T:T + CONV_HIST, :] = jnp.zeros((CONV_HIST, 2 * GROUP_W), F32)
        u1[T:T + POOL_HIST, :] = jnp.zeros((POOL_HIST, GROUP_W), F32)
        for piece in _stage_a(xp_ref, 0, True, set1, set0, *a_args):
            piece()

    t_even = (2 * k) % tiles_per_seq
    first_even = t_even == 0
    first_next = (2 * k + 2) % tiles_per_seq == 0

    def half_step(x_row, first, t_seq, cur, nxt, xa_ref, xa_row, first_a):
        for piece in _stage_a(xa_ref, xa_row, first_a, cur, nxt, *a_args):
            piece()
        b = _stage_b(xp_ref, x_row, first, t_seq, cur, o_ref, *b_args)
        for phase in (b["conv_q"], b["conv_k"], *b["chunks"], b["pool"], b["attention"], b["out_proj"]):
            phase()

    half_step(0, first_even, t_even, set0, set1, xp_ref, T, False)
    half_step(T, False, t_even + 1, set1, set0, xn_ref, 0, first_next)


def _mem_kv_kernel(mem_ref, g_ref, w_ref, kmt_ref, vmx_ref, wb_ref):
    @pl.when(pl.program_id(0) == 0)
    def _cast_weights():
        wb_ref[...] = w_ref[...].astype(BF16)

    mem = mem_ref[...]
    ms = jnp.mean(mem * mem, axis=-1, keepdims=True)
    nrm = mem * lax.rsqrt(ms + EPS)
    col = lax.broadcasted_iota(jnp.int32, (N_MEM, HEAD_D), 1)
    ones_col = jnp.where(col == 0, 1.0, 0.0).astype(BF16)
    for l in range(wb_ref.shape[0]):
        mn = (nrm * g_ref[l]).astype(BF16)
        kv = jnp.dot(mn, wb_ref[l], preferred_element_type=F32)
        for hd in range(N_HEADS):
            hs = slice(hd * HEAD_D, (hd + 1) * HEAD_D)
            kmt_ref[l, hs, :] = kv[:, hs].T.astype(BF16)
            vmx_ref[l, hd, :, 0:HEAD_D] = kv[:, GROUP_W + hd * HEAD_D:GROUP_W + (hd + 1) * HEAD_D].astype(BF16)
            vmx_ref[l, hd, :, HEAD_D:2 * HEAD_D] = ones_col


def _prep_kernel(wint_ref, wout_ref, wall_ref, wg_ref, wo_ref):
    wall_ref[:, 0:GATE0] = wint_ref[0:GATE0, :].T.astype(BF16)
    wall_ref[:, GATE0:N_PROJ] = wint_ref[GATE0 + 2 * N_HEADS:N_IN, :].T.astype(BF16)
    wg_ref[0:2 * N_HEADS, :] = wint_ref[GATE0:GATE0 + 2 * N_HEADS, :].astype(BF16)
    wg_ref[2 * N_HEADS:GATE_ROWS, :] = jnp.zeros((GATE_ROWS - 2 * N_HEADS, wg_ref.shape[1]), BF16)
    for i in range(D_MODEL // GROUP_W):
        wo_ref[i] = wout_ref[:, i * GROUP_W:(i + 1) * GROUP_W].astype(BF16)


def _resident(shape, layer):
    nd = len(shape)
    return pl.BlockSpec((None,) + shape, lambda k: (layer,) + (0,) * nd, pipeline_mode=pl.Buffered(1))


def _prep_weights(w_in, w_out):
    depth = w_in.shape[0]
    steps = 4
    rb_in, rb_out = D_MODEL // steps, 3 * GROUP_W // steps
    return pl.pallas_call(
        _prep_kernel,
        grid=(depth, steps),
        in_specs=[
            pl.BlockSpec((None, N_IN, rb_in), lambda l, i: (l, 0, i)),
            pl.BlockSpec((None, rb_out, D_MODEL), lambda l, i: (l, i, 0)),
        ],
        out_specs=[
            pl.BlockSpec((None, rb_in, N_PROJ), lambda l, i: (l, i, 0)),
            pl.BlockSpec((None, GATE_ROWS, rb_in), lambda l, i: (l, 0, i)),
            pl.BlockSpec((None, D_MODEL // GROUP_W, rb_out, GROUP_W), lambda l, i: (l, 0, i, 0)),
        ],
        out_shape=[
            jax.ShapeDtypeStruct((depth, D_MODEL, N_PROJ), BF16),
            jax.ShapeDtypeStruct((depth, GATE_ROWS, D_MODEL), BF16),
            jax.ShapeDtypeStruct((depth, D_MODEL // GROUP_W, 3 * GROUP_W, GROUP_W), BF16),
        ],
        compiler_params=pltpu.CompilerParams(
            dimension_semantics=("arbitrary", "arbitrary"), vmem_limit_bytes=VMEM_LIMIT_BYTES),
        name="prep_weights",
    )(jnp.swapaxes(w_in, 1, 2), w_out)


def _mem_kv(mem, mem_norm_g, w_kv):
    depth, batch = w_kv.shape[0], mem.shape[0]
    return pl.pallas_call(
        _mem_kv_kernel,
        grid=(batch,),
        in_specs=[
            pl.BlockSpec((None, N_MEM, D_MODEL), lambda b: (b, 0, 0)),
            pl.BlockSpec((depth, 1, D_MODEL), lambda b: (0, 0, 0)),
            pl.BlockSpec((depth, D_MODEL, 2 * GROUP_W), lambda b: (0, 0, 0), pipeline_mode=pl.Buffered(1)),
        ],
        out_specs=[
            pl.BlockSpec((depth, None, GROUP_W, N_MEM), lambda b: (0, b, 0, 0)),
            pl.BlockSpec((depth, None, N_HEADS, N_MEM, 2 * HEAD_D), lambda b: (0, b, 0, 0, 0)),
        ],
        out_shape=[
            jax.ShapeDtypeStruct((depth, batch, GROUP_W, N_MEM), BF16),
            jax.ShapeDtypeStruct((depth, batch, N_HEADS, N_MEM, 2 * HEAD_D), BF16),
        ],
        scratch_shapes=[pltpu.VMEM((depth, D_MODEL, 2 * GROUP_W), BF16)],
        compiler_params=pltpu.CompilerParams(
            dimension_semantics=("arbitrary",), vmem_limit_bytes=VMEM_LIMIT_BYTES),
        name="mem_kv",
    )(mem, mem_norm_g.reshape(depth, 1, D_MODEL), w_kv)


def _layer(layer, final, x, kmt, vmx, ng, w_all, w_g, b_g, conv_w, conv_b, mng, pool_w, pool_scale, w_out, fng):
    batch, seq, _ = x.shape
    T = SEQ_TILE
    assert seq % (2 * T) == 0 and T % CHUNK == 0
    tiles_per_seq = seq // T
    pairs_per_seq = tiles_per_seq // 2
    last_tile = batch * tiles_per_seq - 1

    def next_tile(k):
        nt = jnp.minimum(2 * k + 2, last_tile)
        return nt // tiles_per_seq, nt % tiles_per_seq, 0

    kern = functools.partial(_layer_kernel, final=final, tiles_per_seq=tiles_per_seq)
    set_scratch = [
        pltpu.VMEM((CONV_HIST + T, 2 * GROUP_W), F32),
        pltpu.VMEM((POOL_HIST + T, GROUP_W), F32),
        pltpu.VMEM((T, P_COLS), F32),
        pltpu.VMEM((GATE_ROWS, T), F32),
    ]
    return pl.pallas_call(
        kern,
        grid=(batch * pairs_per_seq,),
        in_specs=[
            pl.BlockSpec((None, 2 * T, D_MODEL), lambda k: (k // pairs_per_seq, k % pairs_per_seq, 0)),
            pl.BlockSpec((None, T, D_MODEL), next_tile),
            pl.BlockSpec((None, None, GROUP_W, N_MEM), lambda k: (layer, k // pairs_per_seq, 0, 0)),
            pl.BlockSpec((None, None, N_HEADS, N_MEM, 2 * HEAD_D),
                         lambda k: (layer, k // pairs_per_seq, 0, 0, 0)),
            _resident((1, D_MODEL), layer),
            _resident((D_MODEL, N_PROJ), layer),
            _resident((GATE_ROWS, D_MODEL), layer),
            _resident((2 * N_HEADS, LANES), layer),
            _resident((CONV_K, 2 * GROUP_W), layer),
            _resident((1, 2 * GROUP_W), layer),
            _resident((1, GROUP_W), layer),
            _resident((len(POOL_WINDOWS), LANES, LANES), layer),
            _resident((1, GROUP_W), layer),
            _resident((D_MODEL // GROUP_W, 3 * GROUP_W, GROUP_W), layer),
            pl.BlockSpec((1, D_MODEL), lambda k: (0, 0), pipeline_mode=pl.Buffered(1)),
        ],
        out_specs=pl.BlockSpec((None, 2 * T, D_MODEL), lambda k: (k // pairs_per_seq, k % pairs_per_seq, 0)),
        out_shape=jax.ShapeDtypeStruct(x.shape, F32),
        scratch_shapes=[pltpu.VMEM((T, D_MODEL), BF16)] + set_scratch + set_scratch + [
            pltpu.VMEM((T, GROUP_W), F32),
            pltpu.VMEM((T, GROUP_W), F32),
            pltpu.VMEM((T, N_HEADS * 2 * HEAD_D), BF16),
            pltpu.VMEM((N_HEADS, HEAD_D, 2 * HEAD_D), F32),
            pltpu.VMEM((2 * N_HEADS, LANES), F32),
            pltpu.VMEM((T, 3 * GROUP_W), BF16),
        ],
        compiler_params=pltpu.CompilerParams(
            dimension_semantics=("arbitrary",), vmem_limit_bytes=VMEM_LIMIT_BYTES),
        name="trunk_layer_final" if final else "trunk_layer",
    )(x, x, kmt, vmx, ng, w_all, w_g, b_g, conv_w, conv_b, mng, pool_w, pool_scale, w_out, fng)


def kernel(x, mem, norm_g, w_in, b_gates, conv_w, conv_b, mlstm_norm_g, pool_w, pool_scale,
           mem_norm_g, w_mem_kv, w_out, final_norm_g):
    depth = w_in.shape[0]
    w_all, w_g, w_out_b = _prep_weights(w_in, w_out)
    kmt, vmx = _mem_kv(mem, mem_norm_g, w_mem_kv)
    b_g = jnp.broadcast_to(b_gates[:, :, None], (depth, 2 * N_HEADS, LANES))
    args = (kmt, vmx, norm_g.reshape(depth, 1, D_MODEL), w_all, w_g, b_g,
            conv_w, conv_b.reshape(depth, 1, 2 * GROUP_W), mlstm_norm_g.reshape(depth, 1, GROUP_W),
            pool_w, pool_scale.reshape(depth, 1, GROUP_W), w_out_b, final_norm_g.reshape(1, D_MODEL))
    for l in range(depth):
        x = _layer(l, l == depth - 1, x, *args)
    return x
```

```python
import functools

import jax
import jax.numpy as jnp
from jax import lax
from jax.experimental import pallas as pl
from jax.experimental.pallas import tpu as pltpu

D_MODEL = 1024
GROUP_W = 512
N_HEADS = 4
HEAD_D = 128
CONV_K = 4
POOL_WINDOWS = (2, 4, 8, 16)
N_MEM = 256
EPS = 1e-6

SEQ_TILE = 256
CHUNK = 256
CONV_HIST = 8
POOL_HIST = 16
LANES = 128
VMEM_LIMIT_BYTES = 52 * 1024 * 1024

N_IN = 2 * GROUP_W + 3 * GROUP_W + 2 * N_HEADS + 4 * GROUP_W
GATE0 = 5 * GROUP_W
Q0, K0, V0, O0, ZA0, U0, ZB0, QC0, ZC0 = 0, 512, 1024, 1536, 2048, 2560, 3072, 3584, 4096
N_PROJ = 4608
GATE_ROWS = 16
P_V, P_O, P_ZA, P_ZB, P_QC, P_ZC = 0, 512, 1024, 1536, 2048, 2560
P_COLS = 3072

BF16 = jnp.bfloat16
F32 = jnp.float32


def _sigmoid(x):
    return 0.5 * jnp.tanh(0.5 * x) + 0.5


def _silu(x):
    hx = 0.5 * x
    return hx + hx * jnp.tanh(hx)


def _log_sigmoid(x):
    return jnp.minimum(x, 0.0) - jnp.log1p(jnp.exp(-jnp.abs(x)))


def _scan_lanes(x, op, identity):
    n = x.shape[1]
    lanes = lax.broadcasted_iota(jnp.int32, x.shape, 1)
    k = 1
    while k < n:
        x = op(x, jnp.where(lanes >= k, pltpu.roll(x, k, axis=1), identity))
        k *= 2
    return x


def _rmsnorm(x, g):
    ms = jnp.mean(x * x, axis=-1, keepdims=True)
    return (x * lax.rsqrt(ms + EPS)) * g


def _stage_a(x_ref, x_row, first, prev, cur, ng_ref, w_ref, wg_ref, h_ref):
    T = SEQ_TILE
    qk_prev, u_prev, _, _ = prev
    qk_cur, u_cur, proj_cur, g_cur = cur

    def head():
        qk_cur[0:CONV_HIST, :] = jnp.where(first, 0.0, qk_prev[T:T + CONV_HIST, :])
        u_cur[0:POOL_HIST, :] = jnp.where(first, 0.0, u_prev[T:T + POOL_HIST, :])
        h_ref[...] = _rmsnorm(x_ref[x_row:x_row + T, :], ng_ref[...]).astype(BF16)

    def proj(c0):
        return jnp.dot(h_ref[...], w_ref[:, c0:c0 + GROUP_W], preferred_element_type=F32)

    def dot_q():
        qk_cur[CONV_HIST:CONV_HIST + T, 0:GROUP_W] = proj(Q0)

    def dot_k():
        qk_cur[CONV_HIST:CONV_HIST + T, GROUP_W:2 * GROUP_W] = proj(K0)

    def dot_u():
        u_cur[POOL_HIST:POOL_HIST + T, :] = proj(U0)

    def dot_to(src, dst):
        def run():
            proj_cur[:, dst:dst + GROUP_W] = proj(src)
        return run

    def dot_gates():
        g_cur[...] = lax.dot_general(wg_ref[...], h_ref[...], (((1,), (1,)), ((), ())),
                                     preferred_element_type=F32)

    return [head, dot_q, dot_k, dot_u, dot_to(V0, P_V), dot_to(O0, P_O), dot_to(ZA0, P_ZA),
            dot_to(ZB0, P_ZB), dot_to(QC0, P_QC), dot_to(ZC0, P_ZC), dot_gates]


def _stage_b(x_ref, x_row, first, t_seq, cur, o_ref, kmt_ref, vmx_ref, bg_ref, cw_ref, cb_ref, mng_ref,
             pw_ref, ps_ref, wo_ref, fng_ref, q_ref, k_ref, vext_ref, cn_ref, m_ref, y_ref, final):
    T, L = SEQ_TILE, CHUNK
    qk_ext, u_ext, proj_ref, g_ref = cur

    def conv(c0):
        xs = qk_ext[:, c0:c0 + GROUP_W]
        acc = cb_ref[:, c0:c0 + GROUP_W] + cw_ref[CONV_K - 1:CONV_K, c0:c0 + GROUP_W] * xs[CONV_HIST:]
        for j in range(CONV_K - 1):
            back = pltpu.roll(xs, CONV_K - 1 - j, axis=0)[CONV_HIST:]
            acc = acc + cw_ref[j:j + 1, c0:c0 + GROUP_W] * back
        return _silu(acc)

    def conv_q():
        q_ref[...] = conv(0)

    def conv_k():
        k_ref[...] = conv(GROUP_W) * (HEAD_D ** -0.5)
        for hd in range(N_HEADS):
            vext_ref[:, hd * 2 * HEAD_D:hd * 2 * HEAD_D + HEAD_D] = (
                proj_ref[:, P_V + hd * HEAD_D:P_V + (hd + 1) * HEAD_D].astype(BF16))

    def gates():
        proj_ref[:, P_O:P_O + GROUP_W] = _sigmoid(proj_ref[:, P_O:P_O + GROUP_W])
        for c0 in (P_ZA, P_ZB, P_ZC):
            proj_ref[:, c0:c0 + GROUP_W] = _silu(proj_ref[:, c0:c0 + GROUP_W])

    def mlstm_chunk(c):
        def run():
            rowi = lax.broadcasted_iota(jnp.int32, (L, L), 0)
            coli = lax.broadcasted_iota(jnp.int32, (L, L), 1)
            causal = coli <= rowi
            r0 = c * L
            g8 = g_ref[0:2 * N_HEADS, r0:r0 + L] + bg_ref[:, 0:1]
            gi = g8
            logf = _log_sigmoid(pltpu.roll(g8, N_HEADS, axis=0))
            b = _scan_lanes(logf, jnp.add, 0.0)
            r = gi - b
            m_prev = m_ref[:, 0:1]
            if c == 0:
                m_prev = jnp.where(first, 0.0, m_prev)
            mm = jnp.maximum(_scan_lanes(r, jnp.maximum, -jnp.inf), m_prev)
            a = jnp.exp(m_prev - mm)
            e = jnp.exp(-(mm + b))
            m_ref[...] = jnp.broadcast_to((mm + b)[:, L - 1:L], m_ref.shape)
            cols = jnp.concatenate([mm, a, e, jnp.zeros((LANES - 6 * N_HEADS, L), F32)], axis=0).T
            mm_c, a_c, e_c = (cols[:, i * 2 * N_HEADS:(i + 1) * 2 * N_HEADS] for i in range(3))
            heads = [slice(hd * HEAD_D, (hd + 1) * HEAD_D) for hd in range(N_HEADS)]
            qs = [q_ref[r0:r0 + L, hs] for hs in heads]
            vxs = [vext_ref[r0:r0 + L, hd * 2 * HEAD_D:(hd + 1) * 2 * HEAD_D] for hd in range(N_HEADS)]
            s_bf, k_tw = [], []
            for hd, hs in enumerate(heads):
                arg = r[hd:hd + 1, :] - mm_c[:, hd:hd + 1]
                p = jnp.exp(jnp.where(causal, arg, -1e30))
                k_t = k_ref[r0:r0 + L, hs].T
                s = jnp.dot(qs[hd].astype(BF16), k_t.astype(BF16), preferred_element_type=F32) * p
                s_bf.append(s.astype(BF16))
                k_tw.append((k_t * p[L - 1:L, :]).astype(BF16))
            nds = []
            for hd in range(N_HEADS):
                cn = cn_ref[hd]
                if c == 0:
                    cn = jnp.where(first, 0.0, cn)
                lhs = jnp.concatenate([(a_c[:, hd:hd + 1] * qs[hd]).astype(BF16), s_bf[hd]], axis=1)
                rhs = jnp.concatenate([cn.astype(BF16), vxs[hd]], axis=0)
                nds.append(jnp.dot(lhs, rhs, preferred_element_type=F32))
                cn_ref[hd] = (a[hd:hd + 1, L - 1:L] * cn
                              + jnp.dot(k_tw[hd], vxs[hd], preferred_element_type=F32))
            for hd, hs in enumerate(heads):
                nd = nds[hd]
                hh = nd[:, 0:HEAD_D] / jnp.maximum(jnp.abs(nd[:, HEAD_D:HEAD_D + 1]), e_c[:, hd:hd + 1])
                hg = proj_ref[r0:r0 + L, P_O + hd * HEAD_D:P_O + (hd + 1) * HEAD_D] * hh
                hg = hg * lax.rsqrt(jnp.mean(hg * hg, axis=-1, keepdims=True) + EPS)
                hg = hg * mng_ref[:, hs]
                ya = hg * proj_ref[r0:r0 + L, P_ZA + hd * HEAD_D:P_ZA + (hd + 1) * HEAD_D]
                y_ref[r0:r0 + L, hs] = ya.astype(BF16)
        return run

    def pool():
        tg = (t_seq * T + lax.broadcasted_iota(jnp.int32, (T, LANES), 0) + 1).astype(F32)
        zero_w = jnp.zeros((LANES, LANES), BF16)
        for g0 in range(0, len(POOL_WINDOWS), 2):
            pooled = []
            for g in (g0, g0 + 1):
                win = POOL_WINDOWS[g]
                gs = slice(g * LANES, (g + 1) * LANES)
                ssum = u_ext[:, gs]
                sh = 1
                while sh < win:
                    ssum = ssum + pltpu.roll(ssum, sh, axis=0)
                    sh *= 2
                u = u_ext[POOL_HIST:POOL_HIST + T, gs]
                pooled.append((ssum[POOL_HIST:POOL_HIST + T] / jnp.minimum(tg, float(win)) - u).astype(BF16))
            w_a, w_b = pw_ref[g0].astype(BF16), pw_ref[g0 + 1].astype(BF16)
            w_pair = jnp.concatenate([jnp.concatenate([w_a, zero_w], axis=1),
                                      jnp.concatenate([zero_w, w_b], axis=1)], axis=0)
            yb = jnp.dot(jnp.concatenate(pooled, axis=1), w_pair, preferred_element_type=F32)
            yb = (yb * ps_ref[:, g0 * LANES:(g0 + 2) * LANES]
                  * proj_ref[:, P_ZB + g0 * LANES:P_ZB + (g0 + 2) * LANES])
            y_ref[:, GROUP_W + g0 * LANES:GROUP_W + (g0 + 2) * LANES] = yb.astype(BF16)

    def attention():
        for hd in range(N_HEADS):
            hs = slice(hd * HEAD_D, (hd + 1) * HEAD_D)
            qc = proj_ref[:, P_QC + hd * HEAD_D:P_QC + (hd + 1) * HEAD_D].astype(BF16)
            s = jnp.dot(qc, kmt_ref[hs, :], preferred_element_type=F32) * (HEAD_D ** -0.5)
            pexp = jnp.exp(s - jnp.max(s, axis=-1, keepdims=True))
            pv = jnp.dot(pexp.astype(BF16), vmx_ref[hd], preferred_element_type=F32)
            yc = pv[:, 0:HEAD_D] / pv[:, HEAD_D:HEAD_D + 1]
            yc = yc * proj_ref[:, P_ZC + hd * HEAD_D:P_ZC + (hd + 1) * HEAD_D]
            y_ref[:, 2 * GROUP_W + hd * HEAD_D:2 * GROUP_W + (hd + 1) * HEAD_D] = yc.astype(BF16)

    def out_proj():
        rows = slice(x_row, x_row + T)
        for i in range(D_MODEL // GROUP_W):
            cs = slice(i * GROUP_W, (i + 1) * GROUP_W)
            o_ref[rows, cs] = x_ref[rows, cs] + jnp.dot(y_ref[...], wo_ref[i], preferred_element_type=F32)
        if final:
            o_ref[rows, :] = _rmsnorm(o_ref[rows, :], fng_ref[...])

    return dict(conv_q=conv_q, conv_k=conv_k, gates=gates, chunks=[mlstm_chunk(c) for c in range(T // L)],
                pool=pool, attention=attention, out_proj=out_proj)


def _layer_kernel(xp_ref, xn_ref, kmt_ref, vmx_ref, ng_ref, w_ref, wg_ref, bg_ref, cw_ref, cb_ref, mng_ref,
                  pw_ref, ps_ref, wo_ref, fng_ref, o_ref,
                  h_ref, qk0, u0, proj0, g0, qk1, u1, proj1, g1, q_ref, k_ref, vext_ref, cn_ref, m_ref, y_ref,
                  *, final, tiles_per_seq):
    T = SEQ_TILE
    k = pl.program_id(0)
    set0, set1 = (qk0, u0, proj0, g0), (qk1, u1, proj1, g1)
    a_args = (ng_ref, w_ref, wg_ref, h_ref)
    b_args = (kmt_ref, vmx_ref, bg_ref, cw_ref, cb_ref, mng_ref, pw_ref, ps_ref, wo_ref, fng_ref,
              q_ref, k_ref, vext_ref, cn_ref, m_ref, y_ref, final)

    @pl.when(k == 0)
    def _prologue():
        col = lax.broadcasted_iota(jnp.int32, vext_ref.shape, 1)
        vext_ref[...] = jnp.where(col % (2 * HEAD_D) == HEAD_D, 1.0, 0.0).astype(BF16)
        cn_ref[...] = jnp.zeros(cn_ref.shape, F32)
        m_ref[...] = jnp.zeros(m_ref.shape, F32)
        qk1[T:T + CONV_HIST, :] = jnp.zeros((CONV_HIST, 2 * GROUP_W), F32)
        u1[T:T + POOL_HIST, :] = jnp.zeros((POOL_HIST, GROUP_W), F32)
        for piece in _stage_a(xp_ref, 0, True, set1, set0, *a_args):
            piece()

    t_even = (2 * k) % tiles_per_seq
    first_even = t_even == 0
    first_next = (2 * k + 2) % tiles_per_seq == 0

    def half_step(x_row, first, t_seq, cur, nxt, xa_ref, xa_row, first_a):
        for piece in _stage_a(xa_ref, xa_row, first_a, cur, nxt, *a_args):
            piece()
        b = _stage_b(xp_ref, x_row, first, t_seq, cur, o_ref, *b_args)
        for phase in (b["conv_q"], b["conv_k"], b["gates"], *b["chunks"], b["pool"], b["attention"],
                      b["out_proj"]):
            phase()

    half_step(0, first_even, t_even, set0, set1, xp_ref, T, False)
    half_step(T, False, t_even + 1, set1, set0, xn_ref, 0, first_next)


def _mem_kv_kernel(mem_ref, g_ref, w_ref, kmt_ref, vmx_ref, wb_ref):
    @pl.when(pl.program_id(0) == 0)
    def _cast_weights():
        wb_ref[...] = w_ref[...].astype(BF16)

    mem = mem_ref[...]
    ms = jnp.mean(mem * mem, axis=-1, keepdims=True)
    nrm = mem * lax.rsqrt(ms + EPS)
    col = lax.broadcasted_iota(jnp.int32, (N_MEM, HEAD_D), 1)
    ones_col = jnp.where(col == 0, 1.0, 0.0).astype(BF16)
    for l in range(wb_ref.shape[0]):
        mn = (nrm * g_ref[l]).astype(BF16)
        kv = jnp.dot(mn, wb_ref[l], preferred_element_type=F32)
        for hd in range(N_HEADS):
            hs = slice(hd * HEAD_D, (hd + 1) * HEAD_D)
            kmt_ref[l, hs, :] = kv[:, hs].T.astype(BF16)
            vmx_ref[l, hd, :, 0:HEAD_D] = kv[:, GROUP_W + hd * HEAD_D:GROUP_W + (hd + 1) * HEAD_D].astype(BF16)
            vmx_ref[l, hd, :, HEAD_D:2 * HEAD_D] = ones_col


def _prep_kernel(wint_ref, wout_ref, wall_ref, wg_ref, wo_ref):
    wall_ref[:, 0:GATE0] = wint_ref[0:GATE0, :].T.astype(BF16)
    wall_ref[:, GATE0:N_PROJ] = wint_ref[GATE0 + 2 * N_HEADS:N_IN, :].T.astype(BF16)
    wg_ref[0:2 * N_HEADS, :] = wint_ref[GATE0:GATE0 + 2 * N_HEADS, :].astype(BF16)
    wg_ref[2 * N_HEADS:GATE_ROWS, :] = jnp.zeros((GATE_ROWS - 2 * N_HEADS, wg_ref.shape[1]), BF16)
    for i in range(D_MODEL // GROUP_W):
        wo_ref[i] = wout_ref[:, i * GROUP_W:(i + 1) * GROUP_W].astype(BF16)


def _resident(shape, layer):
    nd = len(shape)
    return pl.BlockSpec((None,) + shape, lambda k: (layer,) + (0,) * nd, pipeline_mode=pl.Buffered(1))


def _prep_weights(w_in, w_out):
    depth = w_in.shape[0]
    steps = 4
    rb_in, rb_out = D_MODEL // steps, 3 * GROUP_W // steps
    return pl.pallas_call(
        _prep_kernel,
        grid=(depth, steps),
        in_specs=[
            pl.BlockSpec((None, N_IN, rb_in), lambda l, i: (l, 0, i)),
            pl.BlockSpec((None, rb_out, D_MODEL), lambda l, i: (l, i, 0)),
        ],
        out_specs=[
            pl.BlockSpec((None, rb_in, N_PROJ), lambda l, i: (l, i, 0)),
            pl.BlockSpec((None, GATE_ROWS, rb_in), lambda l, i: (l, 0, i)),
            pl.BlockSpec((None, D_MODEL // GROUP_W, rb_out, GROUP_W), lambda l, i: (l, 0, i, 0)),
        ],
        out_shape=[
            jax.ShapeDtypeStruct((depth, D_MODEL, N_PROJ), BF16),
            jax.ShapeDtypeStruct((depth, GATE_ROWS, D_MODEL), BF16),
            jax.ShapeDtypeStruct((depth, D_MODEL // GROUP_W, 3 * GROUP_W, GROUP_W), BF16),
        ],
        compiler_params=pltpu.CompilerParams(
            dimension_semantics=("arbitrary", "arbitrary"), vmem_limit_bytes=VMEM_LIMIT_BYTES),
        name="prep_weights",
    )(jnp.swapaxes(w_in, 1, 2), w_out)


def _mem_kv(mem, mem_norm_g, w_kv):
    depth, batch = w_kv.shape[0], mem.shape[0]
    return pl.pallas_call(
        _mem_kv_kernel,
        grid=(batch,),
        in_specs=[
            pl.BlockSpec((None, N_MEM, D_MODEL), lambda b: (b, 0, 0)),
            pl.BlockSpec((depth, 1, D_MODEL), lambda b: (0, 0, 0)),
            pl.BlockSpec((depth, D_MODEL, 2 * GROUP_W), lambda b: (0, 0, 0), pipeline_mode=pl.Buffered(1)),
        ],
        out_specs=[
            pl.BlockSpec((depth, None, GROUP_W, N_MEM), lambda b: (0, b, 0, 0)),
            pl.BlockSpec((depth, None, N_HEADS, N_MEM, 2 * HEAD_D), lambda b: (0, b, 0, 0, 0)),
        ],
        out_shape=[
            jax.ShapeDtypeStruct((depth, batch, GROUP_W, N_MEM), BF16),
            jax.ShapeDtypeStruct((depth, batch, N_HEADS, N_MEM, 2 * HEAD_D), BF16),
        ],
        scratch_shapes=[pltpu.VMEM((depth, D_MODEL, 2 * GROUP_W), BF16)],
        compiler_params=pltpu.CompilerParams(
            dimension_semantics=("arbitrary",), vmem_limit_bytes=VMEM_LIMIT_BYTES),
        name="mem_kv",
    )(mem, mem_norm_g.reshape(depth, 1, D_MODEL), w_kv)


def _layer(layer, final, x, kmt, vmx, ng, w_all, w_g, b_g, conv_w, conv_b, mng, pool_w, pool_scale, w_out, fng):
    batch, seq, _ = x.shape
    T = SEQ_TILE
    assert seq % (2 * T) == 0 and T % CHUNK == 0
    tiles_per_seq = seq // T
    pairs_per_seq = tiles_per_seq // 2
    last_tile = batch * tiles_per_seq - 1

    def next_tile(k):
        nt = jnp.minimum(2 * k + 2, last_tile)
        return nt // tiles_per_seq, nt % tiles_per_seq, 0

    kern = functools.partial(_layer_kernel, final=final, tiles_per_seq=tiles_per_seq)
    set_scratch = [
        pltpu.VMEM((CONV_HIST + T, 2 * GROUP_W), F32),
        pltpu.VMEM((POOL_HIST + T, GROUP_W), F32),
        pltpu.VMEM((T, P_COLS), F32),
        pltpu.VMEM((GATE_ROWS, T), F32),
    ]
    return pl.pallas_call(
        kern,
        grid=(batch * pairs_per_seq,),
        in_specs=[
            pl.BlockSpec((None, 2 * T, D_MODEL), lambda k: (k // pairs_per_seq, k % pairs_per_seq, 0)),
            pl.BlockSpec((None, T, D_MODEL), next_tile),
            pl.BlockSpec((None, None, GROUP_W, N_MEM), lambda k: (layer, k // pairs_per_seq, 0, 0)),
            pl.BlockSpec((None, None, N_HEADS, N_MEM, 2 * HEAD_D),
                         lambda k: (layer, k // pairs_per_seq, 0, 0, 0)),
            _resident((1, D_MODEL), layer),
            _resident((D_MODEL, N_PROJ), layer),
            _resident((GATE_ROWS, D_MODEL), layer),
            _resident((2 * N_HEADS, LANES), layer),
            _resident((CONV_K, 2 * GROUP_W), layer),
            _resident((1, 2 * GROUP_W), layer),
            _resident((1, GROUP_W), layer),
            _resident((len(POOL_WINDOWS), LANES, LANES), layer),
            _resident((1, GROUP_W), layer),
            _resident((D_MODEL // GROUP_W, 3 * GROUP_W, GROUP_W), layer),
            pl.BlockSpec((1, D_MODEL), lambda k: (0, 0), pipeline_mode=pl.Buffered(1)),
        ],
        out_specs=pl.BlockSpec((None, 2 * T, D_MODEL), lambda k: (k // pairs_per_seq, k % pairs_per_seq, 0)),
        out_shape=jax.ShapeDtypeStruct(x.shape, F32),
        scratch_shapes=[pltpu.VMEM((T, D_MODEL), BF16)] + set_scratch + set_scratch + [
            pltpu.VMEM((T, GROUP_W), F32),
            pltpu.VMEM((T, GROUP_W), F32),
            pltpu.VMEM((T, N_HEADS * 2 * HEAD_D), BF16),
            pltpu.VMEM((N_HEADS, HEAD_D, 2 * HEAD_D), F32),
            pltpu.VMEM((2 * N_HEADS, LANES), F32),
            pltpu.VMEM((T, 3 * GROUP_W), BF16),
        ],
        compiler_params=pltpu.CompilerParams(
            dimension_semantics=("arbitrary",), vmem_limit_bytes=VMEM_LIMIT_BYTES),
        name="trunk_layer_final" if final else "trunk_layer",
    )(x, x, kmt, vmx, ng, w_all, w_g, b_g, conv_w, conv_b, mng, pool_w, pool_scale, w_out, fng)


def kernel(x, mem, norm_g, w_in, b_gates, conv_w, conv_b, mlstm_norm_g, pool_w, pool_scale,
           mem_norm_g, w_mem_kv, w_out, final_norm_g):
    depth = w_in.shape[0]
    w_all, w_g, w_out_b = _prep_weights(w_in, w_out)
    kmt, vmx = _mem_kv(mem, mem_norm_g, w_mem_kv)
    b_g = jnp.broadcast_to(b_gates[:, :, None], (depth, 2 * N_HEADS, LANES))
    args = (kmt, vmx, norm_g.reshape(depth, 1, D_MODEL), w_all, w_g, b_g,
            conv_w, conv_b.reshape(depth, 1, 2 * GROUP_W), mlstm_norm_g.reshape(depth, 1, GROUP_W),
            pool_w, pool_scale.reshape(depth, 1, GROUP_W), w_out_b, final_norm_g.reshape(1, D_MODEL))
    for l in range(depth):
        x = _layer(l, l == depth - 1, x, *args)
    return x
```

```python
import functools

import jax
import jax.numpy as jnp
from jax import lax
from jax.experimental import pallas as pl
from jax.experimental.pallas import tpu as pltpu

D_MODEL = 1024
GROUP_W = 512
N_HEADS = 4
HEAD_D = 128
CONV_K = 4
POOL_WINDOWS = (2, 4, 8, 16)
N_MEM = 256
EPS = 1e-6

SEQ_TILE = 256
CHUNK = 256
CONV_HIST = 8
POOL_HIST = 16
LANES = 128
VMEM_LIMIT_BYTES = 52 * 1024 * 1024

N_IN = 2 * GROUP_W + 3 * GROUP_W + 2 * N_HEADS + 4 * GROUP_W
GATE0 = 5 * GROUP_W
Q0, K0, V0, O0, ZA0, U0, ZB0, QC0, ZC0 = 0, 512, 1024, 1536, 2048, 2560, 3072, 3584, 4096
N_PROJ = 4608
GATE_ROWS = 16
P_V, P_O, P_ZA, P_ZB, P_QC, P_ZC = 0, 512, 1024, 1536, 2048, 2560
P_COLS = 3072

BF16 = jnp.bfloat16
F32 = jnp.float32


def _sigmoid(x):
    return 0.5 * jnp.tanh(0.5 * x) + 0.5


def _silu(x):
    hx = 0.5 * x
    return hx + hx * jnp.tanh(hx)


def _log_sigmoid(x):
    return jnp.minimum(x, 0.0) - jnp.log1p(jnp.exp(-jnp.abs(x)))


def _scan_lanes(x, op, identity):
    n = x.shape[1]
    lanes = lax.broadcasted_iota(jnp.int32, x.shape, 1)
    k = 1
    while k < n:
        x = op(x, jnp.where(lanes >= k, pltpu.roll(x, k, axis=1), identity))
        k *= 2
    return x


def _rmsnorm(x, g):
    ms = jnp.mean(x * x, axis=-1, keepdims=True)
    return (x * lax.rsqrt(ms + EPS)) * g


def _stage_a(x_ref, x_row, first, prev, cur, ng_ref, w_ref, wg_ref, h_ref):
    T = SEQ_TILE
    qk_prev, u_prev, _, _ = prev
    qk_cur, u_cur, proj_cur, g_cur = cur

    def head():
        qk_cur[0:CONV_HIST, :] = jnp.where(first, 0.0, qk_prev[T:T + CONV_HIST, :])
        u_cur[0:POOL_HIST, :] = jnp.where(first, 0.0, u_prev[T:T + POOL_HIST, :])
        h_ref[...] = _rmsnorm(x_ref[x_row:x_row + T, :], ng_ref[...]).astype(BF16)

    def proj(c0):
        return jnp.dot(h_ref[...], w_ref[:, c0:c0 + GROUP_W], preferred_element_type=F32)

    def dot_q():
        qk_cur[CONV_HIST:CONV_HIST + T, 0:GROUP_W] = proj(Q0)

    def dot_k():
        qk_cur[CONV_HIST:CONV_HIST + T, GROUP_W:2 * GROUP_W] = proj(K0)

    def dot_u():
        u_cur[POOL_HIST:POOL_HIST + T, :] = proj(U0)

    def dot_to(src, dst):
        def run():
            proj_cur[:, dst:dst + GROUP_W] = proj(src)
        return run

    def dot_gates():
        g_cur[...] = lax.dot_general(wg_ref[...], h_ref[...], (((1,), (1,)), ((), ())),
                                     preferred_element_type=F32)

    return [head, dot_q, dot_k, dot_u, dot_to(V0, P_V), dot_to(O0, P_O), dot_to(ZA0, P_ZA),
            dot_to(ZB0, P_ZB), dot_to(QC0, P_QC), dot_to(ZC0, P_ZC), dot_gates]


def _stage_b(x_ref, x_row, first, t_seq, cur, o_ref, kmt_ref, vmx_ref, bg_ref, cw_ref, cb_ref, mng_ref,
             pw_ref, ps_ref, wo_ref, fng_ref, q_ref, k_ref, vext_ref, cn_ref, m_ref, y_ref, final):
    T, L = SEQ_TILE, CHUNK
    qk_ext, u_ext, proj_ref, g_ref = cur

    def conv(c0):
        xs = qk_ext[:, c0:c0 + GROUP_W]
        acc = cb_ref[:, c0:c0 + GROUP_W] + cw_ref[CONV_K - 1:CONV_K, c0:c0 + GROUP_W] * xs[CONV_HIST:]
        for j in range(CONV_K - 1):
            back = pltpu.roll(xs, CONV_K - 1 - j, axis=0)[CONV_HIST:]
            acc = acc + cw_ref[j:j + 1, c0:c0 + GROUP_W] * back
        return _silu(acc)

    def conv_q():
        q_ref[...] = conv(0)

    def conv_k():
        k_ref[...] = conv(GROUP_W) * (HEAD_D ** -0.5)
        for hd in range(N_HEADS):
            vext_ref[:, hd * 2 * HEAD_D:hd * 2 * HEAD_D + HEAD_D] = (
                proj_ref[:, P_V + hd * HEAD_D:P_V + (hd + 1) * HEAD_D].astype(BF16))

    def gates():
        proj_ref[:, P_O:P_O + GROUP_W] = _sigmoid(proj_ref[:, P_O:P_O + GROUP_W])
        for c0 in (P_ZA, P_ZB, P_ZC):
            proj_ref[:, c0:c0 + GROUP_W] = _silu(proj_ref[:, c0:c0 + GROUP_W])

    def mlstm_chunk(c):
        def run():
            rowi = lax.broadcasted_iota(jnp.int32, (L, L), 0)
            coli = lax.broadcasted_iota(jnp.int32, (L, L), 1)
            causal = coli <= rowi
            r0 = c * L
            g8 = g_ref[0:2 * N_HEADS, r0:r0 + L] + bg_ref[:, 0:1]
            gi = g8
            logf = _log_sigmoid(pltpu.roll(g8, N_HEADS, axis=0))
            b = _scan_lanes(logf, jnp.add, 0.0)
            r = gi - b
            m_prev = m_ref[:, 0:1]
            if c == 0:
                m_prev = jnp.where(first, 0.0, m_prev)
            mm = jnp.maximum(_scan_lanes(r, jnp.maximum, -jnp.inf), m_prev)
            a = jnp.exp(m_prev - mm)
            e = jnp.exp(-(mm + b))
            m_ref[...] = jnp.broadcast_to((mm + b)[:, L - 1:L], m_ref.shape)
            cols = jnp.concatenate([mm, a, e, jnp.zeros((LANES - 6 * N_HEADS, L), F32)], axis=0).T
            mm_c, a_c, e_c = (cols[:, i * 2 * N_HEADS:(i + 1) * 2 * N_HEADS] for i in range(3))
            heads = [slice(hd * HEAD_D, (hd + 1) * HEAD_D) for hd in range(N_HEADS)]
            qs = [q_ref[r0:r0 + L, hs] for hs in heads]
            vxs = [vext_ref[r0:r0 + L, hd * 2 * HEAD_D:(hd + 1) * 2 * HEAD_D] for hd in range(N_HEADS)]
            s_bf, k_tw = [], []
            for hd, hs in enumerate(heads):
                arg = r[hd:hd + 1, :] - mm_c[:, hd:hd + 1]
                p = jnp.exp(jnp.where(causal, arg, -1e30))
                k_t = k_ref[r0:r0 + L, hs].T
                s = jnp.dot(qs[hd].astype(BF16), k_t.astype(BF16), preferred_element_type=F32) * p
                s_bf.append(s.astype(BF16))
                k_tw.append((k_t * p[L - 1:L, :]).astype(BF16))
            nds = []
            for hd in range(N_HEADS):
                cn = cn_ref[hd]
                if c == 0:
                    cn = jnp.where(first, 0.0, cn)
                lhs = jnp.concatenate([(a_c[:, hd:hd + 1] * qs[hd]).astype(BF16), s_bf[hd]], axis=1)
                rhs = jnp.concatenate([cn.astype(BF16), vxs[hd]], axis=0)
                nds.append(jnp.dot(lhs, rhs, preferred_element_type=F32))
                cn_ref[hd] = (a[hd:hd + 1, L - 1:L] * cn
                              + jnp.dot(k_tw[hd], vxs[hd], preferred_element_type=F32))
            for hd, hs in enumerate(heads):
                nd = nds[hd]
                hh = nd[:, 0:HEAD_D] / jnp.maximum(jnp.abs(nd[:, HEAD_D:HEAD_D + 1]), e_c[:, hd:hd + 1])
                hg = proj_ref[r0:r0 + L, P_O + hd * HEAD_D:P_O + (hd + 1) * HEAD_D] * hh
                hg = hg * lax.rsqrt(jnp.mean(hg * hg, axis=-1, keepdims=True) + EPS)
                hg = hg * mng_ref[:, hs]
                ya = hg * proj_ref[r0:r0 + L, P_ZA + hd * HEAD_D:P_ZA + (hd + 1) * HEAD_D]
                y_ref[r0:r0 + L, hs] = ya.astype(BF16)
        return run

    def pool():
        tg = (t_seq * T + lax.broadcasted_iota(jnp.int32, (T, LANES), 0) + 1).astype(F32)
        zero_w = jnp.zeros((LANES, LANES), BF16)
        for g0 in range(0, len(POOL_WINDOWS), 2):
            pooled = []
            for g in (g0, g0 + 1):
                win = POOL_WINDOWS[g]
                gs = slice(g * LANES, (g + 1) * LANES)
                ssum = u_ext[:, gs]
                sh = 1
                while sh < win:
                    ssum = ssum + pltpu.roll(ssum, sh, axis=0)
                    sh *= 2
                u = u_ext[POOL_HIST:POOL_HIST + T, gs]
                pooled.append((ssum[POOL_HIST:POOL_HIST + T] / jnp.minimum(tg, float(win)) - u).astype(BF16))
            w_a, w_b = pw_ref[g0].astype(BF16), pw_ref[g0 + 1].astype(BF16)
            w_pair = jnp.concatenate([jnp.concatenate([w_a, zero_w], axis=1),
                                      jnp.concatenate([zero_w, w_b], axis=1)], axis=0)
            yb = jnp.dot(jnp.concatenate(pooled, axis=1), w_pair, preferred_element_type=F32)
            yb = (yb * ps_ref[:, g0 * LANES:(g0 + 2) * LANES]
                  * proj_ref[:, P_ZB + g0 * LANES:P_ZB + (g0 + 2) * LANES])
            y_ref[:, GROUP_W + g0 * LANES:GROUP_W + (g0 + 2) * LANES] = yb.astype(BF16)

    def attention():
        heads = [slice(hd * HEAD_D, (hd + 1) * HEAD_D) for hd in range(N_HEADS)]
        scores = []
        for hd, hs in enumerate(heads):
            qc = proj_ref[:, P_QC + hd * HEAD_D:P_QC + (hd + 1) * HEAD_D].astype(BF16)
            scores.append(jnp.dot(qc, kmt_ref[hs, :], preferred_element_type=F32) * (HEAD_D ** -0.5))
        pexps = [jnp.exp(s - jnp.max(s, axis=-1, keepdims=True)).astype(BF16) for s in scores]
        pvs = [jnp.dot(pexps[hd], vmx_ref[hd], preferred_element_type=F32)
               for hd in range(N_HEADS)]
        for hd in range(N_HEADS):
            yc = pvs[hd][:, 0:HEAD_D] / pvs[hd][:, HEAD_D:HEAD_D + 1]
            yc = yc * proj_ref[:, P_ZC + hd * HEAD_D:P_ZC + (hd + 1) * HEAD_D]
            y_ref[:, 2 * GROUP_W + hd * HEAD_D:2 * GROUP_W + (hd + 1) * HEAD_D] = yc.astype(BF16)

    def out_proj():
        rows = slice(x_row, x_row + T)
        for i in range(D_MODEL // GROUP_W):
            cs = slice(i * GROUP_W, (i + 1) * GROUP_W)
            o_ref[rows, cs] = x_ref[rows, cs] + jnp.dot(y_ref[...], wo_ref[i], preferred_element_type=F32)
        if final:
            o_ref[rows, :] = _rmsnorm(o_ref[rows, :], fng_ref[...])

    return dict(conv_q=conv_q, conv_k=conv_k, gates=gates, chunks=[mlstm_chunk(c) for c in range(T // L)],
                pool=pool, attention=attention, out_proj=out_proj)


def _layer_kernel(xp_ref, xn_ref, kmt_ref, vmx_ref, ng_ref, w_ref, wg_ref, bg_ref, cw_ref, cb_ref, mng_ref,
                  pw_ref, ps_ref, wo_ref, fng_ref, o_ref,
                  h_ref, qk0, u0, proj0, g0, qk1, u1, proj1, g1, q_ref, k_ref, vext_ref, cn_ref, m_ref, y_ref,
                  *, final, tiles_per_seq):
    T = SEQ_TILE
    k = pl.program_id(0)
    set0, set1 = (qk0, u0, proj0, g0), (qk1, u1, proj1, g1)
    a_args = (ng_ref, w_ref, wg_ref, h_ref)
    b_args = (kmt_ref, vmx_ref, bg_ref, cw_ref, cb_ref, mng_ref, pw_ref, ps_ref, wo_ref, fng_ref,
              q_ref, k_ref, vext_ref, cn_ref, m_ref, y_ref, final)

    @pl.when(k == 0)
    def _prologue():
        col = lax.broadcasted_iota(jnp.int32, vext_ref.shape, 1)
        vext_ref[...] = jnp.where(col % (2 * HEAD_D) == HEAD_D, 1.0, 0.0).astype(BF16)
        cn_ref[...] = jnp.zeros(cn_ref.shape, F32)
        m_ref[...] = jnp.zeros(m_ref.shape, F32)
        qk1[T:T + CONV_HIST, :] = jnp.zeros((CONV_HIST, 2 * GROUP_W), F32)
        u1[T:T + POOL_HIST, :] = jnp.zeros((POOL_HIST, GROUP_W), F32)
        for piece in _stage_a(xp_ref, 0, True, set1, set0, *a_args):
            piece()

    t_even = (2 * k) % tiles_per_seq
    first_even = t_even == 0
    first_next = (2 * k + 2) % tiles_per_seq == 0

    def half_step(x_row, first, t_seq, cur, nxt, xa_ref, xa_row, first_a):
        for piece in _stage_a(xa_ref, xa_row, first_a, cur, nxt, *a_args):
            piece()
        b = _stage_b(xp_ref, x_row, first, t_seq, cur, o_ref, *b_args)
        for phase in (b["conv_q"], b["conv_k"], b["gates"], *b["chunks"], b["pool"], b["attention"],
                      b["out_proj"]):
            phase()

    half_step(0, first_even, t_even, set0, set1, xp_ref, T, False)
    half_step(T, False, t_even + 1, set1, set0, xn_ref, 0, first_next)


def _mem_kv_kernel(mem_ref, g_ref, w_ref, kmt_ref, vmx_ref, wb_ref):
    @pl.when(pl.program_id(0) == 0)
    def _cast_weights():
        wb_ref[...] = w_ref[...].astype(BF16)

    mem = mem_ref[...]
    ms = jnp.mean(mem * mem, axis=-1, keepdims=True)
    nrm = mem * lax.rsqrt(ms + EPS)
    col = lax.broadcasted_iota(jnp.int32, (N_MEM, HEAD_D), 1)
    ones_col = jnp.where(col == 0, 1.0, 0.0).astype(BF16)
    for l in range(wb_ref.shape[0]):
        mn = (nrm * g_ref[l]).astype(BF16)
        kv = jnp.dot(mn, wb_ref[l], preferred_element_type=F32)
        for hd in range(N_HEADS):
            hs = slice(hd * HEAD_D, (hd + 1) * HEAD_D)
            kmt_ref[l, hs, :] = kv[:, hs].T.astype(BF16)
            vmx_ref[l, hd, :, 0:HEAD_D] = kv[:, GROUP_W + hd * HEAD_D:GROUP_W + (hd + 1) * HEAD_D].astype(BF16)
            vmx_ref[l, hd, :, HEAD_D:2 * HEAD_D] = ones_col


def _prep_kernel(wint_ref, wout_ref, wall_ref, wg_ref, wo_ref):
    wall_ref[:, 0:GATE0] = wint_ref[0:GATE0, :].T.astype(BF16)
    wall_ref[:, GATE0:N_PROJ] = wint_ref[GATE0 + 2 * N_HEADS:N_IN, :].T.astype(BF16)
    wg_ref[0:2 * N_HEADS, :] = wint_ref[GATE0:GATE0 + 2 * N_HEADS, :].astype(BF16)
    wg_ref[2 * N_HEADS:GATE_ROWS, :] = jnp.zeros((GATE_ROWS - 2 * N_HEADS, wg_ref.shape[1]), BF16)
    for i in range(D_MODEL // GROUP_W):
        wo_ref[i] = wout_ref[:, i * GROUP_W:(i + 1) * GROUP_W].astype(BF16)


def _resident(shape, layer):
    nd = len(shape)
    return pl.BlockSpec((None,) + shape, lambda k: (layer,) + (0,) * nd, pipeline_mode=pl.Buffered(1))


def _prep_weights(w_in, w_out):
    depth = w_in.shape[0]
    steps = 4
    rb_in, rb_out = D_MODEL // steps, 3 * GROUP_W // steps
    return pl.pallas_call(
        _prep_kernel,
        grid=(depth, steps),
        in_specs=[
            pl.BlockSpec((None, N_IN, rb_in), lambda l, i: (l, 0, i)),
            pl.BlockSpec((None, rb_out, D_MODEL), lambda l, i: (l, i, 0)),
        ],
        out_specs=[
            pl.BlockSpec((None, rb_in, N_PROJ), lambda l, i: (l, i, 0)),
            pl.BlockSpec((None, GATE_ROWS, rb_in), lambda l, i: (l, 0, i)),
            pl.BlockSpec((None, D_MODEL // GROUP_W, rb_out, GROUP_W), lambda l, i: (l, 0, i, 0)),
        ],
        out_shape=[
            jax.ShapeDtypeStruct((depth, D_MODEL, N_PROJ), BF16),
            jax.ShapeDtypeStruct((depth, GATE_ROWS, D_MODEL), BF16),
            jax.ShapeDtypeStruct((depth, D_MODEL // GROUP_W, 3 * GROUP_W, GROUP_W), BF16),
        ],
        compiler_params=pltpu.CompilerParams(
            dimension_semantics=("arbitrary", "arbitrary"), vmem_limit_bytes=VMEM_LIMIT_BYTES),
        name="prep_weights",
    )(jnp.swapaxes(w_in, 1, 2), w_out)


def _mem_kv(mem, mem_norm_g, w_kv):
    depth, batch = w_kv.shape[0], mem.shape[0]
    return pl.pallas_call(
        _mem_kv_kernel,
        grid=(batch,),
        in_specs=[
            pl.BlockSpec((None, N_MEM, D_MODEL), lambda b: (b, 0, 0)),
            pl.BlockSpec((depth, 1, D_MODEL), lambda b: (0, 0, 0)),
            pl.BlockSpec((depth, D_MODEL, 2 * GROUP_W), lambda b: (0, 0, 0), pipeline_mode=pl.Buffered(1)),
        ],
        out_specs=[
            pl.BlockSpec((depth, None, GROUP_W, N_MEM), lambda b: (0, b, 0, 0)),
            pl.BlockSpec((depth, None, N_HEADS, N_MEM, 2 * HEAD_D), lambda b: (0, b, 0, 0, 0)),
        ],
        out_shape=[
            jax.ShapeDtypeStruct((depth, batch, GROUP_W, N_MEM), BF16),
            jax.ShapeDtypeStruct((depth, batch, N_HEADS, N_MEM, 2 * HEAD_D), BF16),
        ],
        scratch_shapes=[pltpu.VMEM((depth, D_MODEL, 2 * GROUP_W), BF16)],
        compiler_params=pltpu.CompilerParams(
            dimension_semantics=("arbitrary",), vmem_limit_bytes=VMEM_LIMIT_BYTES),
        name="mem_kv",
    )(mem, mem_norm_g.reshape(depth, 1, D_MODEL), w_kv)


def _layer(layer, final, x, kmt, vmx, ng, w_all, w_g, b_g, conv_w, conv_b, mng, pool_w, pool_scale, w_out, fng):
    batch, seq, _ = x.shape
    T = SEQ_TILE
    assert seq % (2 * T) == 0 and T % CHUNK == 0
    tiles_per_seq = seq // T
    pairs_per_seq = tiles_per_seq // 2
    last_tile = batch * tiles_per_seq - 1

    def next_tile(k):
        nt = jnp.minimum(2 * k + 2, last_tile)
        return nt // tiles_per_seq, nt % tiles_per_seq, 0

    kern = functools.partial(_layer_kernel, final=final, tiles_per_seq=tiles_per_seq)
    set_scratch = [
        pltpu.VMEM((CONV_HIST + T, 2 * GROUP_W), F32),
        pltpu.VMEM((POOL_HIST + T, GROUP_W), F32),
        pltpu.VMEM((T, P_COLS), F32),
        pltpu.VMEM((GATE_ROWS, T), F32),
    ]
    return pl.pallas_call(
        kern,
        grid=(batch * pairs_per_seq,),
        in_specs=[
            pl.BlockSpec((None, 2 * T, D_MODEL), lambda k: (k // pairs_per_seq, k % pairs_per_seq, 0)),
            pl.BlockSpec((None, T, D_MODEL), next_tile),
            pl.BlockSpec((None, None, GROUP_W, N_MEM), lambda k: (layer, k // pairs_per_seq, 0, 0)),
            pl.BlockSpec((None, None, N_HEADS, N_MEM, 2 * HEAD_D),
                         lambda k: (layer, k // pairs_per_seq, 0, 0, 0)),
            _resident((1, D_MODEL), layer),
            _resident((D_MODEL, N_PROJ), layer),
            _resident((GATE_ROWS, D_MODEL), layer),
            _resident((2 * N_HEADS, LANES), layer),
            _resident((CONV_K, 2 * GROUP_W), layer),
            _resident((1, 2 * GROUP_W), layer),
            _resident((1, GROUP_W), layer),
            _resident((len(POOL_WINDOWS), LANES, LANES), layer),
            _resident((1, GROUP_W), layer),
            _resident((D_MODEL // GROUP_W, 3 * GROUP_W, GROUP_W), layer),
            pl.BlockSpec((1, D_MODEL), lambda k: (0, 0), pipeline_mode=pl.Buffered(1)),
        ],
        out_specs=pl.BlockSpec((None, 2 * T, D_MODEL), lambda k: (k // pairs_per_seq, k % pairs_per_seq, 0)),
        out_shape=jax.ShapeDtypeStruct(x.shape, F32),
        scratch_shapes=[pltpu.VMEM((T, D_MODEL), BF16)] + set_scratch + set_scratch + [
            pltpu.VMEM((T, GROUP_W), F32),
            pltpu.VMEM((T, GROUP_W), F32),
            pltpu.VMEM((T, N_HEADS * 2 * HEAD_D), BF16),
            pltpu.VMEM((N_HEADS, HEAD_D, 2 * HEAD_D), F32),
            pltpu.VMEM((2 * N_HEADS, LANES), F32),
            pltpu.VMEM((T, 3 * GROUP_W), BF16),
        ],
        compiler_params=pltpu.CompilerParams(
            dimension_semantics=("arbitrary",), vmem_limit_bytes=VMEM_LIMIT_BYTES),
        name="trunk_layer_final" if final else "trunk_layer",
    )(x, x, kmt, vmx, ng, w_all, w_g, b_g, conv_w, conv_b, mng, pool_w, pool_scale, w_out, fng)


def kernel(x, mem, norm_g, w_in, b_gates, conv_w, conv_b, mlstm_norm_g, pool_w, pool_scale,
           mem_norm_g, w_mem_kv, w_out, final_norm_g):
    depth = w_in.shape[0]
    w_all, w_g, w_out_b = _prep_weights(w_in, w_out)
    kmt, vmx = _mem_kv(mem, mem_norm_g, w_mem_kv)
    b_g = jnp.broadcast_to(b_gates[:, :, None], (depth, 2 * N_HEADS, LANES))
    args = (kmt, vmx, norm_g.reshape(depth, 1, D_MODEL), w_all, w_g, b_g,
            conv_w, conv_b.reshape(depth, 1, 2 * GROUP_W), mlstm_norm_g.reshape(depth, 1, GROUP_W),
            pool_w, pool_scale.reshape(depth, 1, GROUP_W), w_out_b, final_norm_g.reshape(1, D_MODEL))
    for l in range(depth):
        x = _layer(l, l == depth - 1, x, *args)
    return x
```

```python
import functools

import jax
import jax.numpy as jnp
from jax import lax
from jax.experimental import pallas as pl
from jax.experimental.pallas import tpu as pltpu

D_MODEL = 1024
GROUP_W = 512
N_HEADS = 4
HEAD_D = 128
CONV_K = 4
POOL_WINDOWS = (2, 4, 8, 16)
N_MEM = 256
EPS = 1e-6

SEQ_TILE = 256
CHUNK = 256
CONV_HIST = 128
POOL_HIST = 16
LANES = 128
VMEM_LIMIT_BYTES = 52 * 1024 * 1024

N_IN = 2 * GROUP_W + 3 * GROUP_W + 2 * N_HEADS + 4 * GROUP_W
GATE0 = 5 * GROUP_W
V0, O0, ZA0, U0, ZB0, QC0, ZC0 = 0, 512, 1024, 1536, 2048, 2560, 3072
N_PROJ = 3584
QK_W = 2 * GROUP_W
GATE_ROWS = 16
P_V, P_O, P_ZA, P_ZB, P_QC, P_ZC = 0, 512, 1024, 1536, 2048, 2560
P_COLS = 3072

BF16 = jnp.bfloat16
F32 = jnp.float32


def _sigmoid(x):
    return 0.5 * jnp.tanh(0.5 * x) + 0.5


def _silu(x):
    hx = 0.5 * x
    return hx + hx * jnp.tanh(hx)


def _log_sigmoid(x):
    return jnp.minimum(x, 0.0) - jnp.log1p(jnp.exp(-jnp.abs(x)))


def _scan_lanes(x, op, identity):
    n = x.shape[1]
    lanes = lax.broadcasted_iota(jnp.int32, x.shape, 1)
    k = 1
    while k < n:
        x = op(x, jnp.where(lanes >= k, pltpu.roll(x, k, axis=1), identity))
        k *= 2
    return x


def _rmsnorm(x, g):
    ms = jnp.mean(x * x, axis=-1, keepdims=True)
    return (x * lax.rsqrt(ms + EPS)) * g


def _stage_a(x_ref, x_row, first, prev, cur, ng_ref, w_ref, wqk_ref, wg_ref, h_ref):
    T = SEQ_TILE
    qk_prev, u_prev, _, _ = prev
    qk_cur, u_cur, proj_cur, g_cur = cur

    def head():
        qk_cur[:, 0:CONV_HIST] = jnp.where(first, 0.0, qk_prev[:, T:T + CONV_HIST])
        u_cur[0:POOL_HIST, :] = jnp.where(first, 0.0, u_prev[T:T + POOL_HIST, :])
        h_ref[...] = _rmsnorm(x_ref[x_row:x_row + T, :], ng_ref[...]).astype(BF16)

    def proj(c0):
        return jnp.dot(h_ref[...], w_ref[:, c0:c0 + GROUP_W], preferred_element_type=F32)

    def dot_qk(c0):
        def run():
            qk_cur[c0:c0 + GROUP_W, CONV_HIST:CONV_HIST + T] = lax.dot_general(
                wqk_ref[c0:c0 + GROUP_W, :], h_ref[...], (((1,), (1,)), ((), ())), preferred_element_type=F32)
        return run

    def dot_u():
        u_cur[POOL_HIST:POOL_HIST + T, :] = proj(U0)

    def dot_to(src, dst):
        def run():
            proj_cur[:, dst:dst + GROUP_W] = proj(src)
        return run

    def dot_gates():
        g_cur[...] = lax.dot_general(wg_ref[...], h_ref[...], (((1,), (1,)), ((), ())),
                                     preferred_element_type=F32)

    return [head, dot_qk(0), dot_qk(GROUP_W), dot_u, dot_to(V0, P_V), dot_to(O0, P_O), dot_to(ZA0, P_ZA),
            dot_to(ZB0, P_ZB), dot_to(QC0, P_QC), dot_to(ZC0, P_ZC), dot_gates]


def _stage_b(x_ref, x_row, first, t_seq, cur, o_ref, kmt_ref, vmx_ref, bg_ref, cw_ref, cb_ref, mng_ref,
             pw_ref, ps_ref, wo_ref, fng_ref, q_ref, k_ref, vext_ref, cn_ref, m_ref, y_ref, final):
    T, L = SEQ_TILE, CHUNK
    qk_ext, u_ext, proj_ref, g_ref = cur

    def conv(c0):
        xs = qk_ext[c0:c0 + GROUP_W, :]
        taps = [pltpu.roll(xs, CONV_K - 1 - j, axis=1) for j in range(CONV_K - 1)] + [xs]
        out = []
        for lt in range(CONV_HIST, CONV_HIST + T, LANES):
            acc = cb_ref[c0:c0 + GROUP_W, :]
            for j in range(CONV_K):
                acc = acc + cw_ref[j, c0:c0 + GROUP_W, :] * taps[j][:, lt:lt + LANES]
            out.append(_silu(acc))
        return jnp.concatenate(out, axis=1)

    def conv_q():
        q_t = conv(0)
        for hd in range(N_HEADS):
            q_ref[:, hd * HEAD_D:(hd + 1) * HEAD_D] = q_t[hd * HEAD_D:(hd + 1) * HEAD_D, :].T

    def conv_k():
        k_ref[...] = conv(GROUP_W) * (HEAD_D ** -0.5)
        for hd in range(N_HEADS):
            vext_ref[:, hd * 2 * HEAD_D:hd * 2 * HEAD_D + HEAD_D] = (
                proj_ref[:, P_V + hd * HEAD_D:P_V + (hd + 1) * HEAD_D].astype(BF16))

    def mlstm_chunk(c):
        def run():
            rowi = lax.broadcasted_iota(jnp.int32, (L, L), 0)
            coli = lax.broadcasted_iota(jnp.int32, (L, L), 1)
            causal = coli <= rowi
            r0 = c * L
            g8 = g_ref[0:2 * N_HEADS, r0:r0 + L] + bg_ref[:, 0:1]
            gi = g8
            logf = _log_sigmoid(pltpu.roll(g8, N_HEADS, axis=0))
            b = _scan_lanes(logf, jnp.add, 0.0)
            r = gi - b
            m_prev = m_ref[:, 0:1]
            if c == 0:
                m_prev = jnp.where(first, 0.0, m_prev)
            mm = jnp.maximum(_scan_lanes(r, jnp.maximum, -jnp.inf), m_prev)
            a = jnp.exp(m_prev - mm)
            e = jnp.exp(-(mm + b))
            m_ref[...] = jnp.broadcast_to((mm + b)[:, L - 1:L], m_ref.shape)
            cols = jnp.concatenate([mm, a, e, jnp.zeros((LANES - 6 * N_HEADS, L), F32)], axis=0).T
            mm_c, a_c, e_c = (cols[:, i * 2 * N_HEADS:(i + 1) * 2 * N_HEADS] for i in range(3))
            heads = [slice(hd * HEAD_D, (hd + 1) * HEAD_D) for hd in range(N_HEADS)]
            qs = [q_ref[r0:r0 + L, hs] for hs in heads]
            vxs = [vext_ref[r0:r0 + L, hd * 2 * HEAD_D:(hd + 1) * 2 * HEAD_D] for hd in range(N_HEADS)]
            s_bf, k_tw = [], []
            for hd, hs in enumerate(heads):
                arg = r[hd:hd + 1, :] - mm_c[:, hd:hd + 1]
                p = jnp.exp(jnp.where(causal, arg, -1e30))
                k_t = k_ref[hs, r0:r0 + L]
                s = jnp.dot(qs[hd].astype(BF16), k_t.astype(BF16), preferred_element_type=F32) * p
                s_bf.append(s.astype(BF16))
                k_tw.append((k_t * p[L - 1:L, :]).astype(BF16))
            nds = []
            for hd in range(N_HEADS):
                cn = cn_ref[hd]
                if c == 0:
                    cn = jnp.where(first, 0.0, cn)
                lhs = jnp.concatenate([(a_c[:, hd:hd + 1] * qs[hd]).astype(BF16), s_bf[hd]], axis=1)
                rhs = jnp.concatenate([cn.astype(BF16), vxs[hd]], axis=0)
                nds.append(jnp.dot(lhs, rhs, preferred_element_type=F32))
                cn_ref[hd] = (a[hd:hd + 1, L - 1:L] * cn
                              + jnp.dot(k_tw[hd], vxs[hd], preferred_element_type=F32))
            for hd, hs in enumerate(heads):
                nd = nds[hd]
                hh = nd[:, 0:HEAD_D] / jnp.maximum(jnp.abs(nd[:, HEAD_D:HEAD_D + 1]), e_c[:, hd:hd + 1])
                hg = _sigmoid(proj_ref[r0:r0 + L, P_O + hd * HEAD_D:P_O + (hd + 1) * HEAD_D]) * hh
                hg = hg * lax.rsqrt(jnp.mean(hg * hg, axis=-1, keepdims=True) + EPS)
                hg = hg * mng_ref[:, hs]
                ya = hg * _silu(proj_ref[r0:r0 + L, P_ZA + hd * HEAD_D:P_ZA + (hd + 1) * HEAD_D])
                y_ref[r0:r0 + L, hs] = ya.astype(BF16)
        return run

    def pool():
        tg = (t_seq * T + lax.broadcasted_iota(jnp.int32, (T, LANES), 0) + 1).astype(F32)
        zero_w = jnp.zeros((LANES, LANES), BF16)
        for g0 in range(0, len(POOL_WINDOWS), 2):
            pooled = []
            for g in (g0, g0 + 1):
                win = POOL_WINDOWS[g]
                gs = slice(g * LANES, (g + 1) * LANES)
                ssum = u_ext[:, gs]
                sh = 1
                while sh < win:
                    ssum = ssum + pltpu.roll(ssum, sh, axis=0)
                    sh *= 2
                u = u_ext[POOL_HIST:POOL_HIST + T, gs]
                pooled.append((ssum[POOL_HIST:POOL_HIST + T] / jnp.minimum(tg, float(win)) - u).astype(BF16))
            w_a, w_b = pw_ref[g0].astype(BF16), pw_ref[g0 + 1].astype(BF16)
            w_pair = jnp.concatenate([jnp.concatenate([w_a, zero_w], axis=1),
                                      jnp.concatenate([zero_w, w_b], axis=1)], axis=0)
            yb = jnp.dot(jnp.concatenate(pooled, axis=1), w_pair, preferred_element_type=F32)
            yb = (yb * ps_ref[:, g0 * LANES:(g0 + 2) * LANES]
                  * _silu(proj_ref[:, P_ZB + g0 * LANES:P_ZB + (g0 + 2) * LANES]))
            y_ref[:, GROUP_W + g0 * LANES:GROUP_W + (g0 + 2) * LANES] = yb.astype(BF16)

    def attention():
        for hd in range(N_HEADS):
            hs = slice(hd * HEAD_D, (hd + 1) * HEAD_D)
            qc = proj_ref[:, P_QC + hd * HEAD_D:P_QC + (hd + 1) * HEAD_D].astype(BF16)
            s = jnp.dot(qc, kmt_ref[hs, :], preferred_element_type=F32) * (HEAD_D ** -0.5)
            pexp = jnp.exp(s - jnp.max(s, axis=-1, keepdims=True))
            pv = jnp.dot(pexp.astype(BF16), vmx_ref[hd], preferred_element_type=F32)
            yc = pv[:, 0:HEAD_D] / pv[:, HEAD_D:HEAD_D + 1]
            yc = yc * _silu(proj_ref[:, P_ZC + hd * HEAD_D:P_ZC + (hd + 1) * HEAD_D])
            y_ref[:, 2 * GROUP_W + hd * HEAD_D:2 * GROUP_W + (hd + 1) * HEAD_D] = yc.astype(BF16)

    def out_proj():
        rows = slice(x_row, x_row + T)
        for i in range(D_MODEL // GROUP_W):
            cs = slice(i * GROUP_W, (i + 1) * GROUP_W)
            o_ref[rows, cs] = x_ref[rows, cs] + jnp.dot(y_ref[...], wo_ref[i], preferred_element_type=F32)
        if final:
            o_ref[rows, :] = _rmsnorm(o_ref[rows, :], fng_ref[...])

    return dict(conv_q=conv_q, conv_k=conv_k, chunks=[mlstm_chunk(c) for c in range(T // L)],
                pool=pool, attention=attention, out_proj=out_proj)


def _layer_kernel(xp_ref, xn_ref, kmt_ref, vmx_ref, ng_ref, w_ref, wqk_ref, wg_ref, bg_ref, cw_ref, cb_ref, mng_ref,
                  pw_ref, ps_ref, wo_ref, fng_ref, o_ref,
                  h_ref, qk0, u0, proj0, g0, qk1, u1, proj1, g1, q_ref, k_ref, vext_ref, cn_ref, m_ref, y_ref,
                  *, final, tiles_per_seq):
    T = SEQ_TILE
    k = pl.program_id(0)
    set0, set1 = (qk0, u0, proj0, g0), (qk1, u1, proj1, g1)
    a_args = (ng_ref, w_ref, wqk_ref, wg_ref, h_ref)
    b_args = (kmt_ref, vmx_ref, bg_ref, cw_ref, cb_ref, mng_ref, pw_ref, ps_ref, wo_ref, fng_ref,
              q_ref, k_ref, vext_ref, cn_ref, m_ref, y_ref, final)

    @pl.when(k == 0)
    def _prologue():
        col = lax.broadcasted_iota(jnp.int32, vext_ref.shape, 1)
        vext_ref[...] = jnp.where(col % (2 * HEAD_D) == HEAD_D, 1.0, 0.0).astype(BF16)
        cn_ref[...] = jnp.zeros(cn_ref.shape, F32)
        m_ref[...] = jnp.zeros(m_ref.shape, F32)
        qk1[:, T:T + CONV_HIST] = jnp.zeros((QK_W, CONV_HIST), F32)
        u1[T:T + POOL_HIST, :] = jnp.zeros((POOL_HIST, GROUP_W), F32)
        for piece in _stage_a(xp_ref, 0, True, set1, set0, *a_args):
            piece()

    t_even = (2 * k) % tiles_per_seq
    first_even = t_even == 0
    first_next = (2 * k + 2) % tiles_per_seq == 0

    def half_step(x_row, first, t_seq, cur, nxt, xa_ref, xa_row, first_a):
        for piece in _stage_a(xa_ref, xa_row, first_a, cur, nxt, *a_args):
            piece()
        b = _stage_b(xp_ref, x_row, first, t_seq, cur, o_ref, *b_args)
        for phase in (b["conv_q"], b["conv_k"], *b["chunks"], b["pool"], b["attention"], b["out_proj"]):
            phase()

    half_step(0, first_even, t_even, set0, set1, xp_ref, T, False)
    half_step(T, False, t_even + 1, set1, set0, xn_ref, 0, first_next)


def _mem_kv_kernel(mem_ref, g_ref, w_ref, kmt_ref, vmx_ref, wb_ref):
    @pl.when(pl.program_id(0) == 0)
    def _cast_weights():
        wb_ref[...] = w_ref[...].astype(BF16)

    mem = mem_ref[...]
    ms = jnp.mean(mem * mem, axis=-1, keepdims=True)
    nrm = mem * lax.rsqrt(ms + EPS)
    col = lax.broadcasted_iota(jnp.int32, (N_MEM, HEAD_D), 1)
    ones_col = jnp.where(col == 0, 1.0, 0.0).astype(BF16)
    for l in range(wb_ref.shape[0]):
        mn = (nrm * g_ref[l]).astype(BF16)
        kv = jnp.dot(mn, wb_ref[l], preferred_element_type=F32)
        for hd in range(N_HEADS):
            hs = slice(hd * HEAD_D, (hd + 1) * HEAD_D)
            kmt_ref[l, hs, :] = kv[:, hs].T.astype(BF16)
            vmx_ref[l, hd, :, 0:HEAD_D] = kv[:, GROUP_W + hd * HEAD_D:GROUP_W + (hd + 1) * HEAD_D].astype(BF16)
            vmx_ref[l, hd, :, HEAD_D:2 * HEAD_D] = ones_col


def _prep_kernel(wint_ref, wout_ref, wall_ref, wqk_ref, wg_ref, wo_ref):
    wqk_ref[...] = wint_ref[0:QK_W, :].astype(BF16)
    wall_ref[:, 0:GATE0 - QK_W] = wint_ref[QK_W:GATE0, :].T.astype(BF16)
    wall_ref[:, GATE0 - QK_W:N_PROJ] = wint_ref[GATE0 + 2 * N_HEADS:N_IN, :].T.astype(BF16)
    wg_ref[0:2 * N_HEADS, :] = wint_ref[GATE0:GATE0 + 2 * N_HEADS, :].astype(BF16)
    wg_ref[2 * N_HEADS:GATE_ROWS, :] = jnp.zeros((GATE_ROWS - 2 * N_HEADS, wg_ref.shape[1]), BF16)
    for i in range(D_MODEL // GROUP_W):
        wo_ref[i] = wout_ref[:, i * GROUP_W:(i + 1) * GROUP_W].astype(BF16)


def _resident(shape, layer):
    nd = len(shape)
    return pl.BlockSpec((None,) + shape, lambda k: (layer,) + (0,) * nd, pipeline_mode=pl.Buffered(1))


def _prep_weights(w_in, w_out):
    depth = w_in.shape[0]
    steps = 4
    rb_in, rb_out = D_MODEL // steps, 3 * GROUP_W // steps
    return pl.pallas_call(
        _prep_kernel,
        grid=(depth, steps),
        in_specs=[
            pl.BlockSpec((None, N_IN, rb_in), lambda l, i: (l, 0, i)),
            pl.BlockSpec((None, rb_out, D_MODEL), lambda l, i: (l, i, 0)),
        ],
        out_specs=[
            pl.BlockSpec((None, rb_in, N_PROJ), lambda l, i: (l, i, 0)),
            pl.BlockSpec((None, QK_W, rb_in), lambda l, i: (l, 0, i)),
            pl.BlockSpec((None, GATE_ROWS, rb_in), lambda l, i: (l, 0, i)),
            pl.BlockSpec((None, D_MODEL // GROUP_W, rb_out, GROUP_W), lambda l, i: (l, 0, i, 0)),
        ],
        out_shape=[
            jax.ShapeDtypeStruct((depth, D_MODEL, N_PROJ), BF16),
            jax.ShapeDtypeStruct((depth, QK_W, D_MODEL), BF16),
            jax.ShapeDtypeStruct((depth, GATE_ROWS, D_MODEL), BF16),
            jax.ShapeDtypeStruct((depth, D_MODEL // GROUP_W, 3 * GROUP_W, GROUP_W), BF16),
        ],
        compiler_params=pltpu.CompilerParams(
            dimension_semantics=("arbitrary", "arbitrary"), vmem_limit_bytes=VMEM_LIMIT_BYTES),
        name="prep_weights",
    )(jnp.swapaxes(w_in, 1, 2), w_out)


def _mem_kv(mem, mem_norm_g, w_kv):
    depth, batch = w_kv.shape[0], mem.shape[0]
    return pl.pallas_call(
        _mem_kv_kernel,
        grid=(batch,),
        in_specs=[
            pl.BlockSpec((None, N_MEM, D_MODEL), lambda b: (b, 0, 0)),
            pl.BlockSpec((depth, 1, D_MODEL), lambda b: (0, 0, 0)),
            pl.BlockSpec((depth, D_MODEL, 2 * GROUP_W), lambda b: (0, 0, 0), pipeline_mode=pl.Buffered(1)),
        ],
        out_specs=[
            pl.BlockSpec((depth, None, GROUP_W, N_MEM), lambda b: (0, b, 0, 0)),
            pl.BlockSpec((depth, None, N_HEADS, N_MEM, 2 * HEAD_D), lambda b: (0, b, 0, 0, 0)),
        ],
        out_shape=[
            jax.ShapeDtypeStruct((depth, batch, GROUP_W, N_MEM), BF16),
            jax.ShapeDtypeStruct((depth, batch, N_HEADS, N_MEM, 2 * HEAD_D), BF16),
        ],
        scratch_shapes=[pltpu.VMEM((depth, D_MODEL, 2 * GROUP_W), BF16)],
        compiler_params=pltpu.CompilerParams(
            dimension_semantics=("arbitrary",), vmem_limit_bytes=VMEM_LIMIT_BYTES),
        name="mem_kv",
    )(mem, mem_norm_g.reshape(depth, 1, D_MODEL), w_kv)


def _layer(layer, final, x, kmt, vmx, ng, w_all, w_qk, w_g, b_g, conv_w, conv_b, mng, pool_w, pool_scale, w_out, fng):
    batch, seq, _ = x.shape
    T = SEQ_TILE
    assert seq % (2 * T) == 0 and T % CHUNK == 0
    tiles_per_seq = seq // T
    pairs_per_seq = tiles_per_seq // 2
    last_tile = batch * tiles_per_seq - 1

    def next_tile(k):
        nt = jnp.minimum(2 * k + 2, last_tile)
        return nt // tiles_per_seq, nt % tiles_per_seq, 0

    kern = functools.partial(_layer_kernel, final=final, tiles_per_seq=tiles_per_seq)
    set_scratch = [
        pltpu.VMEM((QK_W, CONV_HIST + T), F32),
        pltpu.VMEM((POOL_HIST + T, GROUP_W), F32),
        pltpu.VMEM((T, P_COLS), F32),
        pltpu.VMEM((GATE_ROWS, T), F32),
    ]
    return pl.pallas_call(
        kern,
        grid=(batch * pairs_per_seq,),
        in_specs=[
            pl.BlockSpec((None, 2 * T, D_MODEL), lambda k: (k // pairs_per_seq, k % pairs_per_seq, 0)),
            pl.BlockSpec((None, T, D_MODEL), next_tile),
            pl.BlockSpec((None, None, GROUP_W, N_MEM), lambda k: (layer, k // pairs_per_seq, 0, 0)),
            pl.BlockSpec((None, None, N_HEADS, N_MEM, 2 * HEAD_D),
                         lambda k: (layer, k // pairs_per_seq, 0, 0, 0)),
            _resident((1, D_MODEL), layer),
            _resident((D_MODEL, N_PROJ), layer),
            _resident((QK_W, D_MODEL), layer),
            _resident((GATE_ROWS, D_MODEL), layer),
            _resident((2 * N_HEADS, LANES), layer),
            _resident((CONV_K, QK_W, LANES), layer),
            _resident((QK_W, LANES), layer),
            _resident((1, GROUP_W), layer),
            _resident((len(POOL_WINDOWS), LANES, LANES), layer),
            _resident((1, GROUP_W), layer),
            _resident((D_MODEL // GROUP_W, 3 * GROUP_W, GROUP_W), layer),
            pl.BlockSpec((1, D_MODEL), lambda k: (0, 0), pipeline_mode=pl.Buffered(1)),
        ],
        out_specs=pl.BlockSpec((None, 2 * T, D_MODEL), lambda k: (k // pairs_per_seq, k % pairs_per_seq, 0)),
        out_shape=jax.ShapeDtypeStruct(x.shape, F32),
        scratch_shapes=[pltpu.VMEM((T, D_MODEL), BF16)] + set_scratch + set_scratch + [
            pltpu.VMEM((T, GROUP_W), F32),
            pltpu.VMEM((GROUP_W, T), F32),
            pltpu.VMEM((T, N_HEADS * 2 * HEAD_D), BF16),
            pltpu.VMEM((N_HEADS, HEAD_D, 2 * HEAD_D), F32),
            pltpu.VMEM((2 * N_HEADS, LANES), F32),
            pltpu.VMEM((T, 3 * GROUP_W), BF16),
        ],
        compiler_params=pltpu.CompilerParams(
            dimension_semantics=("arbitrary",), vmem_limit_bytes=VMEM_LIMIT_BYTES),
        name="trunk_layer_final" if final else "trunk_layer",
    )(x, x, kmt, vmx, ng, w_all, w_qk, w_g, b_g, conv_w, conv_b, mng, pool_w, pool_scale, w_out, fng)


def kernel(x, mem, norm_g, w_in, b_gates, conv_w, conv_b, mlstm_norm_g, pool_w, pool_scale,
           mem_norm_g, w_mem_kv, w_out, final_norm_g):
    depth = w_in.shape[0]
    w_all, w_qk, w_g, w_out_b = _prep_weights(w_in, w_out)
    kmt, vmx = _mem_kv(mem, mem_norm_g, w_mem_kv)
    b_g = jnp.broadcast_to(b_gates[:, :, None], (depth, 2 * N_HEADS, LANES))
    cw_t = jnp.broadcast_to(conv_w[:, :, :, None], (depth, CONV_K, QK_W, LANES))
    cb_t = jnp.broadcast_to(conv_b[:, :, None], (depth, QK_W, LANES))
    args = (kmt, vmx, norm_g.reshape(depth, 1, D_MODEL), w_all, w_qk, w_g, b_g,
            cw_t, cb_t, mlstm_norm_g.reshape(depth, 1, GROUP_W),
            pool_w, pool_scale.reshape(depth, 1, GROUP_W), w_out_b, final_norm_g.reshape(1, D_MODEL))
    for l in range(depth):
        x = _layer(l, l == depth - 1, x, *args)
    return x
```

```python
import functools

import jax
import jax.numpy as jnp
from jax import lax
from jax.experimental import pallas as pl
from jax.experimental.pallas import tpu as pltpu

D_MODEL = 1024
GROUP_W = 512
N_HEADS = 4
HEAD_D = 128
CONV_K = 4
POOL_WINDOWS = (2, 4, 8, 16)
N_MEM = 256
EPS = 1e-6

SEQ_TILE = 256
CHUNK = 256
CONV_HIST = 8
POOL_HIST = 16
LANES = 128
VMEM_LIMIT_BYTES = 52 * 1024 * 1024

N_IN = 2 * GROUP_W + 3 * GROUP_W + 2 * N_HEADS + 4 * GROUP_W
GATE0 = 5 * GROUP_W
Q0, K0, V0, O0, ZA0, U0, ZB0, QC0, ZC0 = 0, 512, 1024, 1536, 2048, 2560, 3072, 3584, 4096
N_PROJ = 4608
GATE_ROWS = 16
P_V, P_O, P_ZA, P_ZB, P_QC, P_ZC = 0, 512, 1024, 1536, 2048, 2560
P_COLS = 3072

BF16 = jnp.bfloat16
F32 = jnp.float32


def _sigmoid(x):
    return 0.5 * jnp.tanh(0.5 * x) + 0.5


def _silu(x):
    hx = 0.5 * x
    return hx + hx * jnp.tanh(hx)


def _log_sigmoid(x):
    return jnp.minimum(x, 0.0) - jnp.log1p(jnp.exp(-jnp.abs(x)))


def _scan_lanes(x, op, identity):
    n = x.shape[1]
    lanes = lax.broadcasted_iota(jnp.int32, x.shape, 1)
    k = 1
    while k < n:
        x = op(x, jnp.where(lanes >= k, pltpu.roll(x, k, axis=1), identity))
        k *= 2
    return x


def _rmsnorm(x, g):
    ms = jnp.mean(x * x, axis=-1, keepdims=True)
    return (x * lax.rsqrt(ms + EPS)) * g


def _stage_a(x_ref, x_row, first, prev, cur, ng_ref, w_ref, wg_ref, h_ref):
    T = SEQ_TILE
    qk_prev, u_prev, _, _, back_prev = prev
    qk_cur, u_cur, proj_cur, g_cur, back_cur = cur

    def head():
        for s, (b_cur, b_prev) in enumerate(zip(back_cur, back_prev), start=1):
            b_cur[CONV_HIST:CONV_HIST + s, :] = jnp.where(first, 0.0, b_prev[CONV_HIST + T:CONV_HIST + T + s, :])
        u_cur[0:POOL_HIST, :] = jnp.where(first, 0.0, u_prev[T:T + POOL_HIST, :])
        h_ref[...] = _rmsnorm(x_ref[x_row:x_row + T, :], ng_ref[...]).astype(BF16)

    def proj(c0):
        return jnp.dot(h_ref[...], w_ref[:, c0:c0 + GROUP_W], preferred_element_type=F32)

    def dot_qk(c0):
        def run():
            res = proj(c0)
            qk_cur[CONV_HIST:CONV_HIST + T, c0:c0 + GROUP_W] = res
            for s, b_cur in enumerate(back_cur, start=1):
                b_cur[pl.ds(CONV_HIST + s, T), c0:c0 + GROUP_W] = res
        return run

    def dot_u():
        u_cur[POOL_HIST:POOL_HIST + T, :] = proj(U0)

    def dot_to(src, dst):
        def run():
            proj_cur[:, dst:dst + GROUP_W] = proj(src)
        return run

    def dot_gates():
        g_cur[...] = lax.dot_general(wg_ref[...], h_ref[...], (((1,), (1,)), ((), ())),
                                     preferred_element_type=F32)

    return [head, dot_qk(Q0), dot_qk(K0), dot_u, dot_to(V0, P_V), dot_to(O0, P_O), dot_to(ZA0, P_ZA),
            dot_to(ZB0, P_ZB), dot_to(QC0, P_QC), dot_to(ZC0, P_ZC), dot_gates]


def _stage_b(x_ref, x_row, first, t_seq, cur, o_ref, kmt_ref, vmx_ref, bg_ref, cw_ref, cb_ref, mng_ref,
             pw_ref, ps_ref, wo_ref, fng_ref, q_ref, k_ref, vext_ref, cn_ref, m_ref, y_ref, final):
    T, L = SEQ_TILE, CHUNK
    qk_ext, u_ext, proj_ref, g_ref, back = cur

    def conv(c0):
        rows, cols = slice(CONV_HIST, CONV_HIST + T), slice(c0, c0 + GROUP_W)
        acc = cb_ref[:, cols] + cw_ref[CONV_K - 1:CONV_K, cols] * qk_ext[rows, cols]
        for j in range(CONV_K - 1):
            acc = acc + cw_ref[j:j + 1, cols] * back[CONV_K - 2 - j][rows, cols]
        return _silu(acc)

    def conv_q():
        q_ref[...] = conv(0)

    def conv_k():
        k_ref[...] = conv(GROUP_W) * (HEAD_D ** -0.5)
        for hd in range(N_HEADS):
            vext_ref[:, hd * 2 * HEAD_D:hd * 2 * HEAD_D + HEAD_D] = (
                proj_ref[:, P_V + hd * HEAD_D:P_V + (hd + 1) * HEAD_D].astype(BF16))

    def mlstm_chunk(c):
        def run():
            rowi = lax.broadcasted_iota(jnp.int32, (L, L), 0)
            coli = lax.broadcasted_iota(jnp.int32, (L, L), 1)
            causal = coli <= rowi
            r0 = c * L
            g8 = g_ref[0:2 * N_HEADS, r0:r0 + L] + bg_ref[:, 0:1]
            gi = g8
            logf = _log_sigmoid(pltpu.roll(g8, N_HEADS, axis=0))
            b = _scan_lanes(logf, jnp.add, 0.0)
            r = gi - b
            m_prev = m_ref[:, 0:1]
            if c == 0:
                m_prev = jnp.where(first, 0.0, m_prev)
            mm = jnp.maximum(_scan_lanes(r, jnp.maximum, -jnp.inf), m_prev)
            a = jnp.exp(m_prev - mm)
            e = jnp.exp(-(mm + b))
            m_ref[...] = jnp.broadcast_to((mm + b)[:, L - 1:L], m_ref.shape)
            cols = jnp.concatenate([mm, a, e, jnp.zeros((LANES - 6 * N_HEADS, L), F32)], axis=0).T
            mm_c, a_c, e_c = (cols[:, i * 2 * N_HEADS:(i + 1) * 2 * N_HEADS] for i in range(3))
            heads = [slice(hd * HEAD_D, (hd + 1) * HEAD_D) for hd in range(N_HEADS)]
            qs = [q_ref[r0:r0 + L, hs] for hs in heads]
            vxs = [vext_ref[r0:r0 + L, hd * 2 * HEAD_D:(hd + 1) * 2 * HEAD_D] for hd in range(N_HEADS)]
            s_bf, k_tw = [], []
            for hd, hs in enumerate(heads):
                arg = r[hd:hd + 1, :] - mm_c[:, hd:hd + 1]
                p = jnp.exp(jnp.where(causal, arg, -1e30))
                k_t = k_ref[r0:r0 + L, hs].T
                s = jnp.dot(qs[hd].astype(BF16), k_t.astype(BF16), preferred_element_type=F32) * p
                s_bf.append(s.astype(BF16))
                k_tw.append((k_t * p[L - 1:L, :]).astype(BF16))
            nds = []
            for hd in range(N_HEADS):
                cn = cn_ref[hd]
                if c == 0:
                    cn = jnp.where(first, 0.0, cn)
                lhs = jnp.concatenate([(a_c[:, hd:hd + 1] * qs[hd]).astype(BF16), s_bf[hd]], axis=1)
                rhs = jnp.concatenate([cn.astype(BF16), vxs[hd]], axis=0)
                nds.append(jnp.dot(lhs, rhs, preferred_element_type=F32))
                cn_ref[hd] = (a[hd:hd + 1, L - 1:L] * cn
                              + jnp.dot(k_tw[hd], vxs[hd], preferred_element_type=F32))
            for hd, hs in enumerate(heads):
                nd = nds[hd]
                hh = nd[:, 0:HEAD_D] / jnp.maximum(jnp.abs(nd[:, HEAD_D:HEAD_D + 1]), e_c[:, hd:hd + 1])
                hg = _sigmoid(proj_ref[r0:r0 + L, P_O + hd * HEAD_D:P_O + (hd + 1) * HEAD_D]) * hh
                hg = hg * lax.rsqrt(jnp.mean(hg * hg, axis=-1, keepdims=True) + EPS)
                hg = hg * mng_ref[:, hs]
                ya = hg * _silu(proj_ref[r0:r0 + L, P_ZA + hd * HEAD_D:P_ZA + (hd + 1) * HEAD_D])
                y_ref[r0:r0 + L, hs] = ya.astype(BF16)
        return run

    def pool():
        tg = (t_seq * T + lax.broadcasted_iota(jnp.int32, (T, LANES), 0) + 1).astype(F32)
        zero_w = jnp.zeros((LANES, LANES), BF16)
        for g0 in range(0, len(POOL_WINDOWS), 2):
            pooled = []
            for g in (g0, g0 + 1):
                win = POOL_WINDOWS[g]
                gs = slice(g * LANES, (g + 1) * LANES)
                ssum = u_ext[:, gs]
                sh = 1
                while sh < win:
                    ssum = ssum + pltpu.roll(ssum, sh, axis=0)
                    sh *= 2
                u = u_ext[POOL_HIST:POOL_HIST + T, gs]
                pooled.append((ssum[POOL_HIST:POOL_HIST + T] / jnp.minimum(tg, float(win)) - u).astype(BF16))
            w_a, w_b = pw_ref[g0].astype(BF16), pw_ref[g0 + 1].astype(BF16)
            w_pair = jnp.concatenate([jnp.concatenate([w_a, zero_w], axis=1),
                                      jnp.concatenate([zero_w, w_b], axis=1)], axis=0)
            yb = jnp.dot(jnp.concatenate(pooled, axis=1), w_pair, preferred_element_type=F32)
            yb = (yb * ps_ref[:, g0 * LANES:(g0 + 2) * LANES]
                  * _silu(proj_ref[:, P_ZB + g0 * LANES:P_ZB + (g0 + 2) * LANES]))
            y_ref[:, GROUP_W + g0 * LANES:GROUP_W + (g0 + 2) * LANES] = yb.astype(BF16)

    def attention():
        for hd in range(N_HEADS):
            hs = slice(hd * HEAD_D, (hd + 1) * HEAD_D)
            qc = proj_ref[:, P_QC + hd * HEAD_D:P_QC + (hd + 1) * HEAD_D].astype(BF16)
            s = jnp.dot(qc, kmt_ref[hs, :], preferred_element_type=F32) * (HEAD_D ** -0.5)
            pexp = jnp.exp(s - jnp.max(s, axis=-1, keepdims=True))
            pv = jnp.dot(pexp.astype(BF16), vmx_ref[hd], preferred_element_type=F32)
            yc = pv[:, 0:HEAD_D] / pv[:, HEAD_D:HEAD_D + 1]
            yc = yc * _silu(proj_ref[:, P_ZC + hd * HEAD_D:P_ZC + (hd + 1) * HEAD_D])
            y_ref[:, 2 * GROUP_W + hd * HEAD_D:2 * GROUP_W + (hd + 1) * HEAD_D] = yc.astype(BF16)

    def out_proj():
        rows = slice(x_row, x_row + T)
        for i in range(D_MODEL // GROUP_W):
            cs = slice(i * GROUP_W, (i + 1) * GROUP_W)
            o_ref[rows, cs] = x_ref[rows, cs] + jnp.dot(y_ref[...], wo_ref[i], preferred_element_type=F32)
        if final:
            o_ref[rows, :] = _rmsnorm(o_ref[rows, :], fng_ref[...])

    return dict(conv_q=conv_q, conv_k=conv_k, chunks=[mlstm_chunk(c) for c in range(T // L)],
                pool=pool, attention=attention, out_proj=out_proj)


def _layer_kernel(xp_ref, xn_ref, kmt_ref, vmx_ref, ng_ref, w_ref, wg_ref, bg_ref, cw_ref, cb_ref, mng_ref,
                  pw_ref, ps_ref, wo_ref, fng_ref, o_ref,
                  h_ref, qk0, u0, proj0, g0, b01, b02, b03, qk1, u1, proj1, g1, b11, b12, b13,
                  q_ref, k_ref, vext_ref, cn_ref, m_ref, y_ref, *, final, tiles_per_seq):
    T = SEQ_TILE
    k = pl.program_id(0)
    set0, set1 = (qk0, u0, proj0, g0, (b01, b02, b03)), (qk1, u1, proj1, g1, (b11, b12, b13))
    a_args = (ng_ref, w_ref, wg_ref, h_ref)
    b_args = (kmt_ref, vmx_ref, bg_ref, cw_ref, cb_ref, mng_ref, pw_ref, ps_ref, wo_ref, fng_ref,
              q_ref, k_ref, vext_ref, cn_ref, m_ref, y_ref, final)

    @pl.when(k == 0)
    def _prologue():
        col = lax.broadcasted_iota(jnp.int32, vext_ref.shape, 1)
        vext_ref[...] = jnp.where(col % (2 * HEAD_D) == HEAD_D, 1.0, 0.0).astype(BF16)
        cn_ref[...] = jnp.zeros(cn_ref.shape, F32)
        m_ref[...] = jnp.zeros(m_ref.shape, F32)
        for b_prev in (b11, b12, b13):
            b_prev[CONV_HIST + T:2 * CONV_HIST + T, :] = jnp.zeros((CONV_HIST, 2 * GROUP_W), F32)
        u1[T:T + POOL_HIST, :] = jnp.zeros((POOL_HIST, GROUP_W), F32)
        for piece in _stage_a(xp_ref, 0, True, set1, set0, *a_args):
            piece()

    t_even = (2 * k) % tiles_per_seq
    first_even = t_even == 0
    first_next = (2 * k + 2) % tiles_per_seq == 0

    def half_step(x_row, first, t_seq, cur, nxt, xa_ref, xa_row, first_a):
        for piece in _stage_a(xa_ref, xa_row, first_a, cur, nxt, *a_args):
            piece()
        b = _stage_b(xp_ref, x_row, first, t_seq, cur, o_ref, *b_args)
        for phase in (b["conv_q"], b["conv_k"], *b["chunks"], b["pool"], b["attention"], b["out_proj"]):
            phase()

    half_step(0, first_even, t_even, set0, set1, xp_ref, T, False)
    half_step(T, False, t_even + 1, set1, set0, xn_ref, 0, first_next)


def _mem_kv_kernel(mem_ref, g_ref, w_ref, kmt_ref, vmx_ref, wb_ref):
    @pl.when(pl.program_id(0) == 0)
    def _cast_weights():
        wb_ref[...] = w_ref[...].astype(BF16)

    mem = mem_ref[...]
    ms = jnp.mean(mem * mem, axis=-1, keepdims=True)
    nrm = mem * lax.rsqrt(ms + EPS)
    col = lax.broadcasted_iota(jnp.int32, (N_MEM, HEAD_D), 1)
    ones_col = jnp.where(col == 0, 1.0, 0.0).astype(BF16)
    for l in range(wb_ref.shape[0]):
        mn = (nrm * g_ref[l]).astype(BF16)
        kv = jnp.dot(mn, wb_ref[l], preferred_element_type=F32)
        for hd in range(N_HEADS):
            hs = slice(hd * HEAD_D, (hd + 1) * HEAD_D)
            kmt_ref[l, hs, :] = kv[:, hs].T.astype(BF16)
            vmx_ref[l, hd, :, 0:HEAD_D] = kv[:, GROUP_W + hd * HEAD_D:GROUP_W + (hd + 1) * HEAD_D].astype(BF16)
            vmx_ref[l, hd, :, HEAD_D:2 * HEAD_D] = ones_col


def _prep_kernel(wint_ref, wout_ref, wall_ref, wg_ref, wo_ref):
    wall_ref[:, 0:GATE0] = wint_ref[0:GATE0, :].T.astype(BF16)
    wall_ref[:, GATE0:N_PROJ] = wint_ref[GATE0 + 2 * N_HEADS:N_IN, :].T.astype(BF16)
    wg_ref[0:2 * N_HEADS, :] = wint_ref[GATE0:GATE0 + 2 * N_HEADS, :].astype(BF16)
    wg_ref[2 * N_HEADS:GATE_ROWS, :] = jnp.zeros((GATE_ROWS - 2 * N_HEADS, wg_ref.shape[1]), BF16)
    for i in range(D_MODEL // GROUP_W):
        wo_ref[i] = wout_ref[:, i * GROUP_W:(i + 1) * GROUP_W].astype(BF16)


def _resident(shape, layer):
    nd = len(shape)
    return pl.BlockSpec((None,) + shape, lambda k: (layer,) + (0,) * nd, pipeline_mode=pl.Buffered(1))


def _prep_weights(w_in, w_out):
    depth = w_in.shape[0]
    steps = 4
    rb_in, rb_out = D_MODEL // steps, 3 * GROUP_W // steps
    return pl.pallas_call(
        _prep_kernel,
        grid=(depth, steps),
        in_specs=[
            pl.BlockSpec((None, N_IN, rb_in), lambda l, i: (l, 0, i)),
            pl.BlockSpec((None, rb_out, D_MODEL), lambda l, i: (l, i, 0)),
        ],
        out_specs=[
            pl.BlockSpec((None, rb_in, N_PROJ), lambda l, i: (l, i, 0)),
            pl.BlockSpec((None, GATE_ROWS, rb_in), lambda l, i: (l, 0, i)),
            pl.BlockSpec((None, D_MODEL // GROUP_W, rb_out, GROUP_W), lambda l, i: (l, 0, i, 0)),
        ],
        out_shape=[
            jax.ShapeDtypeStruct((depth, D_MODEL, N_PROJ), BF16),
            jax.ShapeDtypeStruct((depth, GATE_ROWS, D_MODEL), BF16),
            jax.ShapeDtypeStruct((depth, D_MODEL // GROUP_W, 3 * GROUP_W, GROUP_W), BF16),
        ],
        compiler_params=pltpu.CompilerParams(
            dimension_semantics=("arbitrary", "arbitrary"), vmem_limit_bytes=VMEM_LIMIT_BYTES),
        name="prep_weights",
    )(jnp.swapaxes(w_in, 1, 2), w_out)


def _mem_kv(mem, mem_norm_g, w_kv):
    depth, batch = w_kv.shape[0], mem.shape[0]
    return pl.pallas_call(
        _mem_kv_kernel,
        grid=(batch,),
        in_specs=[
            pl.BlockSpec((None, N_MEM, D_MODEL), lambda b: (b, 0, 0)),
            pl.BlockSpec((depth, 1, D_MODEL), lambda b: (0, 0, 0)),
            pl.BlockSpec((depth, D_MODEL, 2 * GROUP_W), lambda b: (0, 0, 0), pipeline_mode=pl.Buffered(1)),
        ],
        out_specs=[
            pl.BlockSpec((depth, None, GROUP_W, N_MEM), lambda b: (0, b, 0, 0)),
            pl.BlockSpec((depth, None, N_HEADS, N_MEM, 2 * HEAD_D), lambda b: (0, b, 0, 0, 0)),
        ],
        out_shape=[
            jax.ShapeDtypeStruct((depth, batch, GROUP_W, N_MEM), BF16),
            jax.ShapeDtypeStruct((depth, batch, N_HEADS, N_MEM, 2 * HEAD_D), BF16),
        ],
        scratch_shapes=[pltpu.VMEM((depth, D_MODEL, 2 * GROUP_W), BF16)],
        compiler_params=pltpu.CompilerParams(
            dimension_semantics=("arbitrary",), vmem_limit_bytes=VMEM_LIMIT_BYTES),
        name="mem_kv",
    )(mem, mem_norm_g.reshape(depth, 1, D_MODEL), w_kv)


def _layer(layer, final, x, kmt, vmx, ng, w_all, w_g, b_g, conv_w, conv_b, mng, pool_w, pool_scale, w_out, fng):
    batch, seq, _ = x.shape
    T = SEQ_TILE
    assert seq % (2 * T) == 0 and T % CHUNK == 0
    tiles_per_seq = seq // T
    pairs_per_seq = tiles_per_seq // 2
    last_tile = batch * tiles_per_seq - 1

    def next_tile(k):
        nt = jnp.minimum(2 * k + 2, last_tile)
        return nt // tiles_per_seq, nt % tiles_per_seq, 0

    kern = functools.partial(_layer_kernel, final=final, tiles_per_seq=tiles_per_seq)
    set_scratch = [
        pltpu.VMEM((CONV_HIST + T, 2 * GROUP_W), F32),
        pltpu.VMEM((POOL_HIST + T, GROUP_W), F32),
        pltpu.VMEM((T, P_COLS), F32),
        pltpu.VMEM((GATE_ROWS, T), F32),
    ] + [pltpu.VMEM((2 * CONV_HIST + T, 2 * GROUP_W), F32)] * (CONV_K - 1)
    return pl.pallas_call(
        kern,
        grid=(batch * pairs_per_seq,),
        in_specs=[
            pl.BlockSpec((None, 2 * T, D_MODEL), lambda k: (k // pairs_per_seq, k % pairs_per_seq, 0)),
            pl.BlockSpec((None, T, D_MODEL), next_tile),
            pl.BlockSpec((None, None, GROUP_W, N_MEM), lambda k: (layer, k // pairs_per_seq, 0, 0)),
            pl.BlockSpec((None, None, N_HEADS, N_MEM, 2 * HEAD_D),
                         lambda k: (layer, k // pairs_per_seq, 0, 0, 0)),
            _resident((1, D_MODEL), layer),
            _resident((D_MODEL, N_PROJ), layer),
            _resident((GATE_ROWS, D_MODEL), layer),
            _resident((2 * N_HEADS, LANES), layer),
            _resident((CONV_K, 2 * GROUP_W), layer),
            _resident((1, 2 * GROUP_W), layer),
            _resident((1, GROUP_W), layer),
            _resident((len(POOL_WINDOWS), LANES, LANES), layer),
            _resident((1, GROUP_W), layer),
            _resident((D_MODEL // GROUP_W, 3 * GROUP_W, GROUP_W), layer),
            pl.BlockSpec((1, D_MODEL), lambda k: (0, 0), pipeline_mode=pl.Buffered(1)),
        ],
        out_specs=pl.BlockSpec((None, 2 * T, D_MODEL), lambda k: (k // pairs_per_seq, k % pairs_per_seq, 0)),
        out_shape=jax.ShapeDtypeStruct(x.shape, F32),
        scratch_shapes=[pltpu.VMEM((T, D_MODEL), BF16)] + set_scratch + set_scratch + [
            pltpu.VMEM((T, GROUP_W), F32),
            pltpu.VMEM((T, GROUP_W), F32),
            pltpu.VMEM((T, N_HEADS * 2 * HEAD_D), BF16),
            pltpu.VMEM((N_HEADS, HEAD_D, 2 * HEAD_D), F32),
            pltpu.VMEM((2 * N_HEADS, LANES), F32),
            pltpu.VMEM((T, 3 * GROUP_W), BF16),
        ],
        compiler_params=pltpu.CompilerParams(
            dimension_semantics=("arbitrary",), vmem_limit_bytes=VMEM_LIMIT_BYTES),
        name="trunk_layer_final" if final else "trunk_layer",
    )(x, x, kmt, vmx, ng, w_all, w_g, b_g, conv_w, conv_b, mng, pool_w, pool_scale, w_out, fng)


def kernel(x, mem, norm_g, w_in, b_gates, conv_w, conv_b, mlstm_norm_g, pool_w, pool_scale,
           mem_norm_g, w_mem_kv, w_out, final_norm_g):
    depth = w_in.shape[0]
    w_all, w_g, w_out_b = _prep_weights(w_in, w_out)
    kmt, vmx = _mem_kv(mem, mem_norm_g, w_mem_kv)
    b_g = jnp.broadcast_to(b_gates[:, :, None], (depth, 2 * N_HEADS, LANES))
    args = (kmt, vmx, norm_g.reshape(depth, 1, D_MODEL), w_all, w_g, b_g,
            conv_w, conv_b.reshape(depth, 1, 2 * GROUP_W), mlstm_norm_g.reshape(depth, 1, GROUP_W),
            pool_w, pool_scale.reshape(depth, 1, GROUP_W), w_out_b, final_norm_g.reshape(1, D_MODEL))
    for l in range(depth):
        x = _layer(l, l == depth - 1, x, *args)
    return x
```

```python
import functools

import jax
import jax.numpy as jnp
from jax import lax
from jax.experimental import pallas as pl
from jax.experimental.pallas import tpu as pltpu

D_MODEL = 1024
GROUP_W = 512
N_HEADS = 4
HEAD_D = 128
CONV_K = 4
POOL_WINDOWS = (2, 4, 8, 16)
N_MEM = 256
EPS = 1e-6

SEQ_TILE = 256
CHUNK = 256
CONV_HIST = 8
POOL_HIST = 16
LANES = 128
VMEM_LIMIT_BYTES = 52 * 1024 * 1024

N_IN = 2 * GROUP_W + 3 * GROUP_W + 2 * N_HEADS + 4 * GROUP_W
GATE0 = 5 * GROUP_W
Q0, K0, V0, O0, ZA0, U0, ZB0, QC0, ZC0 = 0, 512, 1024, 1536, 2048, 2560, 3072, 3584, 4096
N_PROJ = 4608
GATE_ROWS = 16
P_V, P_O, P_QC = 0, 512, 1024
P_COLS = 1536
Z_A, Z_B, Z_C = 0, 512, 1024
Z_COLS = 1536

BF16 = jnp.bfloat16
F32 = jnp.float32


def _sigmoid(x):
    return 0.5 * jnp.tanh(0.5 * x) + 0.5


def _silu(x):
    hx = 0.5 * x
    return hx + hx * jnp.tanh(hx)


def _log_sigmoid(x):
    return jnp.minimum(x, 0.0) - jnp.log1p(jnp.exp(-jnp.abs(x)))


def _scan_lanes(x, op, identity):
    n = x.shape[1]
    lanes = lax.broadcasted_iota(jnp.int32, x.shape, 1)
    k = 1
    while k < n:
        x = op(x, jnp.where(lanes >= k, pltpu.roll(x, k, axis=1), identity))
        k *= 2
    return x


def _rmsnorm(x, g):
    ms = jnp.mean(x * x, axis=-1, keepdims=True)
    return (x * lax.rsqrt(ms + EPS)) * g


def _stage_a(x_ref, x_row, first, prev, cur, ng_ref, w_ref, wg_ref, h_ref):
    T = SEQ_TILE
    qk_prev, u_prev, _, _, _ = prev
    qk_cur, u_cur, proj_cur, g_cur, z_cur = cur

    def head():
        qk_cur[0:CONV_HIST, :] = jnp.where(first, 0.0, qk_prev[T:T + CONV_HIST, :])
        u_cur[0:POOL_HIST, :] = jnp.where(first, 0.0, u_prev[T:T + POOL_HIST, :])
        h_ref[...] = _rmsnorm(x_ref[x_row:x_row + T, :], ng_ref[...]).astype(BF16)

    def proj(c0):
        return jnp.dot(h_ref[...], w_ref[:, c0:c0 + GROUP_W], preferred_element_type=F32)

    def dot_q():
        qk_cur[CONV_HIST:CONV_HIST + T, 0:GROUP_W] = proj(Q0)

    def dot_k():
        qk_cur[CONV_HIST:CONV_HIST + T, GROUP_W:2 * GROUP_W] = proj(K0)

    def dot_u():
        u_cur[POOL_HIST:POOL_HIST + T, :] = proj(U0)

    def dot_to(src, dst):
        def run():
            proj_cur[:, dst:dst + GROUP_W] = proj(src)
        return run

    def dot_gate(src, dst):
        def run():
            z_cur[:, dst:dst + GROUP_W] = proj(src).astype(BF16)
        return run

    def dot_gates():
        g_cur[...] = lax.dot_general(wg_ref[...], h_ref[...], (((1,), (1,)), ((), ())),
                                     preferred_element_type=F32)

    return [head, dot_q, dot_k, dot_u, dot_to(V0, P_V), dot_to(O0, P_O), dot_gate(ZA0, Z_A),
            dot_gate(ZB0, Z_B), dot_to(QC0, P_QC), dot_gate(ZC0, Z_C), dot_gates]


def _stage_b(x_ref, x_row, first, t_seq, cur, o_ref, kmt_ref, vmx_ref, bg_ref, cw_ref, cb_ref, mng_ref,
             pw_ref, ps_ref, wo_ref, fng_ref, q_ref, k_ref, vext_ref, cn_ref, m_ref, y_ref, final):
    T, L = SEQ_TILE, CHUNK
    qk_ext, u_ext, proj_ref, g_ref, z_ref = cur

    def conv(c0):
        xs = qk_ext[:, c0:c0 + GROUP_W]
        acc = cb_ref[:, c0:c0 + GROUP_W] + cw_ref[CONV_K - 1:CONV_K, c0:c0 + GROUP_W] * xs[CONV_HIST:]
        for j in range(CONV_K - 1):
            back = pltpu.roll(xs, CONV_K - 1 - j, axis=0)[CONV_HIST:]
            acc = acc + cw_ref[j:j + 1, c0:c0 + GROUP_W] * back
        return _silu(acc)

    def conv_q():
        q_ref[...] = conv(0)

    def conv_k():
        k_ref[...] = conv(GROUP_W) * (HEAD_D ** -0.5)
        for hd in range(N_HEADS):
            vext_ref[:, hd * 2 * HEAD_D:hd * 2 * HEAD_D + HEAD_D] = (
                proj_ref[:, P_V + hd * HEAD_D:P_V + (hd + 1) * HEAD_D].astype(BF16))

    def mlstm_chunk(c):
        def run():
            rowi = lax.broadcasted_iota(jnp.int32, (L, L), 0)
            coli = lax.broadcasted_iota(jnp.int32, (L, L), 1)
            causal = coli <= rowi
            r0 = c * L
            g8 = g_ref[0:2 * N_HEADS, r0:r0 + L] + bg_ref[:, 0:1]
            gi = g8
            logf = _log_sigmoid(pltpu.roll(g8, N_HEADS, axis=0))
            b = _scan_lanes(logf, jnp.add, 0.0)
            r = gi - b
            m_prev = m_ref[:, 0:1]
            if c == 0:
                m_prev = jnp.where(first, 0.0, m_prev)
            mm = jnp.maximum(_scan_lanes(r, jnp.maximum, -jnp.inf), m_prev)
            a = jnp.exp(m_prev - mm)
            e = jnp.exp(-(mm + b))
            m_ref[...] = jnp.broadcast_to((mm + b)[:, L - 1:L], m_ref.shape)
            cols = jnp.concatenate([mm, a, e, jnp.zeros((LANES - 6 * N_HEADS, L), F32)], axis=0).T
            mm_c, a_c, e_c = (cols[:, i * 2 * N_HEADS:(i + 1) * 2 * N_HEADS] for i in range(3))
            heads = [slice(hd * HEAD_D, (hd + 1) * HEAD_D) for hd in range(N_HEADS)]
            qs = [q_ref[r0:r0 + L, hs] for hs in heads]
            vxs = [vext_ref[r0:r0 + L, hd * 2 * HEAD_D:(hd + 1) * 2 * HEAD_D] for hd in range(N_HEADS)]
            s_bf, k_tw = [], []
            for hd, hs in enumerate(heads):
                arg = r[hd:hd + 1, :] - mm_c[:, hd:hd + 1]
                p = jnp.exp(jnp.where(causal, arg, -1e30))
                k_t = k_ref[r0:r0 + L, hs].T
                s = jnp.dot(qs[hd].astype(BF16), k_t.astype(BF16), preferred_element_type=F32) * p
                s_bf.append(s.astype(BF16))
                k_tw.append((k_t * p[L - 1:L, :]).astype(BF16))
            nds = []
            for hd in range(N_HEADS):
                cn = cn_ref[hd]
                if c == 0:
                    cn = jnp.where(first, 0.0, cn)
                lhs = jnp.concatenate([(a_c[:, hd:hd + 1] * qs[hd]).astype(BF16), s_bf[hd]], axis=1)
                rhs = jnp.concatenate([cn.astype(BF16), vxs[hd]], axis=0)
                nds.append(jnp.dot(lhs, rhs, preferred_element_type=F32))
                cn_ref[hd] = (a[hd:hd + 1, L - 1:L] * cn
                              + jnp.dot(k_tw[hd], vxs[hd], preferred_element_type=F32))
            for hd, hs in enumerate(heads):
                nd = nds[hd]
                hh = nd[:, 0:HEAD_D] / jnp.maximum(jnp.abs(nd[:, HEAD_D:HEAD_D + 1]), e_c[:, hd:hd + 1])
                hg = _sigmoid(proj_ref[r0:r0 + L, P_O + hd * HEAD_D:P_O + (hd + 1) * HEAD_D]) * hh
                hg = hg * lax.rsqrt(jnp.mean(hg * hg, axis=-1, keepdims=True) + EPS)
                hg = hg * mng_ref[:, hs]
                y_ref[r0:r0 + L, hs] = hg.astype(BF16) * _silu(z_ref[r0:r0 + L, Z_A + hd * HEAD_D:Z_A + (hd + 1) * HEAD_D])
        return run

    def pool():
        tg = (t_seq * T + lax.broadcasted_iota(jnp.int32, (T, LANES), 0) + 1).astype(F32)
        zero_w = jnp.zeros((LANES, LANES), BF16)
        for g0 in range(0, len(POOL_WINDOWS), 2):
            pooled = []
            for g in (g0, g0 + 1):
                win = POOL_WINDOWS[g]
                gs = slice(g * LANES, (g + 1) * LANES)
                ssum = u_ext[:, gs]
                sh = 1
                while sh < win:
                    ssum = ssum + pltpu.roll(ssum, sh, axis=0)
                    sh *= 2
                u = u_ext[POOL_HIST:POOL_HIST + T, gs]
                pooled.append((ssum[POOL_HIST:POOL_HIST + T] / jnp.minimum(tg, float(win)) - u).astype(BF16))
            w_a, w_b = pw_ref[g0].astype(BF16), pw_ref[g0 + 1].astype(BF16)
            w_pair = jnp.concatenate([jnp.concatenate([w_a, zero_w], axis=1),
                                      jnp.concatenate([zero_w, w_b], axis=1)], axis=0)
            yb = jnp.dot(jnp.concatenate(pooled, axis=1), w_pair, preferred_element_type=F32)
            yb = (yb * ps_ref[:, g0 * LANES:(g0 + 2) * LANES]).astype(BF16)
            y_ref[:, GROUP_W + g0 * LANES:GROUP_W + (g0 + 2) * LANES] = (
                yb * _silu(z_ref[:, Z_B + g0 * LANES:Z_B + (g0 + 2) * LANES]))

    def attention():
        for hd in range(N_HEADS):
            hs = slice(hd * HEAD_D, (hd + 1) * HEAD_D)
            qc = proj_ref[:, P_QC + hd * HEAD_D:P_QC + (hd + 1) * HEAD_D].astype(BF16)
            s = jnp.dot(qc, kmt_ref[hs, :], preferred_element_type=F32) * (HEAD_D ** -0.5)
            pexp = jnp.exp(s - jnp.max(s, axis=-1, keepdims=True))
            pv = jnp.dot(pexp.astype(BF16), vmx_ref[hd], preferred_element_type=F32)
            yc = (pv[:, 0:HEAD_D] / pv[:, HEAD_D:HEAD_D + 1]).astype(BF16)
            y_ref[:, 2 * GROUP_W + hd * HEAD_D:2 * GROUP_W + (hd + 1) * HEAD_D] = (
                yc * _silu(z_ref[:, Z_C + hd * HEAD_D:Z_C + (hd + 1) * HEAD_D]))

    def out_proj():
        rows = slice(x_row, x_row + T)
        for i in range(D_MODEL // GROUP_W):
            cs = slice(i * GROUP_W, (i + 1) * GROUP_W)
            o_ref[rows, cs] = x_ref[rows, cs] + jnp.dot(y_ref[...], wo_ref[i], preferred_element_type=F32)
        if final:
            o_ref[rows, :] = _rmsnorm(o_ref[rows, :], fng_ref[...])

    return dict(conv_q=conv_q, conv_k=conv_k, chunks=[mlstm_chunk(c) for c in range(T // L)],
                pool=pool, attention=attention, out_proj=out_proj)


def _layer_kernel(xp_ref, xn_ref, kmt_ref, vmx_ref, ng_ref, w_ref, wg_ref, bg_ref, cw_ref, cb_ref, mng_ref,
                  pw_ref, ps_ref, wo_ref, fng_ref, o_ref,
                  h_ref, qk0, u0, proj0, g0, z0, qk1, u1, proj1, g1, z1, q_ref, k_ref, vext_ref, cn_ref, m_ref,
                  y_ref, *, final, tiles_per_seq):
    T = SEQ_TILE
    k = pl.program_id(0)
    set0, set1 = (qk0, u0, proj0, g0, z0), (qk1, u1, proj1, g1, z1)
    a_args = (ng_ref, w_ref, wg_ref, h_ref)
    b_args = (kmt_ref, vmx_ref, bg_ref, cw_ref, cb_ref, mng_ref, pw_ref, ps_ref, wo_ref, fng_ref,
              q_ref, k_ref, vext_ref, cn_ref, m_ref, y_ref, final)

    @pl.when(k == 0)
    def _prologue():
        col = lax.broadcasted_iota(jnp.int32, vext_ref.shape, 1)
        vext_ref[...] = jnp.where(col % (2 * HEAD_D) == HEAD_D, 1.0, 0.0).astype(BF16)
        cn_ref[...] = jnp.zeros(cn_ref.shape, F32)
        m_ref[...] = jnp.zeros(m_ref.shape, F32)
        qk1[T:T + CONV_HIST, :] = jnp.zeros((CONV_HIST, 2 * GROUP_W), F32)
        u1[T:T + POOL_HIST, :] = jnp.zeros((POOL_HIST, GROUP_W), F32)
        for piece in _stage_a(xp_ref, 0, True, set1, set0, *a_args):
            piece()

    t_even = (2 * k) % tiles_per_seq
    first_even = t_even == 0
    first_next = (2 * k + 2) % tiles_per_seq == 0

    def half_step(x_row, first, t_seq, cur, nxt, xa_ref, xa_row, first_a):
        for piece in _stage_a(xa_ref, xa_row, first_a, cur, nxt, *a_args):
            piece()
        b = _stage_b(xp_ref, x_row, first, t_seq, cur, o_ref, *b_args)
        for phase in (b["conv_q"], b["conv_k"], *b["chunks"], b["pool"], b["attention"], b["out_proj"]):
            phase()

    half_step(0, first_even, t_even, set0, set1, xp_ref, T, False)
    half_step(T, False, t_even + 1, set1, set0, xn_ref, 0, first_next)


def _mem_kv_kernel(mem_ref, g_ref, w_ref, kmt_ref, vmx_ref, wb_ref):
    @pl.when(pl.program_id(0) == 0)
    def _cast_weights():
        wb_ref[...] = w_ref[...].astype(BF16)

    mem = mem_ref[...]
    ms = jnp.mean(mem * mem, axis=-1, keepdims=True)
    nrm = mem * lax.rsqrt(ms + EPS)
    col = lax.broadcasted_iota(jnp.int32, (N_MEM, HEAD_D), 1)
    ones_col = jnp.where(col == 0, 1.0, 0.0).astype(BF16)
    for l in range(wb_ref.shape[0]):
        mn = (nrm * g_ref[l]).astype(BF16)
        kv = jnp.dot(mn, wb_ref[l], preferred_element_type=F32)
        for hd in range(N_HEADS):
            hs = slice(hd * HEAD_D, (hd + 1) * HEAD_D)
            kmt_ref[l, hs, :] = kv[:, hs].T.astype(BF16)
            vmx_ref[l, hd, :, 0:HEAD_D] = kv[:, GROUP_W + hd * HEAD_D:GROUP_W + (hd + 1) * HEAD_D].astype(BF16)
            vmx_ref[l, hd, :, HEAD_D:2 * HEAD_D] = ones_col


def _prep_kernel(wint_ref, wout_ref, wall_ref, wg_ref, wo_ref):
    wall_ref[:, 0:GATE0] = wint_ref[0:GATE0, :].T.astype(BF16)
    wall_ref[:, GATE0:N_PROJ] = wint_ref[GATE0 + 2 * N_HEADS:N_IN, :].T.astype(BF16)
    wg_ref[0:2 * N_HEADS, :] = wint_ref[GATE0:GATE0 + 2 * N_HEADS, :].astype(BF16)
    wg_ref[2 * N_HEADS:GATE_ROWS, :] = jnp.zeros((GATE_ROWS - 2 * N_HEADS, wg_ref.shape[1]), BF16)
    for i in range(D_MODEL // GROUP_W):
        wo_ref[i] = wout_ref[:, i * GROUP_W:(i + 1) * GROUP_W].astype(BF16)


def _resident(shape, layer):
    nd = len(shape)
    return pl.BlockSpec((None,) + shape, lambda k: (layer,) + (0,) * nd, pipeline_mode=pl.Buffered(1))


def _prep_weights(w_in, w_out):
    depth = w_in.shape[0]
    steps = 4
    rb_in, rb_out = D_MODEL // steps, 3 * GROUP_W // steps
    return pl.pallas_call(
        _prep_kernel,
        grid=(depth, steps),
        in_specs=[
            pl.BlockSpec((None, N_IN, rb_in), lambda l, i: (l, 0, i)),
            pl.BlockSpec((None, rb_out, D_MODEL), lambda l, i: (l, i, 0)),
        ],
        out_specs=[
            pl.BlockSpec((None, rb_in, N_PROJ), lambda l, i: (l, i, 0)),
            pl.BlockSpec((None, GATE_ROWS, rb_in), lambda l, i: (l, 0, i)),
            pl.BlockSpec((None, D_MODEL // GROUP_W, rb_out, GROUP_W), lambda l, i: (l, 0, i, 0)),
        ],
        out_shape=[
            jax.ShapeDtypeStruct((depth, D_MODEL, N_PROJ), BF16),
            jax.ShapeDtypeStruct((depth, GATE_ROWS, D_MODEL), BF16),
            jax.ShapeDtypeStruct((depth, D_MODEL // GROUP_W, 3 * GROUP_W, GROUP_W), BF16),
        ],
        compiler_params=pltpu.CompilerParams(
            dimension_semantics=("arbitrary", "arbitrary"), vmem_limit_bytes=VMEM_LIMIT_BYTES),
        name="prep_weights",
    )(jnp.swapaxes(w_in, 1, 2), w_out)


def _mem_kv(mem, mem_norm_g, w_kv):
    depth, batch = w_kv.shape[0], mem.shape[0]
    return pl.pallas_call(
        _mem_kv_kernel,
        grid=(batch,),
        in_specs=[
            pl.BlockSpec((None, N_MEM, D_MODEL), lambda b: (b, 0, 0)),
            pl.BlockSpec((depth, 1, D_MODEL), lambda b: (0, 0, 0)),
            pl.BlockSpec((depth, D_MODEL, 2 * GROUP_W), lambda b: (0, 0, 0), pipeline_mode=pl.Buffered(1)),
        ],
        out_specs=[
            pl.BlockSpec((depth, None, GROUP_W, N_MEM), lambda b: (0, b, 0, 0)),
            pl.BlockSpec((depth, None, N_HEADS, N_MEM, 2 * HEAD_D), lambda b: (0, b, 0, 0, 0)),
        ],
        out_shape=[
            jax.ShapeDtypeStruct((depth, batch, GROUP_W, N_MEM), BF16),
            jax.ShapeDtypeStruct((depth, batch, N_HEADS, N_MEM, 2 * HEAD_D), BF16),
        ],
        scratch_shapes=[pltpu.VMEM((depth, D_MODEL, 2 * GROUP_W), BF16)],
        compiler_params=pltpu.CompilerParams(
            dimension_semantics=("arbitrary",), vmem_limit_bytes=VMEM_LIMIT_BYTES),
        name="mem_kv",
    )(mem, mem_norm_g.reshape(depth, 1, D_MODEL), w_kv)


def _layer(layer, final, x, kmt, vmx, ng, w_all, w_g, b_g, conv_w, conv_b, mng, pool_w, pool_scale, w_out, fng):
    batch, seq, _ = x.shape
    T = SEQ_TILE
    assert seq % (2 * T) == 0 and T % CHUNK == 0
    tiles_per_seq = seq // T
    pairs_per_seq = tiles_per_seq // 2
    last_tile = batch * tiles_per_seq - 1

    def next_tile(k):
        nt = jnp.minimum(2 * k + 2, last_tile)
        return nt // tiles_per_seq, nt % tiles_per_seq, 0

    kern = functools.partial(_layer_kernel, final=final, tiles_per_seq=tiles_per_seq)
    set_scratch = [
        pltpu.VMEM((CONV_HIST + T, 2 * GROUP_W), F32),
        pltpu.VMEM((POOL_HIST + T, GROUP_W), F32),
        pltpu.VMEM((T, P_COLS), F32),
        pltpu.VMEM((GATE_ROWS, T), F32),
        pltpu.VMEM((T, Z_COLS), BF16),
    ]
    return pl.pallas_call(
        kern,
        grid=(batch * pairs_per_seq,),
        in_specs=[
            pl.BlockSpec((None, 2 * T, D_MODEL), lambda k: (k // pairs_per_seq, k % pairs_per_seq, 0)),
            pl.BlockSpec((None, T, D_MODEL), next_tile),
            pl.BlockSpec((None, None, GROUP_W, N_MEM), lambda k: (layer, k // pairs_per_seq, 0, 0)),
            pl.BlockSpec((None, None, N_HEADS, N_MEM, 2 * HEAD_D),
                         lambda k: (layer, k // pairs_per_seq, 0, 0, 0)),
            _resident((1, D_MODEL), layer),
            _resident((D_MODEL, N_PROJ), layer),
            _resident((GATE_ROWS, D_MODEL), layer),
            _resident((2 * N_HEADS, LANES), layer),
            _resident((CONV_K, 2 * GROUP_W), layer),
            _resident((1, 2 * GROUP_W), layer),
            _resident((1, GROUP_W), layer),
            _resident((len(POOL_WINDOWS), LANES, LANES), layer),
            _resident((1, GROUP_W), layer),
            _resident((D_MODEL // GROUP_W, 3 * GROUP_W, GROUP_W), layer),
            pl.BlockSpec((1, D_MODEL), lambda k: (0, 0), pipeline_mode=pl.Buffered(1)),
        ],
        out_specs=pl.BlockSpec((None, 2 * T, D_MODEL), lambda k: (k // pairs_per_seq, k % pairs_per_seq, 0)),
        out_shape=jax.ShapeDtypeStruct(x.shape, F32),
        scratch_shapes=[pltpu.VMEM((T, D_MODEL), BF16)] + set_scratch + set_scratch + [
            pltpu.VMEM((T, GROUP_W), F32),
            pltpu.VMEM((T, GROUP_W), F32),
            pltpu.VMEM((T, N_HEADS * 2 * HEAD_D), BF16),
            pltpu.VMEM((N_HEADS, HEAD_D, 2 * HEAD_D), F32),
            pltpu.VMEM((2 * N_HEADS, LANES), F32),
            pltpu.VMEM((T, 3 * GROUP_W), BF16),
        ],
        compiler_params=pltpu.CompilerParams(
            dimension_semantics=("arbitrary",), vmem_limit_bytes=VMEM_LIMIT_BYTES),
        name="trunk_layer_final" if final else "trunk_layer",
    )(x, x, kmt, vmx, ng, w_all, w_g, b_g, conv_w, conv_b, mng, pool_w, pool_scale, w_out, fng)


def kernel(x, mem, norm_g, w_in, b_gates, conv_w, conv_b, mlstm_norm_g, pool_w, pool_scale,
           mem_norm_g, w_mem_kv, w_out, final_norm_g):
    depth = w_in.shape[0]
    w_all, w_g, w_out_b = _prep_weights(w_in, w_out)
    kmt, vmx = _mem_kv(mem, mem_norm_g, w_mem_kv)
    b_g = jnp.broadcast_to(b_gates[:, :, None], (depth, 2 * N_HEADS, LANES))
    args = (kmt, vmx, norm_g.reshape(depth, 1, D_MODEL), w_all, w_g, b_g,
            conv_w, conv_b.reshape(depth, 1, 2 * GROUP_W), mlstm_norm_g.reshape(depth, 1, GROUP_W),
            pool_w, pool_scale.reshape(depth, 1, GROUP_W), w_out_b, final_norm_g.reshape(1, D_MODEL))
    for l in range(depth):
        x = _layer(l, l == depth - 1, x, *args)
    return x
```

```python
import functools

import jax
import jax.numpy as jnp
from jax import lax
from jax.experimental import pallas as pl
from jax.experimental.pallas import tpu as pltpu

D_MODEL = 1024
GROUP_W = 512
N_HEADS = 4
HEAD_D = 128
CONV_K = 4
POOL_WINDOWS = (2, 4, 8, 16)
N_MEM = 256
EPS = 1e-6

SEQ_TILE = 256
CHUNK = 256
CONV_HIST = 8
POOL_HIST = 16
LANES = 128
VMEM_LIMIT_BYTES = 52 * 1024 * 1024

N_IN = 2 * GROUP_W + 3 * GROUP_W + 2 * N_HEADS + 4 * GROUP_W
GATE0 = 5 * GROUP_W
Q0, K0, V0, O0, ZA0, U0, ZB0, QC0, ZC0 = 0, 512, 1024, 1536, 2048, 2560, 3072, 3584, 4096
N_PROJ = 4608
GATE_ROWS = 16
P_V, P_O, P_ZA, P_ZB, P_QC, P_ZC = 0, 512, 1024, 1536, 2048, 2560
P_COLS = 3072

BF16 = jnp.bfloat16
F32 = jnp.float32


def _sigmoid(x):
    return 0.5 * jnp.tanh(0.5 * x) + 0.5


def _silu(x):
    hx = 0.5 * x
    return hx + hx * jnp.tanh(hx)


def _log_sigmoid(x):
    return jnp.minimum(x, 0.0) - jnp.log1p(jnp.exp(-jnp.abs(x)))


def _scan_lanes(x, op, identity):
    n = x.shape[1]
    lanes = lax.broadcasted_iota(jnp.int32, x.shape, 1)
    k = 1
    while k < n:
        x = op(x, jnp.where(lanes >= k, pltpu.roll(x, k, axis=1), identity))
        k *= 2
    return x


def _rmsnorm(x, g):
    ms = jnp.mean(x * x, axis=-1, keepdims=True)
    return (x * lax.rsqrt(ms + EPS)) * g


def _stage_a(x_ref, x_row, first, prev, cur, ng_ref, w_ref, wg_ref, h_ref):
    T = SEQ_TILE
    qk_prev, u_prev, _, _ = prev
    qk_cur, u_cur, proj_cur, g_cur = cur

    def head():
        qk_cur[0:CONV_HIST, :] = jnp.where(first, 0.0, qk_prev[T:T + CONV_HIST, :])
        u_cur[0:POOL_HIST, :] = jnp.where(first, 0.0, u_prev[T:T + POOL_HIST, :])
        h_ref[...] = _rmsnorm(x_ref[x_row:x_row + T, :], ng_ref[...]).astype(BF16)

    def proj(c0):
        return jnp.dot(h_ref[...], w_ref[:, c0:c0 + GROUP_W], preferred_element_type=F32)

    def dot_q():
        qk_cur[CONV_HIST:CONV_HIST + T, 0:GROUP_W] = proj(Q0)

    def dot_k():
        qk_cur[CONV_HIST:CONV_HIST + T, GROUP_W:2 * GROUP_W] = proj(K0)

    def dot_u():
        u_cur[POOL_HIST:POOL_HIST + T, :] = proj(U0)

    def dot_to(src, dst):
        def run():
            proj_cur[:, dst:dst + GROUP_W] = proj(src)
        return run

    def dot_gates():
        g_cur[...] = lax.dot_general(wg_ref[...], h_ref[...], (((1,), (1,)), ((), ())),
                                     preferred_element_type=F32)

    return [head, dot_q, dot_k, dot_u, dot_to(V0, P_V), dot_to(O0, P_O), dot_to(ZA0, P_ZA),
            dot_to(ZB0, P_ZB), dot_to(QC0, P_QC), dot_to(ZC0, P_ZC), dot_gates]


def _stage_b(x_ref, x_row, first, t_seq, cur, o_ref, kmt_ref, vmx_ref, bg_ref, cw_ref, cb_ref, mng_ref,
             pw_ref, ps_ref, wo_ref, fng_ref, q_ref, k_ref, vext_ref, cn_ref, m_ref, y_ref, final):
    T, L = SEQ_TILE, CHUNK
    qk_ext, u_ext, proj_ref, g_ref = cur

    def conv(c0):
        xs = qk_ext[:, c0:c0 + GROUP_W]
        acc = cb_ref[:, c0:c0 + GROUP_W] + cw_ref[CONV_K - 1:CONV_K, c0:c0 + GROUP_W] * xs[CONV_HIST:]
        for j in range(CONV_K - 1):
            back = pltpu.roll(xs, CONV_K - 1 - j, axis=0)[CONV_HIST:]
            acc = acc + cw_ref[j:j + 1, c0:c0 + GROUP_W] * back
        return _silu(acc)

    def conv_q():
        q_ref[...] = conv(0)

    def conv_k():
        k_ref[...] = conv(GROUP_W) * (HEAD_D ** -0.5)
        for hd in range(N_HEADS):
            vext_ref[:, hd * 2 * HEAD_D:hd * 2 * HEAD_D + HEAD_D] = (
                proj_ref[:, P_V + hd * HEAD_D:P_V + (hd + 1) * HEAD_D].astype(BF16))

    def mlstm_chunk(c):
        def run():
            rowi = lax.broadcasted_iota(jnp.int32, (L, L), 0)
            coli = lax.broadcasted_iota(jnp.int32, (L, L), 1)
            causal = coli <= rowi
            r0 = c * L
            g8 = g_ref[0:2 * N_HEADS, r0:r0 + L] + bg_ref[:, 0:1]
            gi = g8
            logf = _log_sigmoid(pltpu.roll(g8, N_HEADS, axis=0))
            b = _scan_lanes(logf, jnp.add, 0.0)
            r = gi - b
            m_prev = m_ref[:, 0:1]
            if c == 0:
                m_prev = jnp.where(first, 0.0, m_prev)
            mm = jnp.maximum(_scan_lanes(r, jnp.maximum, -jnp.inf), m_prev)
            a = jnp.exp(m_prev - mm)
            e = jnp.exp(-(mm + b))
            m_ref[...] = jnp.broadcast_to((mm + b)[:, L - 1:L], m_ref.shape)
            cols = jnp.concatenate([mm, a, e, jnp.zeros((LANES - 6 * N_HEADS, L), F32)], axis=0).T
            mm_c, a_c, e_c = (cols[:, i * 2 * N_HEADS:(i + 1) * 2 * N_HEADS] for i in range(3))
            heads = [slice(hd * HEAD_D, (hd + 1) * HEAD_D) for hd in range(N_HEADS)]
            qs = [q_ref[r0:r0 + L, hs] for hs in heads]
            vxs = [vext_ref[r0:r0 + L, hd * 2 * HEAD_D:(hd + 1) * 2 * HEAD_D] for hd in range(N_HEADS)]
            s_bf, k_tw = [], []
            for hd, hs in enumerate(heads):
                arg = r[hd:hd + 1, :] - mm_c[:, hd:hd + 1]
                p = jnp.exp(jnp.where(causal, arg, -1e30))
                k_t = k_ref[r0:r0 + L, hs].T
                s = jnp.dot(qs[hd].astype(BF16), k_t.astype(BF16), preferred_element_type=F32) * p
                s_bf.append(s.astype(BF16))
                k_tw.append((k_t * p[L - 1:L, :]).astype(BF16))
            nds = []
            for hd in range(N_HEADS):
                cn = cn_ref[hd]
                if c == 0:
                    cn = jnp.where(first, 0.0, cn)
                lhs = jnp.concatenate([(a_c[:, hd:hd + 1] * qs[hd]).astype(BF16), s_bf[hd]], axis=1)
                rhs = jnp.concatenate([cn.astype(BF16), vxs[hd]], axis=0)
                nds.append(jnp.dot(lhs, rhs, preferred_element_type=F32))
                cn_ref[hd] = (a[hd:hd + 1, L - 1:L] * cn
                              + jnp.dot(k_tw[hd], vxs[hd], preferred_element_type=F32))
            for hd, hs in enumerate(heads):
                nd = nds[hd]
                hh = nd[:, 0:HEAD_D] / jnp.maximum(jnp.abs(nd[:, HEAD_D:HEAD_D + 1]), e_c[:, hd:hd + 1])
                hg = _sigmoid(proj_ref[r0:r0 + L, P_O + hd * HEAD_D:P_O + (hd + 1) * HEAD_D]) * hh
                hg = hg * lax.rsqrt(jnp.mean(hg * hg, axis=-1, keepdims=True) + EPS)
                hg = hg * mng_ref[:, hs]
                ya = hg * _silu(proj_ref[r0:r0 + L, P_ZA + hd * HEAD_D:P_ZA + (hd + 1) * HEAD_D])
                y_ref[r0:r0 + L, hs] = ya.astype(BF16)
        return run

    def pool():
        tg = (t_seq * T + lax.broadcasted_iota(jnp.int32, (T, LANES), 0) + 1).astype(F32)
        zero_w = jnp.zeros((LANES, LANES), BF16)
        for g0 in range(0, len(POOL_WINDOWS), 2):
            pooled = []
            for g in (g0, g0 + 1):
                win = POOL_WINDOWS[g]
                gs = slice(g * LANES, (g + 1) * LANES)
                ssum = u_ext[:, gs]
                sh = 1
                while sh < win:
                    ssum = ssum + pltpu.roll(ssum, sh, axis=0)
                    sh *= 2
                u = u_ext[POOL_HIST:POOL_HIST + T, gs]
                pooled.append((ssum[POOL_HIST:POOL_HIST + T] / jnp.minimum(tg, float(win)) - u).astype(BF16))
            w_a, w_b = pw_ref[g0].astype(BF16), pw_ref[g0 + 1].astype(BF16)
            w_pair = jnp.concatenate([jnp.concatenate([w_a, zero_w], axis=1),
                                      jnp.concatenate([zero_w, w_b], axis=1)], axis=0)
            yb = jnp.dot(jnp.concatenate(pooled, axis=1), w_pair, preferred_element_type=F32)
            yb = (yb * ps_ref[:, g0 * LANES:(g0 + 2) * LANES]
                  * _silu(proj_ref[:, P_ZB + g0 * LANES:P_ZB + (g0 + 2) * LANES]))
            y_ref[:, GROUP_W + g0 * LANES:GROUP_W + (g0 + 2) * LANES] = yb.astype(BF16)

    def attention():
        for hd in range(N_HEADS):
            hs = slice(hd * HEAD_D, (hd + 1) * HEAD_D)
            qc = proj_ref[:, P_QC + hd * HEAD_D:P_QC + (hd + 1) * HEAD_D].astype(BF16)
            s = jnp.dot(qc, kmt_ref[hs, :], preferred_element_type=F32) * (HEAD_D ** -0.5)
            pexp = jnp.exp(s - jnp.max(s, axis=-1, keepdims=True))
            pv = jnp.dot(pexp.astype(BF16), vmx_ref[hd], preferred_element_type=F32)
            yc = pv[:, 0:HEAD_D] / pv[:, HEAD_D:HEAD_D + 1]
            yc = yc * _silu(proj_ref[:, P_ZC + hd * HEAD_D:P_ZC + (hd + 1) * HEAD_D])
            y_ref[:, 2 * GROUP_W + hd * HEAD_D:2 * GROUP_W + (hd + 1) * HEAD_D] = yc.astype(BF16)

    def out_proj():
        rows = slice(x_row, x_row + T)
        for i in range(D_MODEL // GROUP_W):
            cs = slice(i * GROUP_W, (i + 1) * GROUP_W)
            o_ref[rows, cs] = x_ref[rows, cs] + jnp.dot(y_ref[...], wo_ref[i], preferred_element_type=F32)
        if final:
            o_ref[rows, :] = _rmsnorm(o_ref[rows, :], fng_ref[...])

    return dict(conv_q=conv_q, conv_k=conv_k, chunks=[mlstm_chunk(c) for c in range(T // L)],
                pool=pool, attention=attention, out_proj=out_proj)


def _layer_kernel(xp_ref, xn_ref, kmt_ref, vmx_ref, ng_ref, w_ref, wg_ref, bg_ref, cw_ref, cb_ref, mng_ref,
                  pw_ref, ps_ref, wo_ref, fng_ref, o_ref,
                  h_ref, qk0, u0, proj0, g0, qk1, u1, proj1, g1, q_ref, k_ref, vext_ref, cn_ref, y_ref, m_ref,
                  *, final, tiles_per_seq):
    T = SEQ_TILE
    k = pl.program_id(0)
    set0, set1 = (qk0, u0, proj0, g0), (qk1, u1, proj1, g1)
    a_args = (ng_ref, w_ref, wg_ref, h_ref)
    b_args = (kmt_ref, vmx_ref, bg_ref, cw_ref, cb_ref, mng_ref, pw_ref, ps_ref, wo_ref, fng_ref,
              q_ref, k_ref, vext_ref, cn_ref, m_ref, y_ref, final)

    @pl.when(k == 0)
    def _prologue():
        col = lax.broadcasted_iota(jnp.int32, vext_ref.shape, 1)
        vext_ref[...] = jnp.where(col % (2 * HEAD_D) == HEAD_D, 1.0, 0.0).astype(BF16)
        cn_ref[...] = jnp.zeros(cn_ref.shape, F32)
        m_ref[...] = jnp.zeros(m_ref.shape, F32)
        qk1[T:T + CONV_HIST, :] = jnp.zeros((CONV_HIST, 2 * GROUP_W), F32)
        u1[T:T + POOL_HIST, :] = jnp.zeros((POOL_HIST, GROUP_W), F32)
        for piece in _stage_a(xp_ref, 0, True, set1, set0, *a_args):
            piece()

    t_even = (2 * k) % tiles_per_seq
    first_even = t_even == 0
    first_next = (2 * k + 2) % tiles_per_seq == 0

    def half_step(x_row, first, t_seq, cur, nxt, xa_ref, xa_row, first_a):
        for piece in _stage_a(xa_ref, xa_row, first_a, cur, nxt, *a_args):
            piece()
        b = _stage_b(xp_ref, x_row, first, t_seq, cur, o_ref, *b_args)
        for phase in (b["conv_q"], b["conv_k"], *b["chunks"], b["pool"], b["attention"], b["out_proj"]):
            phase()

    half_step(0, first_even, t_even, set0, set1, xp_ref, T, False)
    half_step(T, False, t_even + 1, set1, set0, xn_ref, 0, first_next)


def _mem_kv_kernel(mem_ref, g_ref, w_ref, kmt_ref, vmx_ref, wb_ref):
    @pl.when(pl.program_id(0) == 0)
    def _cast_weights():
        wb_ref[...] = w_ref[...].astype(BF16)

    mem = mem_ref[...]
    ms = jnp.mean(mem * mem, axis=-1, keepdims=True)
    nrm = mem * lax.rsqrt(ms + EPS)
    col = lax.broadcasted_iota(jnp.int32, (N_MEM, HEAD_D), 1)
    ones_col = jnp.where(col == 0, 1.0, 0.0).astype(BF16)
    for l in range(wb_ref.shape[0]):
        mn = (nrm * g_ref[l]).astype(BF16)
        kv = jnp.dot(mn, wb_ref[l], preferred_element_type=F32)
        for hd in range(N_HEADS):
            hs = slice(hd * HEAD_D, (hd + 1) * HEAD_D)
            kmt_ref[l, hs, :] = kv[:, hs].T.astype(BF16)
            vmx_ref[l, hd, :, 0:HEAD_D] = kv[:, GROUP_W + hd * HEAD_D:GROUP_W + (hd + 1) * HEAD_D].astype(BF16)
            vmx_ref[l, hd, :, HEAD_D:2 * HEAD_D] = ones_col


def _prep_kernel(wint_ref, wout_ref, wall_ref, wg_ref, wo_ref):
    wall_ref[:, 0:GATE0] = wint_ref[0:GATE0, :].T.astype(BF16)
    wall_ref[:, GATE0:N_PROJ] = wint_ref[GATE0 + 2 * N_HEADS:N_IN, :].T.astype(BF16)
    wg_ref[0:2 * N_HEADS, :] = wint_ref[GATE0:GATE0 + 2 * N_HEADS, :].astype(BF16)
    wg_ref[2 * N_HEADS:GATE_ROWS, :] = jnp.zeros((GATE_ROWS - 2 * N_HEADS, wg_ref.shape[1]), BF16)
    for i in range(D_MODEL // GROUP_W):
        wo_ref[i] = wout_ref[:, i * GROUP_W:(i + 1) * GROUP_W].astype(BF16)


def _resident(shape, layer):
    nd = len(shape)
    return pl.BlockSpec((None,) + shape, lambda k: (layer,) + (0,) * nd, pipeline_mode=pl.Buffered(1))


def _prep_weights(w_in, w_out):
    depth = w_in.shape[0]
    steps = 4
    rb_in, rb_out = D_MODEL // steps, 3 * GROUP_W // steps
    return pl.pallas_call(
        _prep_kernel,
        grid=(depth, steps),
        in_specs=[
            pl.BlockSpec((None, N_IN, rb_in), lambda l, i: (l, 0, i)),
            pl.BlockSpec((None, rb_out, D_MODEL), lambda l, i: (l, i, 0)),
        ],
        out_specs=[
            pl.BlockSpec((None, rb_in, N_PROJ), lambda l, i: (l, i, 0)),
            pl.BlockSpec((None, GATE_ROWS, rb_in), lambda l, i: (l, 0, i)),
            pl.BlockSpec((None, D_MODEL // GROUP_W, rb_out, GROUP_W), lambda l, i: (l, 0, i, 0)),
        ],
        out_shape=[
            jax.ShapeDtypeStruct((depth, D_MODEL, N_PROJ), BF16),
            jax.ShapeDtypeStruct((depth, GATE_ROWS, D_MODEL), BF16),
            jax.ShapeDtypeStruct((depth, D_MODEL // GROUP_W, 3 * GROUP_W, GROUP_W), BF16),
        ],
        compiler_params=pltpu.CompilerParams(
            dimension_semantics=("arbitrary", "arbitrary"), vmem_limit_bytes=VMEM_LIMIT_BYTES),
        name="prep_weights",
    )(jnp.swapaxes(w_in, 1, 2), w_out)


def _mem_kv(mem, mem_norm_g, w_kv):
    depth, batch = w_kv.shape[0], mem.shape[0]
    return pl.pallas_call(
        _mem_kv_kernel,
        grid=(batch,),
        in_specs=[
            pl.BlockSpec((None, N_MEM, D_MODEL), lambda b: (b, 0, 0)),
            pl.BlockSpec((depth, 1, D_MODEL), lambda b: (0, 0, 0)),
            pl.BlockSpec((depth, D_MODEL, 2 * GROUP_W), lambda b: (0, 0, 0), pipeline_mode=pl.Buffered(1)),
        ],
        out_specs=[
            pl.BlockSpec((depth, None, GROUP_W, N_MEM), lambda b: (0, b, 0, 0)),
            pl.BlockSpec((depth, None, N_HEADS, N_MEM, 2 * HEAD_D), lambda b: (0, b, 0, 0, 0)),
        ],
        out_shape=[
            jax.ShapeDtypeStruct((depth, batch, GROUP_W, N_MEM), BF16),
            jax.ShapeDtypeStruct((depth, batch, N_HEADS, N_MEM, 2 * HEAD_D), BF16),
        ],
        scratch_shapes=[pltpu.VMEM((depth, D_MODEL, 2 * GROUP_W), BF16)],
        compiler_params=pltpu.CompilerParams(
            dimension_semantics=("arbitrary",), vmem_limit_bytes=VMEM_LIMIT_BYTES),
        name="mem_kv",
    )(mem, mem_norm_g.reshape(depth, 1, D_MODEL), w_kv)


def _layer(layer, final, x, kmt, vmx, ng, w_all, w_g, b_g, conv_w, conv_b, mng, pool_w, pool_scale, w_out, fng):
    batch, seq, _ = x.shape
    T = SEQ_TILE
    assert seq % (2 * T) == 0 and T % CHUNK == 0
    tiles_per_seq = seq // T
    pairs_per_seq = tiles_per_seq // 2
    last_tile = batch * tiles_per_seq - 1

    def next_tile(k):
        nt = jnp.minimum(2 * k + 2, last_tile)
        return nt // tiles_per_seq, nt % tiles_per_seq, 0

    kern = functools.partial(_layer_kernel, final=final, tiles_per_seq=tiles_per_seq)
    set_scratch = [
        pltpu.VMEM((CONV_HIST + T, 2 * GROUP_W), F32),
        pltpu.VMEM((POOL_HIST + T, GROUP_W), F32),
        pltpu.VMEM((T, P_COLS), F32),
        pltpu.VMEM((GATE_ROWS, T), F32),
    ]
    return pl.pallas_call(
        kern,
        grid=(batch * pairs_per_seq,),
        in_specs=[
            pl.BlockSpec((None, 2 * T, D_MODEL), lambda k: (k // pairs_per_seq, k % pairs_per_seq, 0)),
            pl.BlockSpec((None, T, D_MODEL), next_tile),
            pl.BlockSpec((None, None, GROUP_W, N_MEM), lambda k: (layer, k // pairs_per_seq, 0, 0)),
            pl.BlockSpec((None, None, N_HEADS, N_MEM, 2 * HEAD_D),
                         lambda k: (layer, k // pairs_per_seq, 0, 0, 0)),
            _resident((1, D_MODEL), layer),
            _resident((D_MODEL, N_PROJ), layer),
            _resident((GATE_ROWS, D_MODEL), layer),
            _resident((2 * N_HEADS, LANES), layer),
            _resident((CONV_K, 2 * GROUP_W), layer),
            _resident((1, 2 * GROUP_W), layer),
            _resident((1, GROUP_W), layer),
            _resident((len(POOL_WINDOWS), LANES, LANES), layer),
            _resident((1, GROUP_W), layer),
            _resident((D_MODEL // GROUP_W, 3 * GROUP_W, GROUP_W), layer),
            pl.BlockSpec((1, D_MODEL), lambda k: (0, 0), pipeline_mode=pl.Buffered(1)),
        ],
        out_specs=pl.BlockSpec((None, 2 * T, D_MODEL), lambda k: (k // pairs_per_seq, k % pairs_per_seq, 0)),
        out_shape=jax.ShapeDtypeStruct(x.shape, F32),
        scratch_shapes=[pltpu.VMEM((T, D_MODEL), BF16)] + set_scratch + set_scratch + [
            pltpu.VMEM((T, GROUP_W), F32),
            pltpu.VMEM((T, GROUP_W), F32),
            pltpu.VMEM((T, N_HEADS * 2 * HEAD_D), BF16),
            pltpu.VMEM((N_HEADS, HEAD_D, 2 * HEAD_D), F32),
            pltpu.VMEM((T, 3 * GROUP_W), BF16),
            pltpu.VMEM((2 * N_HEADS, LANES), F32),
        ],
        compiler_params=pltpu.CompilerParams(
            dimension_semantics=("arbitrary",), vmem_limit_bytes=VMEM_LIMIT_BYTES),
        name="trunk_layer_final" if final else "trunk_layer",
    )(x, x, kmt, vmx, ng, w_all, w_g, b_g, conv_w, conv_b, mng, pool_w, pool_scale, w_out, fng)


def kernel(x, mem, norm_g, w_in, b_gates, conv_w, conv_b, mlstm_norm_g, pool_w, pool_scale,
           mem_norm_g, w_mem_kv, w_out, final_norm_g):
    depth = w_in.shape[0]
    w_all, w_g, w_out_b = _prep_weights(w_in, w_out)
    kmt, vmx = _mem_kv(mem, mem_norm_g, w_mem_kv)
    b_g = jnp.broadcast_to(b_gates[:, :, None], (depth, 2 * N_HEADS, LANES))
    args = (kmt, vmx, norm_g.reshape(depth, 1, D_MODEL), w_all, w_g, b_g,
            conv_w, conv_b.reshape(depth, 1, 2 * GROUP_W), mlstm_norm_g.reshape(depth, 1, GROUP_W),
            pool_w, pool_scale.reshape(depth, 1, GROUP_W), w_out_b, final_norm_g.reshape(1, D_MODEL))
    for l in range(depth):
        x = _layer(l, l == depth - 1, x, *args)
    return x
```

```python
import functools

import jax
import jax.numpy as jnp
from jax import lax
from jax.experimental import pallas as pl
from jax.experimental.pallas import tpu as pltpu

D_MODEL = 1024
GROUP_W = 512
N_HEADS = 4
HEAD_D = 128
CONV_K = 4
POOL_WINDOWS = (2, 4, 8, 16)
N_MEM = 256
EPS = 1e-6

SEQ_TILE = 256
CHUNK = 256
CONV_HIST = 8
POOL_HIST = 16
LANES = 128
VMEM_LIMIT_BYTES = 52 * 1024 * 1024

N_IN = 2 * GROUP_W + 3 * GROUP_W + 2 * N_HEADS + 4 * GROUP_W
GATE0 = 5 * GROUP_W
Q0, K0, V0, O0, ZA0, U0, ZB0, QC0, ZC0 = 0, 512, 1024, 1536, 2048, 2560, 3072, 3584, 4096
N_PROJ = 4608
GATE_ROWS = 16
P_V, P_O, P_ZA, P_ZB, P_QC, P_ZC = 0, 512, 1024, 1536, 2048, 2560
P_COLS = 3072

BF16 = jnp.bfloat16
F32 = jnp.float32


def _sigmoid(x):
    return 0.5 * jnp.tanh(0.5 * x) + 0.5


def _silu(x):
    hx = 0.5 * x
    return hx + hx * jnp.tanh(hx)


def _log_sigmoid(x):
    return jnp.minimum(x, 0.0) - jnp.log1p(jnp.exp(-jnp.abs(x)))


def _scan_lanes(x, op, identity):
    n = x.shape[1]
    lanes = lax.broadcasted_iota(jnp.int32, x.shape, 1)
    k = 1
    while k < n:
        x = op(x, jnp.where(lanes >= k, pltpu.roll(x, k, axis=1), identity))
        k *= 2
    return x


def _rmsnorm(x, g):
    ms = jnp.mean(x * x, axis=-1, keepdims=True)
    return (x * lax.rsqrt(ms + EPS)) * g


def _stage_a(x_ref, x_row, first, prev, cur, ng_ref, w_ref, wg_ref, h_ref):
    T = SEQ_TILE
    qk_prev, u_prev, _, _ = prev
    qk_cur, u_cur, proj_cur, g_cur = cur

    def head():
        qk_cur[0:CONV_HIST, :] = jnp.where(first, 0.0, qk_prev[T:T + CONV_HIST, :])
        u_cur[0:POOL_HIST, :] = jnp.where(first, 0.0, u_prev[T:T + POOL_HIST, :])
        h_ref[...] = _rmsnorm(x_ref[x_row:x_row + T, :], ng_ref[...]).astype(BF16)

    def proj(c0):
        return jnp.dot(h_ref[...], w_ref[:, c0:c0 + GROUP_W], preferred_element_type=F32)

    def dot_q():
        qk_cur[CONV_HIST:CONV_HIST + T, 0:GROUP_W] = proj(Q0)

    def dot_k():
        qk_cur[CONV_HIST:CONV_HIST + T, GROUP_W:2 * GROUP_W] = proj(K0)

    def dot_u():
        u_cur[POOL_HIST:POOL_HIST + T, :] = proj(U0)

    def dot_to(src, dst):
        def run():
            proj_cur[:, dst:dst + GROUP_W] = proj(src)
        return run

    def dot_gates():
        g_cur[...] = lax.dot_general(wg_ref[...], h_ref[...], (((1,), (1,)), ((), ())),
                                     preferred_element_type=F32)

    return [head, dot_q, dot_k, dot_u, dot_to(V0, P_V), dot_to(O0, P_O), dot_to(ZA0, P_ZA),
            dot_to(ZB0, P_ZB), dot_to(QC0, P_QC), dot_to(ZC0, P_ZC), dot_gates]


def _stage_b(x_ref, x_row, first, t_seq, cur, o_ref, kmt_ref, vmx_ref, bg_ref, cw_ref, cb_ref, mng_ref,
             pw_ref, ps_ref, wo_ref, fng_ref, q_ref, k_ref, vext_ref, cn_ref, m_ref, y_ref, final):
    T, L = SEQ_TILE, CHUNK
    qk_ext, u_ext, proj_ref, g_ref = cur

    def conv(c0):
        xs = qk_ext[:, c0:c0 + GROUP_W]
        acc = cb_ref[:, c0:c0 + GROUP_W] + cw_ref[CONV_K - 1:CONV_K, c0:c0 + GROUP_W] * xs[CONV_HIST:]
        for j in range(CONV_K - 1):
            back = pltpu.roll(xs, CONV_K - 1 - j, axis=0)[CONV_HIST:]
            acc = acc + cw_ref[j:j + 1, c0:c0 + GROUP_W] * back
        return _silu(acc)

    def conv_q():
        q_ref[...] = conv(0)

    def conv_k():
        k_ref[...] = conv(GROUP_W) * (HEAD_D ** -0.5)
        for hd in range(N_HEADS):
            vext_ref[:, hd * 2 * HEAD_D:hd * 2 * HEAD_D + HEAD_D] = (
                proj_ref[:, P_V + hd * HEAD_D:P_V + (hd + 1) * HEAD_D].astype(BF16))

    def mlstm_chunk(c):
        def run():
            rowi = lax.broadcasted_iota(jnp.int32, (L, L), 0)
            coli = lax.broadcasted_iota(jnp.int32, (L, L), 1)
            causal = coli <= rowi
            r0 = c * L
            g8 = g_ref[0:2 * N_HEADS, r0:r0 + L] + bg_ref[:, 0:1]
            gi = g8
            logf = _log_sigmoid(pltpu.roll(g8, N_HEADS, axis=0))
            b = _scan_lanes(logf, jnp.add, 0.0)
            r = gi - b
            m_prev = m_ref[:, 0:1]
            if c == 0:
                m_prev = jnp.where(first, 0.0, m_prev)
            mm = jnp.maximum(_scan_lanes(r, jnp.maximum, -jnp.inf), m_prev)
            a = jnp.exp(m_prev - mm)
            e = jnp.exp(-(mm + b))
            m_ref[...] = jnp.broadcast_to((mm + b)[:, L - 1:L], m_ref.shape)
            cols = jnp.concatenate([mm, a, e, jnp.zeros((LANES - 6 * N_HEADS, L), F32)], axis=0).T
            mm_c, a_c, e_c = (cols[:, i * 2 * N_HEADS:(i + 1) * 2 * N_HEADS] for i in range(3))
            heads = [slice(hd * HEAD_D, (hd + 1) * HEAD_D) for hd in range(N_HEADS)]
            qs = [q_ref[r0:r0 + L, hs] for hs in heads]
            vxs = [vext_ref[r0:r0 + L, hd * 2 * HEAD_D:(hd + 1) * 2 * HEAD_D] for hd in range(N_HEADS)]
            s_bf, k_tw = [], []
            for hd, hs in enumerate(heads):
                arg = r[hd:hd + 1, :] - mm_c[:, hd:hd + 1]
                p = jnp.exp(jnp.where(causal, arg, -1e30))
                k_t = k_ref[r0:r0 + L, hs].T
                s = jnp.dot(qs[hd].astype(BF16), k_t.astype(BF16), preferred_element_type=F32) * p
                s_bf.append(s.astype(BF16))
                k_tw.append((k_t * p[L - 1:L, :]).astype(BF16))
            nds = []
            for hd in range(N_HEADS):
                cn = cn_ref[hd]
                if c == 0:
                    cn = jnp.where(first, 0.0, cn)
                lhs = jnp.concatenate([(a_c[:, hd:hd + 1] * qs[hd]).astype(BF16), s_bf[hd]], axis=1)
                rhs = jnp.concatenate([cn.astype(BF16), vxs[hd]], axis=0)
                nds.append(jnp.dot(lhs, rhs, preferred_element_type=F32))
                cn_ref[hd] = (a[hd:hd + 1, L - 1:L] * cn
                              + jnp.dot(k_tw[hd], vxs[hd], preferred_element_type=F32))
            for hd, hs in enumerate(heads):
                nd = nds[hd]
                hh = nd[:, 0:HEAD_D] / jnp.maximum(jnp.abs(nd[:, HEAD_D:HEAD_D + 1]), e_c[:, hd:hd + 1])
                hg = _sigmoid(proj_ref[r0:r0 + L, P_O + hd * HEAD_D:P_O + (hd + 1) * HEAD_D]) * hh
                hg = hg * lax.rsqrt(jnp.mean(hg * hg, axis=-1, keepdims=True) + EPS)
                hg = hg * mng_ref[:, hs]
                ya = hg * _silu(proj_ref[r0:r0 + L, P_ZA + hd * HEAD_D:P_ZA + (hd + 1) * HEAD_D])
                y_ref[r0:r0 + L, hs] = ya.astype(BF16)
        return run

    def pool():
        tg = (t_seq * T + lax.broadcasted_iota(jnp.int32, (T, LANES), 0) + 1).astype(F32)
        zero_w = jnp.zeros((LANES, LANES), BF16)
        for g0 in range(0, len(POOL_WINDOWS), 2):
            pooled = []
            for g in (g0, g0 + 1):
                win = POOL_WINDOWS[g]
                gs = slice(g * LANES, (g + 1) * LANES)
                ssum = u_ext[:, gs]
                sh = 1
                while sh < win:
                    ssum = ssum + pltpu.roll(ssum, sh, axis=0)
                    sh *= 2
                u = u_ext[POOL_HIST:POOL_HIST + T, gs]
                pooled.append((ssum[POOL_HIST:POOL_HIST + T] / jnp.minimum(tg, float(win)) - u).astype(BF16))
            w_a, w_b = pw_ref[g0].astype(BF16), pw_ref[g0 + 1].astype(BF16)
            w_pair = jnp.concatenate([jnp.concatenate([w_a, zero_w], axis=1),
                                      jnp.concatenate([zero_w, w_b], axis=1)], axis=0)
            yb = jnp.dot(jnp.concatenate(pooled, axis=1), w_pair, preferred_element_type=F32)
            yb = (yb * ps_ref[:, g0 * LANES:(g0 + 2) * LANES]
                  * _silu(proj_ref[:, P_ZB + g0 * LANES:P_ZB + (g0 + 2) * LANES]))
            y_ref[:, GROUP_W + g0 * LANES:GROUP_W + (g0 + 2) * LANES] = yb.astype(BF16)

    def attention():
        for hd in range(N_HEADS):
            hs = slice(hd * HEAD_D, (hd + 1) * HEAD_D)
            qc = proj_ref[:, P_QC + hd * HEAD_D:P_QC + (hd + 1) * HEAD_D].astype(BF16)
            s = jnp.dot(qc, kmt_ref[hs, :], preferred_element_type=F32) * (HEAD_D ** -0.5)
            pexp = jnp.exp(s - jnp.max(s, axis=-1, keepdims=True))
            pv = jnp.dot(pexp.astype(BF16), vmx_ref[hd], preferred_element_type=F32)
            yc = pv[:, 0:HEAD_D] / pv[:, HEAD_D:HEAD_D + 1]
            yc = yc * _silu(proj_ref[:, P_ZC + hd * HEAD_D:P_ZC + (hd + 1) * HEAD_D])
            y_ref[:, 2 * GROUP_W + hd * HEAD_D:2 * GROUP_W + (hd + 1) * HEAD_D] = yc.astype(BF16)

    def out_proj():
        rows = slice(x_row, x_row + T)
        for i in range(D_MODEL // GROUP_W):
            cs = slice(i * GROUP_W, (i + 1) * GROUP_W)
            o_ref[rows, cs] = x_ref[rows, cs] + jnp.dot(y_ref[...], wo_ref[i], preferred_element_type=F32)
        if final:
            o_ref[rows, :] = _rmsnorm(o_ref[rows, :], fng_ref[...])

    return dict(conv_q=conv_q, conv_k=conv_k, chunks=[mlstm_chunk(c) for c in range(T // L)],
                pool=pool, attention=attention, out_proj=out_proj)


def _layer_kernel(xp_ref, xn_ref, kmt_ref, vmx_ref, w_ref, wo_ref, wg_ref, cw_ref, pw_ref,
                  ng_ref, bg_ref, cb_ref, mng_ref, ps_ref, fng_ref, o_ref,
                  h_ref, qk0, u0, proj0, g0, qk1, u1, proj1, g1, q_ref, k_ref, vext_ref, cn_ref, y_ref, m_ref,
                  *, final, tiles_per_seq):
    T = SEQ_TILE
    k = pl.program_id(0)
    set0, set1 = (qk0, u0, proj0, g0), (qk1, u1, proj1, g1)
    a_args = (ng_ref, w_ref, wg_ref, h_ref)
    b_args = (kmt_ref, vmx_ref, bg_ref, cw_ref, cb_ref, mng_ref, pw_ref, ps_ref, wo_ref, fng_ref,
              q_ref, k_ref, vext_ref, cn_ref, m_ref, y_ref, final)

    @pl.when(k == 0)
    def _prologue():
        col = lax.broadcasted_iota(jnp.int32, vext_ref.shape, 1)
        vext_ref[...] = jnp.where(col % (2 * HEAD_D) == HEAD_D, 1.0, 0.0).astype(BF16)
        cn_ref[...] = jnp.zeros(cn_ref.shape, F32)
        m_ref[...] = jnp.zeros(m_ref.shape, F32)
        qk1[T:T + CONV_HIST, :] = jnp.zeros((CONV_HIST, 2 * GROUP_W), F32)
        u1[T:T + POOL_HIST, :] = jnp.zeros((POOL_HIST, GROUP_W), F32)
        for piece in _stage_a(xp_ref, 0, True, set1, set0, *a_args):
            piece()

    t_even = (2 * k) % tiles_per_seq
    first_even = t_even == 0
    first_next = (2 * k + 2) % tiles_per_seq == 0

    def half_step(x_row, first, t_seq, cur, nxt, xa_ref, xa_row, first_a):
        for piece in _stage_a(xa_ref, xa_row, first_a, cur, nxt, *a_args):
            piece()
        b = _stage_b(xp_ref, x_row, first, t_seq, cur, o_ref, *b_args)
        for phase in (b["conv_q"], b["conv_k"], *b["chunks"], b["pool"], b["attention"], b["out_proj"]):
            phase()

    half_step(0, first_even, t_even, set0, set1, xp_ref, T, False)
    half_step(T, False, t_even + 1, set1, set0, xn_ref, 0, first_next)


def _mem_kv_kernel(mem_ref, g_ref, w_ref, kmt_ref, vmx_ref, wb_ref):
    @pl.when(pl.program_id(0) == 0)
    def _cast_weights():
        wb_ref[...] = w_ref[...].astype(BF16)

    mem = mem_ref[...]
    ms = jnp.mean(mem * mem, axis=-1, keepdims=True)
    nrm = mem * lax.rsqrt(ms + EPS)
    col = lax.broadcasted_iota(jnp.int32, (N_MEM, HEAD_D), 1)
    ones_col = jnp.where(col == 0, 1.0, 0.0).astype(BF16)
    for l in range(wb_ref.shape[0]):
        mn = (nrm * g_ref[l]).astype(BF16)
        kv = jnp.dot(mn, wb_ref[l], preferred_element_type=F32)
        for hd in range(N_HEADS):
            hs = slice(hd * HEAD_D, (hd + 1) * HEAD_D)
            kmt_ref[l, hs, :] = kv[:, hs].T.astype(BF16)
            vmx_ref[l, hd, :, 0:HEAD_D] = kv[:, GROUP_W + hd * HEAD_D:GROUP_W + (hd + 1) * HEAD_D].astype(BF16)
            vmx_ref[l, hd, :, HEAD_D:2 * HEAD_D] = ones_col


def _prep_kernel(wint_ref, wout_ref, wall_ref, wg_ref, wo_ref):
    wall_ref[:, 0:GATE0] = wint_ref[0:GATE0, :].T.astype(BF16)
    wall_ref[:, GATE0:N_PROJ] = wint_ref[GATE0 + 2 * N_HEADS:N_IN, :].T.astype(BF16)
    wg_ref[0:2 * N_HEADS, :] = wint_ref[GATE0:GATE0 + 2 * N_HEADS, :].astype(BF16)
    wg_ref[2 * N_HEADS:GATE_ROWS, :] = jnp.zeros((GATE_ROWS - 2 * N_HEADS, wg_ref.shape[1]), BF16)
    for i in range(D_MODEL // GROUP_W):
        wo_ref[i] = wout_ref[:, i * GROUP_W:(i + 1) * GROUP_W].astype(BF16)


def _resident(shape, layer):
    nd = len(shape)
    return pl.BlockSpec((None,) + shape, lambda k: (layer,) + (0,) * nd, pipeline_mode=pl.Buffered(1))


def _prep_weights(w_in, w_out):
    depth = w_in.shape[0]
    steps = 4
    rb_in, rb_out = D_MODEL // steps, 3 * GROUP_W // steps
    return pl.pallas_call(
        _prep_kernel,
        grid=(depth, steps),
        in_specs=[
            pl.BlockSpec((None, N_IN, rb_in), lambda l, i: (l, 0, i)),
            pl.BlockSpec((None, rb_out, D_MODEL), lambda l, i: (l, i, 0)),
        ],
        out_specs=[
            pl.BlockSpec((None, rb_in, N_PROJ), lambda l, i: (l, i, 0)),
            pl.BlockSpec((None, GATE_ROWS, rb_in), lambda l, i: (l, 0, i)),
            pl.BlockSpec((None, D_MODEL // GROUP_W, rb_out, GROUP_W), lambda l, i: (l, 0, i, 0)),
        ],
        out_shape=[
            jax.ShapeDtypeStruct((depth, D_MODEL, N_PROJ), BF16),
            jax.ShapeDtypeStruct((depth, GATE_ROWS, D_MODEL), BF16),
            jax.ShapeDtypeStruct((depth, D_MODEL // GROUP_W, 3 * GROUP_W, GROUP_W), BF16),
        ],
        compiler_params=pltpu.CompilerParams(
            dimension_semantics=("arbitrary", "arbitrary"), vmem_limit_bytes=VMEM_LIMIT_BYTES),
        name="prep_weights",
    )(jnp.swapaxes(w_in, 1, 2), w_out)


def _mem_kv(mem, mem_norm_g, w_kv):
    depth, batch = w_kv.shape[0], mem.shape[0]
    return pl.pallas_call(
        _mem_kv_kernel,
        grid=(batch,),
        in_specs=[
            pl.BlockSpec((None, N_MEM, D_MODEL), lambda b: (b, 0, 0)),
            pl.BlockSpec((depth, 1, D_MODEL), lambda b: (0, 0, 0)),
            pl.BlockSpec((depth, D_MODEL, 2 * GROUP_W), lambda b: (0, 0, 0), pipeline_mode=pl.Buffered(1)),
        ],
        out_specs=[
            pl.BlockSpec((depth, None, GROUP_W, N_MEM), lambda b: (0, b, 0, 0)),
            pl.BlockSpec((depth, None, N_HEADS, N_MEM, 2 * HEAD_D), lambda b: (0, b, 0, 0, 0)),
        ],
        out_shape=[
            jax.ShapeDtypeStruct((depth, batch, GROUP_W, N_MEM), BF16),
            jax.ShapeDtypeStruct((depth, batch, N_HEADS, N_MEM, 2 * HEAD_D), BF16),
        ],
        scratch_shapes=[pltpu.VMEM((depth, D_MODEL, 2 * GROUP_W), BF16)],
        compiler_params=pltpu.CompilerParams(
            dimension_semantics=("arbitrary",), vmem_limit_bytes=VMEM_LIMIT_BYTES),
        name="mem_kv",
    )(mem, mem_norm_g.reshape(depth, 1, D_MODEL), w_kv)


def _layer(layer, final, x, kmt, vmx, ng, w_all, w_g, b_g, conv_w, conv_b, mng, pool_w, pool_scale, w_out, fng):
    batch, seq, _ = x.shape
    T = SEQ_TILE
    assert seq % (2 * T) == 0 and T % CHUNK == 0
    tiles_per_seq = seq // T
    pairs_per_seq = tiles_per_seq // 2
    last_tile = batch * tiles_per_seq - 1

    def next_tile(k):
        nt = jnp.minimum(2 * k + 2, last_tile)
        return nt // tiles_per_seq, nt % tiles_per_seq, 0

    kern = functools.partial(_layer_kernel, final=final, tiles_per_seq=tiles_per_seq)
    set_scratch = [
        pltpu.VMEM((CONV_HIST + T, 2 * GROUP_W), F32),
        pltpu.VMEM((POOL_HIST + T, GROUP_W), F32),
        pltpu.VMEM((T, P_COLS), F32),
        pltpu.VMEM((GATE_ROWS, T), F32),
    ]
    return pl.pallas_call(
        kern,
        grid=(batch * pairs_per_seq,),
        in_specs=[
            pl.BlockSpec((None, 2 * T, D_MODEL), lambda k: (k // pairs_per_seq, k % pairs_per_seq, 0)),
            pl.BlockSpec((None, T, D_MODEL), next_tile),
            pl.BlockSpec((None, None, GROUP_W, N_MEM), lambda k: (layer, k // pairs_per_seq, 0, 0)),
            pl.BlockSpec((None, None, N_HEADS, N_MEM, 2 * HEAD_D),
                         lambda k: (layer, k // pairs_per_seq, 0, 0, 0)),
            _resident((D_MODEL, N_PROJ), layer),
            _resident((D_MODEL // GROUP_W, 3 * GROUP_W, GROUP_W), layer),
            _resident((GATE_ROWS, D_MODEL), layer),
            _resident((CONV_K, 2 * GROUP_W), layer),
            _resident((len(POOL_WINDOWS), LANES, LANES), layer),
            _resident((1, D_MODEL), layer),
            _resident((2 * N_HEADS, LANES), layer),
            _resident((1, 2 * GROUP_W), layer),
            _resident((1, GROUP_W), layer),
            _resident((1, GROUP_W), layer),
            pl.BlockSpec((1, D_MODEL), lambda k: (0, 0), pipeline_mode=pl.Buffered(1)),
        ],
        out_specs=pl.BlockSpec((None, 2 * T, D_MODEL), lambda k: (k // pairs_per_seq, k % pairs_per_seq, 0)),
        out_shape=jax.ShapeDtypeStruct(x.shape, F32),
        scratch_shapes=[pltpu.VMEM((T, D_MODEL), BF16)] + set_scratch + set_scratch + [
            pltpu.VMEM((T, GROUP_W), F32),
            pltpu.VMEM((T, GROUP_W), F32),
            pltpu.VMEM((T, N_HEADS * 2 * HEAD_D), BF16),
            pltpu.VMEM((N_HEADS, HEAD_D, 2 * HEAD_D), F32),
            pltpu.VMEM((T, 3 * GROUP_W), BF16),
            pltpu.VMEM((2 * N_HEADS, LANES), F32),
        ],
        compiler_params=pltpu.CompilerParams(
            dimension_semantics=("arbitrary",), vmem_limit_bytes=VMEM_LIMIT_BYTES),
        name="trunk_layer_final" if final else "trunk_layer",
    )(x, x, kmt, vmx, w_all, w_out, w_g, conv_w, pool_w, ng, b_g, conv_b, mng, pool_scale, fng)


def kernel(x, mem, norm_g, w_in, b_gates, conv_w, conv_b, mlstm_norm_g, pool_w, pool_scale,
           mem_norm_g, w_mem_kv, w_out, final_norm_g):
    depth = w_in.shape[0]
    w_all, w_g, w_out_b = _prep_weights(w_in, w_out)
    kmt, vmx = _mem_kv(mem, mem_norm_g, w_mem_kv)
    b_g = jnp.broadcast_to(b_gates[:, :, None], (depth, 2 * N_HEADS, LANES))
    args = (kmt, vmx, norm_g.reshape(depth, 1, D_MODEL), w_all, w_g, b_g,
            conv_w, conv_b.reshape(depth, 1, 2 * GROUP_W), mlstm_norm_g.reshape(depth, 1, GROUP_W),
            pool_w, pool_scale.reshape(depth, 1, GROUP_W), w_out_b, final_norm_g.reshape(1, D_MODEL))
    for l in range(depth):
        x = _layer(l, l == depth - 1, x, *args)
    return x
```

```python
import functools

import jax
import jax.numpy as jnp
from jax import lax
from jax.experimental import pallas as pl
from jax.experimental.pallas import tpu as pltpu

D_MODEL = 1024
GROUP_W = 512
N_HEADS = 4
HEAD_D = 128
CONV_K = 4
POOL_WINDOWS = (2, 4, 8, 16)
N_MEM = 256
EPS = 1e-6

SEQ_TILE = 256
CHUNK = 256
CONV_HIST = 8
POOL_HIST = 16
LANES = 128
VMEM_LIMIT_BYTES = 52 * 1024 * 1024

N_IN = 2 * GROUP_W + 3 * GROUP_W + 2 * N_HEADS + 4 * GROUP_W
GATE0 = 5 * GROUP_W
Q0, K0, V0, O0, ZA0, U0, ZB0, QC0, ZC0 = 0, 512, 1024, 1536, 2048, 2560, 3072, 3584, 4096
N_PROJ = 4608
GATE_ROWS = 16
P_V, P_O, P_ZA, P_ZB, P_QC, P_ZC = 0, 512, 1024, 1536, 2048, 2560
P_COLS = 3072

BF16 = jnp.bfloat16
F32 = jnp.float32


def _sigmoid(x):
    return 0.5 * jnp.tanh(0.5 * x) + 0.5


def _silu(x):
    hx = 0.5 * x
    return hx + hx * jnp.tanh(hx)


def _log_sigmoid(x):
    return jnp.minimum(x, 0.0) - jnp.log1p(jnp.exp(-jnp.abs(x)))


def _scan_lanes(x, op, identity):
    n = x.shape[1]
    lanes = lax.broadcasted_iota(jnp.int32, x.shape, 1)
    k = 1
    while k < n:
        x = op(x, jnp.where(lanes >= k, pltpu.roll(x, k, axis=1), identity))
        k *= 2
    return x


def _rmsnorm(x, g):
    ms = jnp.mean(x * x, axis=-1, keepdims=True)
    return (x * lax.rsqrt(ms + EPS)) * g


def _stage_a(x_ref, x_row, first, prev, cur, ng_ref, w_ref, wg_ref, h_ref):
    T = SEQ_TILE
    qk_prev, u_prev, _, _ = prev
    qk_cur, u_cur, proj_cur, g_cur = cur

    def head():
        qk_cur[0:CONV_HIST, :] = jnp.where(first, 0.0, qk_prev[T:T + CONV_HIST, :])
        u_cur[0:POOL_HIST, :] = jnp.where(first, 0.0, u_prev[T:T + POOL_HIST, :])
        h_ref[...] = _rmsnorm(x_ref[x_row:x_row + T, :], ng_ref[...]).astype(BF16)

    def proj(c0):
        return jnp.dot(h_ref[...], w_ref[:, c0:c0 + GROUP_W], preferred_element_type=F32)

    def dot_q():
        qk_cur[CONV_HIST:CONV_HIST + T, 0:GROUP_W] = proj(Q0)

    def dot_k():
        qk_cur[CONV_HIST:CONV_HIST + T, GROUP_W:2 * GROUP_W] = proj(K0)

    def dot_u():
        u_cur[POOL_HIST:POOL_HIST + T, :] = proj(U0)

    def dot_to(src, dst):
        def run():
            proj_cur[:, dst:dst + GROUP_W] = proj(src)
        return run

    def dot_gates():
        g_cur[...] = lax.dot_general(wg_ref[...], h_ref[...], (((1,), (1,)), ((), ())),
                                     preferred_element_type=F32)

    return [head, dot_q, dot_k, dot_u, dot_to(V0, P_V), dot_to(O0, P_O), dot_to(ZA0, P_ZA),
            dot_to(ZB0, P_ZB), dot_to(QC0, P_QC), dot_to(ZC0, P_ZC), dot_gates]


def _stage_b(x_ref, x_row, first, t_seq, cur, o_ref, kmt_ref, vmx_ref, bg_ref, cw_ref, cb_ref, mng_ref,
             pw_ref, ps_ref, wo_ref, fng_ref, q_ref, k_ref, vext_ref, cn_ref, m_ref, y_ref, final):
    T, L = SEQ_TILE, CHUNK
    qk_ext, u_ext, proj_ref, g_ref = cur

    def conv(c0):
        xs = qk_ext[:, c0:c0 + GROUP_W]
        acc = cb_ref[:, c0:c0 + GROUP_W] + cw_ref[CONV_K - 1:CONV_K, c0:c0 + GROUP_W] * xs[CONV_HIST:]
        for j in range(CONV_K - 1):
            back = pltpu.roll(xs, CONV_K - 1 - j, axis=0)[CONV_HIST:]
            acc = acc + cw_ref[j:j + 1, c0:c0 + GROUP_W] * back
        return _silu(acc)

    def conv_q():
        q_ref[...] = conv(0)

    def conv_k():
        k_ref[...] = conv(GROUP_W) * (HEAD_D ** -0.5)
        for hd in range(N_HEADS):
            vext_ref[:, hd * 2 * HEAD_D:hd * 2 * HEAD_D + HEAD_D] = (
                proj_ref[:, P_V + hd * HEAD_D:P_V + (hd + 1) * HEAD_D].astype(BF16))

    def mlstm_chunk(c):
        def run():
            rowi = lax.broadcasted_iota(jnp.int32, (L, L), 0)
            coli = lax.broadcasted_iota(jnp.int32, (L, L), 1)
            causal = coli <= rowi
            r0 = c * L
            g8 = g_ref[0:2 * N_HEADS, r0:r0 + L] + bg_ref[:, 0:1]
            gi = g8
            logf = _log_sigmoid(pltpu.roll(g8, N_HEADS, axis=0))
            b = _scan_lanes(logf, jnp.add, 0.0)
            r = gi - b
            m_prev = m_ref[:, 0:1]
            if c == 0:
                m_prev = jnp.where(first, 0.0, m_prev)
            mm = jnp.maximum(_scan_lanes(r, jnp.maximum, -jnp.inf), m_prev)
            a = jnp.exp(m_prev - mm)
            e = jnp.exp(-(mm + b))
            m_ref[...] = jnp.broadcast_to((mm + b)[:, L - 1:L], m_ref.shape)
            cols = jnp.concatenate([mm, a, e, jnp.zeros((LANES - 6 * N_HEADS, L), F32)], axis=0).T
            mm_c, a_c, e_c = (cols[:, i * 2 * N_HEADS:(i + 1) * 2 * N_HEADS] for i in range(3))
            heads = [slice(hd * HEAD_D, (hd + 1) * HEAD_D) for hd in range(N_HEADS)]
            qs = [q_ref[r0:r0 + L, hs] for hs in heads]
            vxs = [vext_ref[r0:r0 + L, hd * 2 * HEAD_D:(hd + 1) * 2 * HEAD_D] for hd in range(N_HEADS)]
            s_bf, k_tw = [], []
            for hd, hs in enumerate(heads):
                arg = r[hd:hd + 1, :] - mm_c[:, hd:hd + 1]
                p = jnp.exp(jnp.where(causal, arg, -1e30))
                k_t = k_ref[r0:r0 + L, hs].T
                s = jnp.dot(qs[hd].astype(BF16), k_t.astype(BF16), preferred_element_type=F32) * p
                s_bf.append(s.astype(BF16))
                k_tw.append((k_t * p[L - 1:L, :]).astype(BF16))
            nds = []
            for hd in range(N_HEADS):
                cn = cn_ref[hd]
                if c == 0:
                    cn = jnp.where(first, 0.0, cn)
                lhs = jnp.concatenate([(a_c[:, hd:hd + 1] * qs[hd]).astype(BF16), s_bf[hd]], axis=1)
                rhs = jnp.concatenate([cn.astype(BF16), vxs[hd]], axis=0)
                nds.append(jnp.dot(lhs, rhs, preferred_element_type=F32))
                cn_ref[hd] = (a[hd:hd + 1, L - 1:L] * cn
                              + jnp.dot(k_tw[hd], vxs[hd], preferred_element_type=F32))
            for hd, hs in enumerate(heads):
                nd = nds[hd]
                hh = nd[:, 0:HEAD_D] / jnp.maximum(jnp.abs(nd[:, HEAD_D:HEAD_D + 1]), e_c[:, hd:hd + 1])
                hg = _sigmoid(proj_ref[r0:r0 + L, P_O + hd * HEAD_D:P_O + (hd + 1) * HEAD_D]) * hh
                hg = hg * lax.rsqrt(jnp.mean(hg * hg, axis=-1, keepdims=True) + EPS)
                hg = hg * mng_ref[:, hs]
                ya = hg * _silu(proj_ref[r0:r0 + L, P_ZA + hd * HEAD_D:P_ZA + (hd + 1) * HEAD_D])
                y_ref[r0:r0 + L, hs] = ya.astype(BF16)
        return run

    def pool():
        tg = (t_seq * T + lax.broadcasted_iota(jnp.int32, (T, LANES), 0) + 1).astype(F32)
        zero_w = jnp.zeros((LANES, LANES), BF16)
        for g0 in range(0, len(POOL_WINDOWS), 2):
            pooled = []
            for g in (g0, g0 + 1):
                win = POOL_WINDOWS[g]
                gs = slice(g * LANES, (g + 1) * LANES)
                ssum = u_ext[:, gs]
                sh = 1
                while sh < win:
                    ssum = ssum + pltpu.roll(ssum, sh, axis=0)
                    sh *= 2
                u = u_ext[POOL_HIST:POOL_HIST + T, gs]
                pooled.append((ssum[POOL_HIST:POOL_HIST + T] / jnp.minimum(tg, float(win)) - u).astype(BF16))
            w_a, w_b = pw_ref[g0].astype(BF16), pw_ref[g0 + 1].astype(BF16)
            w_pair = jnp.concatenate([jnp.concatenate([w_a, zero_w], axis=1),
                                      jnp.concatenate([zero_w, w_b], axis=1)], axis=0)
            yb = jnp.dot(jnp.concatenate(pooled, axis=1), w_pair, preferred_element_type=F32)
            yb = (yb * ps_ref[:, g0 * LANES:(g0 + 2) * LANES]
                  * _silu(proj_ref[:, P_ZB + g0 * LANES:P_ZB + (g0 + 2) * LANES]))
            y_ref[:, GROUP_W + g0 * LANES:GROUP_W + (g0 + 2) * LANES] = yb.astype(BF16)

    def attention():
        for hd in range(N_HEADS):
            hs = slice(hd * HEAD_D, (hd + 1) * HEAD_D)
            qc = proj_ref[:, P_QC + hd * HEAD_D:P_QC + (hd + 1) * HEAD_D].astype(BF16)
            s = jnp.dot(qc, kmt_ref[hs, :], preferred_element_type=F32) * (HEAD_D ** -0.5)
            pexp = jnp.exp(s - jnp.max(s, axis=-1, keepdims=True))
            pv = jnp.dot(pexp.astype(BF16), vmx_ref[hd], preferred_element_type=F32)
            yc = pv[:, 0:HEAD_D] / pv[:, HEAD_D:HEAD_D + 1]
            yc = yc * _silu(proj_ref[:, P_ZC + hd * HEAD_D:P_ZC + (hd + 1) * HEAD_D])
            y_ref[:, 2 * GROUP_W + hd * HEAD_D:2 * GROUP_W + (hd + 1) * HEAD_D] = yc.astype(BF16)

    def out_proj():
        rows = slice(x_row, x_row + T)
        for i in range(D_MODEL // GROUP_W):
            cs = slice(i * GROUP_W, (i + 1) * GROUP_W)
            o_ref[rows, cs] = x_ref[rows, cs] + jnp.dot(y_ref[...], wo_ref[i], preferred_element_type=F32)
        if final:
            o_ref[rows, :] = _rmsnorm(o_ref[rows, :], fng_ref[...])

    return dict(conv_q=conv_q, conv_k=conv_k, chunks=[mlstm_chunk(c) for c in range(T // L)],
                pool=pool, attention=attention, out_proj=out_proj)


def _layer_kernel(xp_ref, xn_ref, kmt_ref, vmx_ref, ng_ref, w_ref, wg_ref, bg_ref, cw_ref, cb_ref, mng_ref,
                  pw_ref, ps_ref, wo_ref, fng_ref, o_ref,
                  h_ref, qk0, u0, proj0, g0, qk1, u1, proj1, g1, q_ref, k_ref, vext_ref, cn_ref, y_ref, m_ref,
                  *, final, tiles_per_seq):
    T = SEQ_TILE
    k = pl.program_id(0)
    set0, set1 = (qk0, u0, proj0, g0), (qk1, u1, proj1, g1)
    a_args = (ng_ref, w_ref, wg_ref, h_ref)
    b_args = (kmt_ref, vmx_ref, bg_ref, cw_ref, cb_ref, mng_ref, pw_ref, ps_ref, wo_ref, fng_ref,
              q_ref, k_ref, vext_ref, cn_ref, m_ref, y_ref, final)

    @pl.when(k == 0)
    def _prologue():
        col = lax.broadcasted_iota(jnp.int32, vext_ref.shape, 1)
        vext_ref[...] = jnp.where(col % (2 * HEAD_D) == HEAD_D, 1.0, 0.0).astype(BF16)
        cn_ref[...] = jnp.zeros(cn_ref.shape, F32)
        m_ref[...] = jnp.zeros(m_ref.shape, F32)
        qk1[T:T + CONV_HIST, :] = jnp.zeros((CONV_HIST, 2 * GROUP_W), F32)
        u1[T:T + POOL_HIST, :] = jnp.zeros((POOL_HIST, GROUP_W), F32)
        for piece in _stage_a(xp_ref, 0, True, set1, set0, *a_args):
            piece()

    t_even = (2 * k) % tiles_per_seq
    first_even = t_even == 0
    first_next = (2 * k + 2) % tiles_per_seq == 0

    def half_step(x_row, first, t_seq, cur, nxt, xa_ref, xa_row, first_a):
        for piece in _stage_a(xa_ref, xa_row, first_a, cur, nxt, *a_args):
            piece()
        b = _stage_b(xp_ref, x_row, first, t_seq, cur, o_ref, *b_args)
        for phase in (b["conv_q"], b["conv_k"], *b["chunks"], b["pool"], b["attention"], b["out_proj"]):
            phase()

    half_step(0, first_even, t_even, set0, set1, xp_ref, T, False)
    half_step(T, False, t_even + 1, set1, set0, xn_ref, 0, first_next)


def _setup_kernel(wint_ref, wout_ref, mem_ref, g_ref, wkv_ref, wall_ref, wg_ref, wo_ref, kmt_ref, vmx_ref,
                  wb_ref):
    wall_ref[:, 0:GATE0] = wint_ref[0:GATE0, :].T.astype(BF16)
    wall_ref[:, GATE0:N_PROJ] = wint_ref[GATE0 + 2 * N_HEADS:N_IN, :].T.astype(BF16)
    wg_ref[0:2 * N_HEADS, :] = wint_ref[GATE0:GATE0 + 2 * N_HEADS, :].astype(BF16)
    wg_ref[2 * N_HEADS:GATE_ROWS, :] = jnp.zeros((GATE_ROWS - 2 * N_HEADS, wg_ref.shape[1]), BF16)
    for i in range(D_MODEL // GROUP_W):
        wo_ref[i] = wout_ref[:, i * GROUP_W:(i + 1) * GROUP_W].astype(BF16)

    @pl.when(pl.program_id(0) == 0)
    def _cast_weights():
        wb_ref[...] = wkv_ref[...].astype(BF16)

    mem = mem_ref[...]
    ms = jnp.mean(mem * mem, axis=-1, keepdims=True)
    nrm = mem * lax.rsqrt(ms + EPS)
    col = lax.broadcasted_iota(jnp.int32, (N_MEM, HEAD_D), 1)
    ones_col = jnp.where(col == 0, 1.0, 0.0).astype(BF16)
    for l in range(wb_ref.shape[0]):
        mn = (nrm * g_ref[l]).astype(BF16)
        kv = jnp.dot(mn, wb_ref[l], preferred_element_type=F32)
        for hd in range(N_HEADS):
            hs = slice(hd * HEAD_D, (hd + 1) * HEAD_D)
            kmt_ref[l, hs, :] = kv[:, hs].T.astype(BF16)
            vmx_ref[l, hd, :, 0:HEAD_D] = kv[:, GROUP_W + hd * HEAD_D:GROUP_W + (hd + 1) * HEAD_D].astype(BF16)
            vmx_ref[l, hd, :, HEAD_D:2 * HEAD_D] = ones_col


def _resident(shape, layer):
    nd = len(shape)
    return pl.BlockSpec((None,) + shape, lambda k: (layer,) + (0,) * nd, pipeline_mode=pl.Buffered(1))


def _setup(w_in, w_out, mem, mem_norm_g, w_kv):
    depth, batch = w_in.shape[0], mem.shape[0]
    assert batch % depth == 0
    steps = batch // depth
    rb_in, rb_out = D_MODEL // steps, 3 * GROUP_W // steps
    return pl.pallas_call(
        _setup_kernel,
        grid=(batch,),
        in_specs=[
            pl.BlockSpec((None, N_IN, rb_in), lambda i: (i // steps, 0, i % steps)),
            pl.BlockSpec((None, rb_out, D_MODEL), lambda i: (i // steps, i % steps, 0)),
            pl.BlockSpec((None, N_MEM, D_MODEL), lambda i: (i, 0, 0)),
            pl.BlockSpec((depth, 1, D_MODEL), lambda i: (0, 0, 0)),
            pl.BlockSpec((depth, D_MODEL, 2 * GROUP_W), lambda i: (0, 0, 0), pipeline_mode=pl.Buffered(1)),
        ],
        out_specs=[
            pl.BlockSpec((None, rb_in, N_PROJ), lambda i: (i // steps, i % steps, 0)),
            pl.BlockSpec((None, GATE_ROWS, rb_in), lambda i: (i // steps, 0, i % steps)),
            pl.BlockSpec((None, D_MODEL // GROUP_W, rb_out, GROUP_W), lambda i: (i // steps, 0, i % steps, 0)),
            pl.BlockSpec((depth, None, GROUP_W, N_MEM), lambda i: (0, i, 0, 0)),
            pl.BlockSpec((depth, None, N_HEADS, N_MEM, 2 * HEAD_D), lambda i: (0, i, 0, 0, 0)),
        ],
        out_shape=[
            jax.ShapeDtypeStruct((depth, D_MODEL, N_PROJ), BF16),
            jax.ShapeDtypeStruct((depth, GATE_ROWS, D_MODEL), BF16),
            jax.ShapeDtypeStruct((depth, D_MODEL // GROUP_W, 3 * GROUP_W, GROUP_W), BF16),
            jax.ShapeDtypeStruct((depth, batch, GROUP_W, N_MEM), BF16),
            jax.ShapeDtypeStruct((depth, batch, N_HEADS, N_MEM, 2 * HEAD_D), BF16),
        ],
        scratch_shapes=[pltpu.VMEM((depth, D_MODEL, 2 * GROUP_W), BF16)],
        compiler_params=pltpu.CompilerParams(
            dimension_semantics=("arbitrary",), vmem_limit_bytes=VMEM_LIMIT_BYTES),
        name="setup_weights_memkv",
    )(jnp.swapaxes(w_in, 1, 2), w_out, mem, mem_norm_g.reshape(depth, 1, D_MODEL), w_kv)


def _layer(layer, final, x, kmt, vmx, ng, w_all, w_g, b_g, conv_w, conv_b, mng, pool_w, pool_scale, w_out, fng):
    batch, seq, _ = x.shape
    T = SEQ_TILE
    assert seq % (2 * T) == 0 and T % CHUNK == 0
    tiles_per_seq = seq // T
    pairs_per_seq = tiles_per_seq // 2
    last_tile = batch * tiles_per_seq - 1

    def next_tile(k):
        nt = jnp.minimum(2 * k + 2, last_tile)
        return nt // tiles_per_seq, nt % tiles_per_seq, 0

    kern = functools.partial(_layer_kernel, final=final, tiles_per_seq=tiles_per_seq)
    set_scratch = [
        pltpu.VMEM((CONV_HIST + T, 2 * GROUP_W), F32),
        pltpu.VMEM((POOL_HIST + T, GROUP_W), F32),
        pltpu.VMEM((T, P_COLS), F32),
        pltpu.VMEM((GATE_ROWS, T), F32),
    ]
    return pl.pallas_call(
        kern,
        grid=(batch * pairs_per_seq,),
        in_specs=[
            pl.BlockSpec((None, 2 * T, D_MODEL), lambda k: (k // pairs_per_seq, k % pairs_per_seq, 0)),
            pl.BlockSpec((None, T, D_MODEL), next_tile),
            pl.BlockSpec((None, None, GROUP_W, N_MEM), lambda k: (layer, k // pairs_per_seq, 0, 0)),
            pl.BlockSpec((None, None, N_HEADS, N_MEM, 2 * HEAD_D),
                         lambda k: (layer, k // pairs_per_seq, 0, 0, 0)),
            _resident((1, D_MODEL), layer),
            _resident((D_MODEL, N_PROJ), layer),
            _resident((GATE_ROWS, D_MODEL), layer),
            _resident((2 * N_HEADS, LANES), layer),
            _resident((CONV_K, 2 * GROUP_W), layer),
            _resident((1, 2 * GROUP_W), layer),
            _resident((1, GROUP_W), layer),
            _resident((len(POOL_WINDOWS), LANES, LANES), layer),
            _resident((1, GROUP_W), layer),
            _resident((D_MODEL // GROUP_W, 3 * GROUP_W, GROUP_W), layer),
            pl.BlockSpec((1, D_MODEL), lambda k: (0, 0), pipeline_mode=pl.Buffered(1)),
        ],
        out_specs=pl.BlockSpec((None, 2 * T, D_MODEL), lambda k: (k // pairs_per_seq, k % pairs_per_seq, 0)),
        out_shape=jax.ShapeDtypeStruct(x.shape, F32),
        scratch_shapes=[pltpu.VMEM((T, D_MODEL), BF16)] + set_scratch + set_scratch + [
            pltpu.VMEM((T, GROUP_W), F32),
            pltpu.VMEM((T, GROUP_W), F32),
            pltpu.VMEM((T, N_HEADS * 2 * HEAD_D), BF16),
            pltpu.VMEM((N_HEADS, HEAD_D, 2 * HEAD_D), F32),
            pltpu.VMEM((T, 3 * GROUP_W), BF16),
            pltpu.VMEM((2 * N_HEADS, LANES), F32),
        ],
        compiler_params=pltpu.CompilerParams(
            dimension_semantics=("arbitrary",), vmem_limit_bytes=VMEM_LIMIT_BYTES),
        name="trunk_layer_final" if final else "trunk_layer",
    )(x, x, kmt, vmx, ng, w_all, w_g, b_g, conv_w, conv_b, mng, pool_w, pool_scale, w_out, fng)


def kernel(x, mem, norm_g, w_in, b_gates, conv_w, conv_b, mlstm_norm_g, pool_w, pool_scale,
           mem_norm_g, w_mem_kv, w_out, final_norm_g):
    depth = w_in.shape[0]
    w_all, w_g, w_out_b, kmt, vmx = _setup(w_in, w_out, mem, mem_norm_g, w_mem_kv)
    b_g = jnp.broadcast_to(b_gates[:, :, None], (depth, 2 * N_HEADS, LANES))
    args = (kmt, vmx, norm_g.reshape(depth, 1, D_MODEL), w_all, w_g, b_g,
            conv_w, conv_b.reshape(depth, 1, 2 * GROUP_W), mlstm_norm_g.reshape(depth, 1, GROUP_W),
            pool_w, pool_scale.reshape(depth, 1, GROUP_W), w_out_b, final_norm_g.reshape(1, D_MODEL))
    for l in range(depth):
        x = _layer(l, l == depth - 1, x, *args)
    return x
```

```python
import functools

import jax
import jax.numpy as jnp
from jax import lax
from jax.experimental import pallas as pl
from jax.experimental.pallas import tpu as pltpu

D_MODEL = 1024
GROUP_W = 512
N_HEADS = 4
HEAD_D = 128
CONV_K = 4
POOL_WINDOWS = (2, 4, 8, 16)
N_MEM = 256
EPS = 1e-6

SEQ_TILE = 256
CHUNK = 256
CONV_HIST = 8
POOL_HIST = 16
LANES = 128
VMEM_LIMIT_BYTES = 52 * 1024 * 1024

N_IN = 2 * GROUP_W + 3 * GROUP_W + 2 * N_HEADS + 4 * GROUP_W
GATE0 = 5 * GROUP_W
Q0, K0, V0, O0, ZA0, U0, ZB0, QC0, ZC0 = 0, 512, 1024, 1536, 2048, 2560, 3072, 3584, 4096
N_PROJ = 4608
GATE_ROWS = 16
P_O, P_ZA, P_ZB, P_ZC = 0, 512, 1024, 1536
P_COLS = 2048

BF16 = jnp.bfloat16
F32 = jnp.float32


def _sigmoid(x):
    return 0.5 * jnp.tanh(0.5 * x) + 0.5


def _silu(x):
    hx = 0.5 * x
    return hx + hx * jnp.tanh(hx)


def _log_sigmoid(x):
    return jnp.minimum(x, 0.0) - jnp.log1p(jnp.exp(-jnp.abs(x)))


def _scan_lanes(x, op, identity):
    n = x.shape[1]
    lanes = lax.broadcasted_iota(jnp.int32, x.shape, 1)
    k = 1
    while k < n:
        x = op(x, jnp.where(lanes >= k, pltpu.roll(x, k, axis=1), identity))
        k *= 2
    return x


def _rmsnorm(x, g):
    ms = jnp.mean(x * x, axis=-1, keepdims=True)
    return (x * lax.rsqrt(ms + EPS)) * g


def _stage_a(x_ref, x_row, first, prev, cur, ng_ref, w_ref, wg_ref, h_ref):
    T = SEQ_TILE
    qk_prev, u_prev = prev[0], prev[1]
    qk_cur, u_cur, proj_cur, g_cur, vext_cur, qc_cur = cur

    def head():
        qk_cur[0:CONV_HIST, :] = jnp.where(first, 0.0, qk_prev[T:T + CONV_HIST, :])
        u_cur[0:POOL_HIST, :] = jnp.where(first, 0.0, u_prev[T:T + POOL_HIST, :])
        h_ref[...] = _rmsnorm(x_ref[x_row:x_row + T, :], ng_ref[...]).astype(BF16)

    def proj(c0):
        return jnp.dot(h_ref[...], w_ref[:, c0:c0 + GROUP_W], preferred_element_type=F32)

    def dot_q():
        qk_cur[CONV_HIST:CONV_HIST + T, 0:GROUP_W] = proj(Q0)

    def dot_k():
        qk_cur[CONV_HIST:CONV_HIST + T, GROUP_W:2 * GROUP_W] = proj(K0)

    def dot_u():
        u_cur[POOL_HIST:POOL_HIST + T, :] = proj(U0)

    def dot_to(src, dst):
        def run():
            proj_cur[:, dst:dst + GROUP_W] = proj(src)
        return run

    def dot_v():
        res = proj(V0)
        for hd in range(N_HEADS):
            vext_cur[:, hd * 2 * HEAD_D:hd * 2 * HEAD_D + HEAD_D] = (
                res[:, hd * HEAD_D:(hd + 1) * HEAD_D].astype(BF16))

    def dot_qc():
        qc_cur[...] = proj(QC0).astype(BF16)

    def dot_gates():
        g_cur[...] = lax.dot_general(wg_ref[...], h_ref[...], (((1,), (1,)), ((), ())),
                                     preferred_element_type=F32)

    return [head, dot_q, dot_k, dot_u, dot_v, dot_to(O0, P_O), dot_to(ZA0, P_ZA),
            dot_to(ZB0, P_ZB), dot_qc, dot_to(ZC0, P_ZC), dot_gates]


def _stage_b(x_ref, x_row, first, t_seq, cur, o_ref, kmt_ref, vmx_ref, bg_ref, cw_ref, cb_ref, mng_ref,
             pw_ref, ps_ref, wo_ref, fng_ref, q_ref, k_ref, cn_ref, m_ref, y_ref, final):
    T, L = SEQ_TILE, CHUNK
    qk_ext, u_ext, proj_ref, g_ref, vext_ref, qc_ref = cur

    def conv(c0):
        xs = qk_ext[:, c0:c0 + GROUP_W]
        acc = cb_ref[:, c0:c0 + GROUP_W] + cw_ref[CONV_K - 1:CONV_K, c0:c0 + GROUP_W] * xs[CONV_HIST:]
        for j in range(CONV_K - 1):
            back = pltpu.roll(xs, CONV_K - 1 - j, axis=0)[CONV_HIST:]
            acc = acc + cw_ref[j:j + 1, c0:c0 + GROUP_W] * back
        return _silu(acc)

    def conv_q():
        q_ref[...] = conv(0)

    def conv_k():
        k_ref[...] = conv(GROUP_W) * (HEAD_D ** -0.5)

    def mlstm_chunk(c):
        def run():
            rowi = lax.broadcasted_iota(jnp.int32, (L, L), 0)
            coli = lax.broadcasted_iota(jnp.int32, (L, L), 1)
            causal = coli <= rowi
            r0 = c * L
            g8 = g_ref[0:2 * N_HEADS, r0:r0 + L] + bg_ref[:, 0:1]
            gi = g8
            logf = _log_sigmoid(pltpu.roll(g8, N_HEADS, axis=0))
            b = _scan_lanes(logf, jnp.add, 0.0)
            r = gi - b
            m_prev = m_ref[:, 0:1]
            if c == 0:
                m_prev = jnp.where(first, 0.0, m_prev)
            mm = jnp.maximum(_scan_lanes(r, jnp.maximum, -jnp.inf), m_prev)
            a = jnp.exp(m_prev - mm)
            e = jnp.exp(-(mm + b))
            m_ref[...] = jnp.broadcast_to((mm + b)[:, L - 1:L], m_ref.shape)
            cols = jnp.concatenate([mm, a, e, jnp.zeros((LANES - 6 * N_HEADS, L), F32)], axis=0).T
            mm_c, a_c, e_c = (cols[:, i * 2 * N_HEADS:(i + 1) * 2 * N_HEADS] for i in range(3))
            heads = [slice(hd * HEAD_D, (hd + 1) * HEAD_D) for hd in range(N_HEADS)]
            qs = [q_ref[r0:r0 + L, hs] for hs in heads]
            vxs = [vext_ref[r0:r0 + L, hd * 2 * HEAD_D:(hd + 1) * 2 * HEAD_D] for hd in range(N_HEADS)]
            s_bf, k_tw = [], []
            for hd, hs in enumerate(heads):
                arg = r[hd:hd + 1, :] - mm_c[:, hd:hd + 1]
                p = jnp.exp(jnp.where(causal, arg, -1e30))
                k_t = k_ref[r0:r0 + L, hs].T
                s = jnp.dot(qs[hd].astype(BF16), k_t.astype(BF16), preferred_element_type=F32) * p
                s_bf.append(s.astype(BF16))
                k_tw.append((k_t * p[L - 1:L, :]).astype(BF16))
            nds = []
            for hd in range(N_HEADS):
                cn = cn_ref[hd]
                if c == 0:
                    cn = jnp.where(first, 0.0, cn)
                lhs = jnp.concatenate([(a_c[:, hd:hd + 1] * qs[hd]).astype(BF16), s_bf[hd]], axis=1)
                rhs = jnp.concatenate([cn.astype(BF16), vxs[hd]], axis=0)
                nds.append(jnp.dot(lhs, rhs, preferred_element_type=F32))
                cn_ref[hd] = (a[hd:hd + 1, L - 1:L] * cn
                              + jnp.dot(k_tw[hd], vxs[hd], preferred_element_type=F32))
            for hd, hs in enumerate(heads):
                nd = nds[hd]
                hh = nd[:, 0:HEAD_D] / jnp.maximum(jnp.abs(nd[:, HEAD_D:HEAD_D + 1]), e_c[:, hd:hd + 1])
                hg = _sigmoid(proj_ref[r0:r0 + L, P_O + hd * HEAD_D:P_O + (hd + 1) * HEAD_D]) * hh
                hg = hg * lax.rsqrt(jnp.mean(hg * hg, axis=-1, keepdims=True) + EPS)
                hg = hg * mng_ref[:, hs]
                ya = hg * _silu(proj_ref[r0:r0 + L, P_ZA + hd * HEAD_D:P_ZA + (hd + 1) * HEAD_D])
                y_ref[r0:r0 + L, hs] = ya.astype(BF16)
        return run

    def pool():
        tg = (t_seq * T + lax.broadcasted_iota(jnp.int32, (T, LANES), 0) + 1).astype(F32)
        zero_w = jnp.zeros((LANES, LANES), BF16)
        for g0 in range(0, len(POOL_WINDOWS), 2):
            pooled = []
            for g in (g0, g0 + 1):
                win = POOL_WINDOWS[g]
                gs = slice(g * LANES, (g + 1) * LANES)
                ssum = u_ext[:, gs]
                sh = 1
                while sh < win:
                    ssum = ssum + pltpu.roll(ssum, sh, axis=0)
                    sh *= 2
                u = u_ext[POOL_HIST:POOL_HIST + T, gs]
                pooled.append((ssum[POOL_HIST:POOL_HIST + T] / jnp.minimum(tg, float(win)) - u).astype(BF16))
            w_a, w_b = pw_ref[g0].astype(BF16), pw_ref[g0 + 1].astype(BF16)
            w_pair = jnp.concatenate([jnp.concatenate([w_a, zero_w], axis=1),
                                      jnp.concatenate([zero_w, w_b], axis=1)], axis=0)
            yb = jnp.dot(jnp.concatenate(pooled, axis=1), w_pair, preferred_element_type=F32)
            yb = (yb * ps_ref[:, g0 * LANES:(g0 + 2) * LANES]
                  * _silu(proj_ref[:, P_ZB + g0 * LANES:P_ZB + (g0 + 2) * LANES]))
            y_ref[:, GROUP_W + g0 * LANES:GROUP_W + (g0 + 2) * LANES] = yb.astype(BF16)

    def attention():
        for hd in range(N_HEADS):
            hs = slice(hd * HEAD_D, (hd + 1) * HEAD_D)
            s = jnp.dot(qc_ref[:, hs], kmt_ref[hs, :], preferred_element_type=F32) * (HEAD_D ** -0.5)
            pexp = jnp.exp(s - jnp.max(s, axis=-1, keepdims=True))
            pv = jnp.dot(pexp.astype(BF16), vmx_ref[hd], preferred_element_type=F32)
            yc = pv[:, 0:HEAD_D] / pv[:, HEAD_D:HEAD_D + 1]
            yc = yc * _silu(proj_ref[:, P_ZC + hd * HEAD_D:P_ZC + (hd + 1) * HEAD_D])
            y_ref[:, 2 * GROUP_W + hd * HEAD_D:2 * GROUP_W + (hd + 1) * HEAD_D] = yc.astype(BF16)

    def out_proj():
        rows = slice(x_row, x_row + T)
        for i in range(D_MODEL // GROUP_W):
            cs = slice(i * GROUP_W, (i + 1) * GROUP_W)
            o_ref[rows, cs] = x_ref[rows, cs] + jnp.dot(y_ref[...], wo_ref[i], preferred_element_type=F32)
        if final:
            o_ref[rows, :] = _rmsnorm(o_ref[rows, :], fng_ref[...])

    return dict(conv_q=conv_q, conv_k=conv_k, chunks=[mlstm_chunk(c) for c in range(T // L)],
                pool=pool, attention=attention, out_proj=out_proj)


def _layer_kernel(xp_ref, xn_ref, kmt_ref, vmx_ref, ng_ref, w_ref, wg_ref, bg_ref, cw_ref, cb_ref, mng_ref,
                  pw_ref, ps_ref, wo_ref, fng_ref, o_ref,
                  h_ref, qk0, u0, proj0, g0, vext0, qc0, qk1, u1, proj1, g1, vext1, qc1,
                  q_ref, k_ref, cn_ref, y_ref, m_ref, *, final, tiles_per_seq):
    T = SEQ_TILE
    k = pl.program_id(0)
    set0, set1 = (qk0, u0, proj0, g0, vext0, qc0), (qk1, u1, proj1, g1, vext1, qc1)
    a_args = (ng_ref, w_ref, wg_ref, h_ref)
    b_args = (kmt_ref, vmx_ref, bg_ref, cw_ref, cb_ref, mng_ref, pw_ref, ps_ref, wo_ref, fng_ref,
              q_ref, k_ref, cn_ref, m_ref, y_ref, final)

    @pl.when(k == 0)
    def _prologue():
        col = lax.broadcasted_iota(jnp.int32, vext0.shape, 1)
        for vext in (vext0, vext1):
            vext[...] = jnp.where(col % (2 * HEAD_D) == HEAD_D, 1.0, 0.0).astype(BF16)
        cn_ref[...] = jnp.zeros(cn_ref.shape, F32)
        m_ref[...] = jnp.zeros(m_ref.shape, F32)
        qk1[T:T + CONV_HIST, :] = jnp.zeros((CONV_HIST, 2 * GROUP_W), F32)
        u1[T:T + POOL_HIST, :] = jnp.zeros((POOL_HIST, GROUP_W), F32)
        for piece in _stage_a(xp_ref, 0, True, set1, set0, *a_args):
            piece()

    t_even = (2 * k) % tiles_per_seq
    first_even = t_even == 0
    first_next = (2 * k + 2) % tiles_per_seq == 0

    def half_step(x_row, first, t_seq, cur, nxt, xa_ref, xa_row, first_a):
        for piece in _stage_a(xa_ref, xa_row, first_a, cur, nxt, *a_args):
            piece()
        b = _stage_b(xp_ref, x_row, first, t_seq, cur, o_ref, *b_args)
        for phase in (b["conv_q"], b["conv_k"], *b["chunks"], b["pool"], b["attention"], b["out_proj"]):
            phase()

    half_step(0, first_even, t_even, set0, set1, xp_ref, T, False)
    half_step(T, False, t_even + 1, set1, set0, xn_ref, 0, first_next)


def _setup_kernel(wint_ref, wout_ref, mem_ref, g_ref, wkv_ref, wall_ref, wg_ref, wo_ref, kmt_ref, vmx_ref,
                  wb_ref):
    wall_ref[:, 0:GATE0] = wint_ref[0:GATE0, :].T.astype(BF16)
    wall_ref[:, GATE0:N_PROJ] = wint_ref[GATE0 + 2 * N_HEADS:N_IN, :].T.astype(BF16)
    wg_ref[0:2 * N_HEADS, :] = wint_ref[GATE0:GATE0 + 2 * N_HEADS, :].astype(BF16)
    wg_ref[2 * N_HEADS:GATE_ROWS, :] = jnp.zeros((GATE_ROWS - 2 * N_HEADS, wg_ref.shape[1]), BF16)
    for i in range(D_MODEL // GROUP_W):
        wo_ref[i] = wout_ref[:, i * GROUP_W:(i + 1) * GROUP_W].astype(BF16)

    @pl.when(pl.program_id(0) == 0)
    def _cast_weights():
        wb_ref[...] = wkv_ref[...].astype(BF16)

    mem = mem_ref[...]
    ms = jnp.mean(mem * mem, axis=-1, keepdims=True)
    nrm = mem * lax.rsqrt(ms + EPS)
    col = lax.broadcasted_iota(jnp.int32, (N_MEM, HEAD_D), 1)
    ones_col = jnp.where(col == 0, 1.0, 0.0).astype(BF16)
    for l in range(wb_ref.shape[0]):
        mn = (nrm * g_ref[l]).astype(BF16)
        kv = jnp.dot(mn, wb_ref[l], preferred_element_type=F32)
        for hd in range(N_HEADS):
            hs = slice(hd * HEAD_D, (hd + 1) * HEAD_D)
            kmt_ref[l, hs, :] = kv[:, hs].T.astype(BF16)
            vmx_ref[l, hd, :, 0:HEAD_D] = kv[:, GROUP_W + hd * HEAD_D:GROUP_W + (hd + 1) * HEAD_D].astype(BF16)
            vmx_ref[l, hd, :, HEAD_D:2 * HEAD_D] = ones_col


def _resident(shape, layer):
    nd = len(shape)
    return pl.BlockSpec((None,) + shape, lambda k: (layer,) + (0,) * nd, pipeline_mode=pl.Buffered(1))


def _setup(w_in, w_out, mem, mem_norm_g, w_kv):
    depth, batch = w_in.shape[0], mem.shape[0]
    assert batch % depth == 0
    steps = batch // depth
    rb_in, rb_out = D_MODEL // steps, 3 * GROUP_W // steps
    return pl.pallas_call(
        _setup_kernel,
        grid=(batch,),
        in_specs=[
            pl.BlockSpec((None, N_IN, rb_in), lambda i: (i // steps, 0, i % steps)),
            pl.BlockSpec((None, rb_out, D_MODEL), lambda i: (i // steps, i % steps, 0)),
            pl.BlockSpec((None, N_MEM, D_MODEL), lambda i: (i, 0, 0)),
            pl.BlockSpec((depth, 1, D_MODEL), lambda i: (0, 0, 0)),
            pl.BlockSpec((depth, D_MODEL, 2 * GROUP_W), lambda i: (0, 0, 0), pipeline_mode=pl.Buffered(1)),
        ],
        out_specs=[
            pl.BlockSpec((None, rb_in, N_PROJ), lambda i: (i // steps, i % steps, 0)),
            pl.BlockSpec((None, GATE_ROWS, rb_in), lambda i: (i // steps, 0, i % steps)),
            pl.BlockSpec((None, D_MODEL // GROUP_W, rb_out, GROUP_W), lambda i: (i // steps, 0, i % steps, 0)),
            pl.BlockSpec((depth, None, GROUP_W, N_MEM), lambda i: (0, i, 0, 0)),
            pl.BlockSpec((depth, None, N_HEADS, N_MEM, 2 * HEAD_D), lambda i: (0, i, 0, 0, 0)),
        ],
        out_shape=[
            jax.ShapeDtypeStruct((depth, D_MODEL, N_PROJ), BF16),
            jax.ShapeDtypeStruct((depth, GATE_ROWS, D_MODEL), BF16),
            jax.ShapeDtypeStruct((depth, D_MODEL // GROUP_W, 3 * GROUP_W, GROUP_W), BF16),
            jax.ShapeDtypeStruct((depth, batch, GROUP_W, N_MEM), BF16),
            jax.ShapeDtypeStruct((depth, batch, N_HEADS, N_MEM, 2 * HEAD_D), BF16),
        ],
        scratch_shapes=[pltpu.VMEM((depth, D_MODEL, 2 * GROUP_W), BF16)],
        compiler_params=pltpu.CompilerParams(
            dimension_semantics=("arbitrary",), vmem_limit_bytes=VMEM_LIMIT_BYTES),
        name="setup_weights_memkv",
    )(jnp.swapaxes(w_in, 1, 2), w_out, mem, mem_norm_g.reshape(depth, 1, D_MODEL), w_kv)


def _layer(layer, final, x, kmt, vmx, ng, w_all, w_g, b_g, conv_w, conv_b, mng, pool_w, pool_scale, w_out, fng):
    batch, seq, _ = x.shape
    T = SEQ_TILE
    assert seq % (2 * T) == 0 and T % CHUNK == 0
    tiles_per_seq = seq // T
    pairs_per_seq = tiles_per_seq // 2
    last_tile = batch * tiles_per_seq - 1

    def next_tile(k):
        nt = jnp.minimum(2 * k + 2, last_tile)
        return nt // tiles_per_seq, nt % tiles_per_seq, 0

    kern = functools.partial(_layer_kernel, final=final, tiles_per_seq=tiles_per_seq)
    set_scratch = [
        pltpu.VMEM((CONV_HIST + T, 2 * GROUP_W), F32),
        pltpu.VMEM((POOL_HIST + T, GROUP_W), F32),
        pltpu.VMEM((T, P_COLS), F32),
        pltpu.VMEM((GATE_ROWS, T), F32),
        pltpu.VMEM((T, N_HEADS * 2 * HEAD_D), BF16),
        pltpu.VMEM((T, GROUP_W), BF16),
    ]
    return pl.pallas_call(
        kern,
        grid=(batch * pairs_per_seq,),
        in_specs=[
            pl.BlockSpec((None, 2 * T, D_MODEL), lambda k: (k // pairs_per_seq, k % pairs_per_seq, 0)),
            pl.BlockSpec((None, T, D_MODEL), next_tile),
            pl.BlockSpec((None, None, GROUP_W, N_MEM), lambda k: (layer, k // pairs_per_seq, 0, 0)),
            pl.BlockSpec((None, None, N_HEADS, N_MEM, 2 * HEAD_D),
                         lambda k: (layer, k // pairs_per_seq, 0, 0, 0)),
            _resident((1, D_MODEL), layer),
            _resident((D_MODEL, N_PROJ), layer),
            _resident((GATE_ROWS, D_MODEL), layer),
            _resident((2 * N_HEADS, LANES), layer),
            _resident((CONV_K, 2 * GROUP_W), layer),
            _resident((1, 2 * GROUP_W), layer),
            _resident((1, GROUP_W), layer),
            _resident((len(POOL_WINDOWS), LANES, LANES), layer),
            _resident((1, GROUP_W), layer),
            _resident((D_MODEL // GROUP_W, 3 * GROUP_W, GROUP_W), layer),
            pl.BlockSpec((1, D_MODEL), lambda k: (0, 0), pipeline_mode=pl.Buffered(1)),
        ],
        out_specs=pl.BlockSpec((None, 2 * T, D_MODEL), lambda k: (k // pairs_per_seq, k % pairs_per_seq, 0)),
        out_shape=jax.ShapeDtypeStruct(x.shape, F32),
        scratch_shapes=[pltpu.VMEM((T, D_MODEL), BF16)] + set_scratch + set_scratch + [
            pltpu.VMEM((T, GROUP_W), F32),
            pltpu.VMEM((T, GROUP_W), F32),
            pltpu.VMEM((N_HEADS, HEAD_D, 2 * HEAD_D), F32),
            pltpu.VMEM((T, 3 * GROUP_W), BF16),
            pltpu.VMEM((2 * N_HEADS, LANES), F32),
        ],
        compiler_params=pltpu.CompilerParams(
            dimension_semantics=("arbitrary",), vmem_limit_bytes=VMEM_LIMIT_BYTES),
        name="trunk_layer_final" if final else "trunk_layer",
    )(x, x, kmt, vmx, ng, w_all, w_g, b_g, conv_w, conv_b, mng, pool_w, pool_scale, w_out, fng)


def kernel(x, mem, norm_g, w_in, b_gates, conv_w, conv_b, mlstm_norm_g, pool_w, pool_scale,
           mem_norm_g, w_mem_kv, w_out, final_norm_g):
    depth = w_in.shape[0]
    w_all, w_g, w_out_b, kmt, vmx = _setup(w_in, w_out, mem, mem_norm_g, w_mem_kv)
    b_g = jnp.broadcast_to(b_gates[:, :, None], (depth, 2 * N_HEADS, LANES))
    args = (kmt, vmx, norm_g.reshape(depth, 1, D_MODEL), w_all, w_g, b_g,
            conv_w, conv_b.reshape(depth, 1, 2 * GROUP_W), mlstm_norm_g.reshape(depth, 1, GROUP_W),
            pool_w, pool_scale.reshape(depth, 1, GROUP_W), w_out_b, final_norm_g.reshape(1, D_MODEL))
    for l in range(depth):
        x = _layer(l, l == depth - 1, x, *args)
    return x
```

```python
import functools

import jax
import jax.numpy as jnp
from jax import lax
from jax.experimental import pallas as pl
from jax.experimental.pallas import tpu as pltpu

D_MODEL = 1024
GROUP_W = 512
N_HEADS = 4
HEAD_D = 128
CONV_K = 4
POOL_WINDOWS = (2, 4, 8, 16)
N_MEM = 256
EPS = 1e-6

SEQ_TILE = 256
CHUNK = 256
CONV_HIST = 8
POOL_HIST = 16
LANES = 128
VMEM_LIMIT_BYTES = 52 * 1024 * 1024

N_IN = 2 * GROUP_W + 3 * GROUP_W + 2 * N_HEADS + 4 * GROUP_W
GATE0 = 5 * GROUP_W
Q0, K0, V0, O0, ZA0, U0, ZB0, QC0, ZC0 = 0, 512, 1024, 1536, 2048, 2560, 3072, 3584, 4096
N_PROJ = 4608
GATE_ROWS = 16
P_O, P_ZA, P_ZB, P_ZC = 0, 512, 1024, 1536
P_COLS = 2048

BF16 = jnp.bfloat16
F32 = jnp.float32


def _sigmoid(x):
    return 0.5 * jnp.tanh(0.5 * x) + 0.5


def _silu(x):
    hx = 0.5 * x
    return hx + hx * jnp.tanh(hx)


def _log_sigmoid(x):
    return jnp.minimum(x, 0.0) - jnp.log1p(jnp.exp(-jnp.abs(x)))


def _scan_lanes(x, op, identity):
    n = x.shape[1]
    lanes = lax.broadcasted_iota(jnp.int32, x.shape, 1)
    k = 1
    while k < n:
        x = op(x, jnp.where(lanes >= k, pltpu.roll(x, k, axis=1), identity))
        k *= 2
    return x


def _rmsnorm(x, g):
    ms = jnp.mean(x * x, axis=-1, keepdims=True)
    return (x * lax.rsqrt(ms + EPS)) * g


def _stage_a(first, prev, cur, w_ref, wg_ref, h_ref):
    T = SEQ_TILE
    qk_prev, u_prev = prev[0], prev[1]
    qk_cur, u_cur, proj_cur, g_cur, vext_cur, qc_cur = cur

    def head():
        qk_cur[0:CONV_HIST, :] = jnp.where(first, 0.0, qk_prev[T:T + CONV_HIST, :])
        u_cur[0:POOL_HIST, :] = jnp.where(first, 0.0, u_prev[T:T + POOL_HIST, :])

    def proj(c0):
        return jnp.dot(h_ref[...], w_ref[:, c0:c0 + GROUP_W], preferred_element_type=F32)

    def dot_q():
        qk_cur[CONV_HIST:CONV_HIST + T, 0:GROUP_W] = proj(Q0)

    def dot_k():
        qk_cur[CONV_HIST:CONV_HIST + T, GROUP_W:2 * GROUP_W] = proj(K0)

    def dot_u():
        u_cur[POOL_HIST:POOL_HIST + T, :] = proj(U0)

    def dot_to(src, dst):
        def run():
            proj_cur[:, dst:dst + GROUP_W] = proj(src)
        return run

    def dot_v():
        res = proj(V0)
        for hd in range(N_HEADS):
            vext_cur[:, hd * 2 * HEAD_D:hd * 2 * HEAD_D + HEAD_D] = (
                res[:, hd * HEAD_D:(hd + 1) * HEAD_D].astype(BF16))

    def dot_qc():
        qc_cur[...] = proj(QC0).astype(BF16)

    def dot_gates():
        g_cur[...] = lax.dot_general(wg_ref[...], h_ref[...], (((1,), (1,)), ((), ())),
                                     preferred_element_type=F32)

    return [head, dot_q, dot_k, dot_u, dot_v, dot_to(O0, P_O), dot_to(ZA0, P_ZA),
            dot_to(ZB0, P_ZB), dot_qc, dot_to(ZC0, P_ZC), dot_gates]


def _stage_b(x_ref, x_row, first, t_seq, cur, o_ref, kmt_ref, vmx_ref, bg_ref, cw_ref, cb_ref, mng_ref,
             pw_ref, ps_ref, wo_ref, fng_ref, q_ref, k_ref, cn_ref, m_ref, y_ref, final):
    T, L = SEQ_TILE, CHUNK
    qk_ext, u_ext, proj_ref, g_ref, vext_ref, qc_ref = cur

    def conv(c0):
        xs = qk_ext[:, c0:c0 + GROUP_W]
        acc = cb_ref[:, c0:c0 + GROUP_W] + cw_ref[CONV_K - 1:CONV_K, c0:c0 + GROUP_W] * xs[CONV_HIST:]
        for j in range(CONV_K - 1):
            back = pltpu.roll(xs, CONV_K - 1 - j, axis=0)[CONV_HIST:]
            acc = acc + cw_ref[j:j + 1, c0:c0 + GROUP_W] * back
        return _silu(acc)

    def conv_q():
        q_ref[...] = conv(0)

    def conv_k():
        k_ref[...] = conv(GROUP_W) * (HEAD_D ** -0.5)

    def mlstm_chunk(c):
        def run():
            rowi = lax.broadcasted_iota(jnp.int32, (L, L), 0)
            coli = lax.broadcasted_iota(jnp.int32, (L, L), 1)
            causal = coli <= rowi
            r0 = c * L
            g8 = g_ref[0:2 * N_HEADS, r0:r0 + L] + bg_ref[:, 0:1]
            gi = g8
            logf = _log_sigmoid(pltpu.roll(g8, N_HEADS, axis=0))
            b = _scan_lanes(logf, jnp.add, 0.0)
            r = gi - b
            m_prev = m_ref[:, 0:1]
            if c == 0:
                m_prev = jnp.where(first, 0.0, m_prev)
            mm = jnp.maximum(_scan_lanes(r, jnp.maximum, -jnp.inf), m_prev)
            a = jnp.exp(m_prev - mm)
            e = jnp.exp(-(mm + b))
            m_ref[...] = jnp.broadcast_to((mm + b)[:, L - 1:L], m_ref.shape)
            cols = jnp.concatenate([mm, a, e, jnp.zeros((LANES - 6 * N_HEADS, L), F32)], axis=0).T
            mm_c, a_c, e_c = (cols[:, i * 2 * N_HEADS:(i + 1) * 2 * N_HEADS] for i in range(3))
            heads = [slice(hd * HEAD_D, (hd + 1) * HEAD_D) for hd in range(N_HEADS)]
            qs = [q_ref[r0:r0 + L, hs] for hs in heads]
            vxs = [vext_ref[r0:r0 + L, hd * 2 * HEAD_D:(hd + 1) * 2 * HEAD_D] for hd in range(N_HEADS)]
            s_bf, k_tw = [], []
            for hd, hs in enumerate(heads):
                arg = r[hd:hd + 1, :] - mm_c[:, hd:hd + 1]
                p = jnp.exp(jnp.where(causal, arg, -1e30))
                k_t = k_ref[r0:r0 + L, hs].T
                s = jnp.dot(qs[hd].astype(BF16), k_t.astype(BF16), preferred_element_type=F32) * p
                s_bf.append(s.astype(BF16))
                k_tw.append((k_t * p[L - 1:L, :]).astype(BF16))
            nds = []
            for hd in range(N_HEADS):
                cn = cn_ref[hd]
                if c == 0:
                    cn = jnp.where(first, 0.0, cn)
                lhs = jnp.concatenate([(a_c[:, hd:hd + 1] * qs[hd]).astype(BF16), s_bf[hd]], axis=1)
                rhs = jnp.concatenate([cn.astype(BF16), vxs[hd]], axis=0)
                nds.append(jnp.dot(lhs, rhs, preferred_element_type=F32))
                cn_ref[hd] = (a[hd:hd + 1, L - 1:L] * cn
                              + jnp.dot(k_tw[hd], vxs[hd], preferred_element_type=F32))
            for hd, hs in enumerate(heads):
                nd = nds[hd]
                hh = nd[:, 0:HEAD_D] / jnp.maximum(jnp.abs(nd[:, HEAD_D:HEAD_D + 1]), e_c[:, hd:hd + 1])
                hg = _sigmoid(proj_ref[r0:r0 + L, P_O + hd * HEAD_D:P_O + (hd + 1) * HEAD_D]) * hh
                hg = hg * lax.rsqrt(jnp.mean(hg * hg, axis=-1, keepdims=True) + EPS)
                hg = hg * mng_ref[:, hs]
                ya = hg * _silu(proj_ref[r0:r0 + L, P_ZA + hd * HEAD_D:P_ZA + (hd + 1) * HEAD_D])
                y_ref[r0:r0 + L, hs] = ya.astype(BF16)
        return run

    def pool():
        tg = (t_seq * T + lax.broadcasted_iota(jnp.int32, (T, LANES), 0) + 1).astype(F32)
        zero_w = jnp.zeros((LANES, LANES), BF16)
        for g0 in range(0, len(POOL_WINDOWS), 2):
            pooled = []
            for g in (g0, g0 + 1):
                win = POOL_WINDOWS[g]
                gs = slice(g * LANES, (g + 1) * LANES)
                ssum = u_ext[:, gs]
                sh = 1
                while sh < win:
                    ssum = ssum + pltpu.roll(ssum, sh, axis=0)
                    sh *= 2
                u = u_ext[POOL_HIST:POOL_HIST + T, gs]
                pooled.append((ssum[POOL_HIST:POOL_HIST + T] / jnp.minimum(tg, float(win)) - u).astype(BF16))
            w_a, w_b = pw_ref[g0].astype(BF16), pw_ref[g0 + 1].astype(BF16)
            w_pair = jnp.concatenate([jnp.concatenate([w_a, zero_w], axis=1),
                                      jnp.concatenate([zero_w, w_b], axis=1)], axis=0)
            yb = jnp.dot(jnp.concatenate(pooled, axis=1), w_pair, preferred_element_type=F32)
            yb = (yb * ps_ref[:, g0 * LANES:(g0 + 2) * LANES]
                  * _silu(proj_ref[:, P_ZB + g0 * LANES:P_ZB + (g0 + 2) * LANES]))
            y_ref[:, GROUP_W + g0 * LANES:GROUP_W + (g0 + 2) * LANES] = yb.astype(BF16)

    def attention():
        for hd in range(N_HEADS):
            hs = slice(hd * HEAD_D, (hd + 1) * HEAD_D)
            s = jnp.dot(qc_ref[:, hs], kmt_ref[hs, :], preferred_element_type=F32) * (HEAD_D ** -0.5)
            pexp = jnp.exp(s - jnp.max(s, axis=-1, keepdims=True))
            pv = jnp.dot(pexp.astype(BF16), vmx_ref[hd], preferred_element_type=F32)
            yc = pv[:, 0:HEAD_D] / pv[:, HEAD_D:HEAD_D + 1]
            yc = yc * _silu(proj_ref[:, P_ZC + hd * HEAD_D:P_ZC + (hd + 1) * HEAD_D])
            y_ref[:, 2 * GROUP_W + hd * HEAD_D:2 * GROUP_W + (hd + 1) * HEAD_D] = yc.astype(BF16)

    def out_proj():
        rows = slice(x_row, x_row + T)
        for i in range(D_MODEL // GROUP_W):
            cs = slice(i * GROUP_W, (i + 1) * GROUP_W)
            o_ref[rows, cs] = x_ref[rows, cs] + jnp.dot(y_ref[...], wo_ref[i], preferred_element_type=F32)
        if final:
            o_ref[rows, :] = _rmsnorm(o_ref[rows, :], fng_ref[...])

    return dict(conv_q=conv_q, conv_k=conv_k, chunks=[mlstm_chunk(c) for c in range(T // L)],
                pool=pool, attention=attention, out_proj=out_proj)


def _layer_kernel(xp_ref, xn_ref, kmt_ref, vmx_ref, ng_ref, w_ref, wg_ref, bg_ref, cw_ref, cb_ref, mng_ref,
                  pw_ref, ps_ref, wo_ref, fng_ref, o_ref,
                  h_even, h_odd, qk0, u0, proj0, g0, vext0, qc0, qk1, u1, proj1, g1, vext1, qc1,
                  q_ref, k_ref, cn_ref, y_ref, m_ref, *, final, tiles_per_seq):
    T = SEQ_TILE
    k = pl.program_id(0)
    set0, set1 = (qk0, u0, proj0, g0, vext0, qc0), (qk1, u1, proj1, g1, vext1, qc1)
    a_args = (w_ref, wg_ref)

    def norm_into(h_ref, x_ref, x_row):
        def run():
            h_ref[...] = _rmsnorm(x_ref[x_row:x_row + T, :], ng_ref[...]).astype(BF16)
        return run
    b_args = (kmt_ref, vmx_ref, bg_ref, cw_ref, cb_ref, mng_ref, pw_ref, ps_ref, wo_ref, fng_ref,
              q_ref, k_ref, cn_ref, m_ref, y_ref, final)

    @pl.when(k == 0)
    def _prologue():
        col = lax.broadcasted_iota(jnp.int32, vext0.shape, 1)
        for vext in (vext0, vext1):
            vext[...] = jnp.where(col % (2 * HEAD_D) == HEAD_D, 1.0, 0.0).astype(BF16)
        cn_ref[...] = jnp.zeros(cn_ref.shape, F32)
        m_ref[...] = jnp.zeros(m_ref.shape, F32)
        qk1[T:T + CONV_HIST, :] = jnp.zeros((CONV_HIST, 2 * GROUP_W), F32)
        u1[T:T + POOL_HIST, :] = jnp.zeros((POOL_HIST, GROUP_W), F32)
        norm_into(h_even, xp_ref, 0)()
        norm_into(h_odd, xp_ref, T)()
        for piece in _stage_a(True, set1, set0, *a_args, h_even):
            piece()

    t_even = (2 * k) % tiles_per_seq
    first_even = t_even == 0
    first_next = (2 * k + 2) % tiles_per_seq == 0

    def half_step(x_row, first, t_seq, cur, nxt, h_ref, first_a, norm_ahead):
        for piece in _stage_a(first_a, cur, nxt, *a_args, h_ref):
            piece()
        norm_ahead()
        b = _stage_b(xp_ref, x_row, first, t_seq, cur, o_ref, *b_args)
        for phase in (b["conv_q"], b["conv_k"], *b["chunks"], b["pool"], b["attention"], b["out_proj"]):
            phase()

    half_step(0, first_even, t_even, set0, set1, h_odd, False, norm_into(h_even, xn_ref, 0))
    half_step(T, False, t_even + 1, set1, set0, h_even, first_next, norm_into(h_odd, xn_ref, T))


def _setup_kernel(wint_ref, wout_ref, mem_ref, g_ref, wkv_ref, wall_ref, wg_ref, wo_ref, kmt_ref, vmx_ref,
                  wb_ref):
    wall_ref[:, 0:GATE0] = wint_ref[0:GATE0, :].T.astype(BF16)
    wall_ref[:, GATE0:N_PROJ] = wint_ref[GATE0 + 2 * N_HEADS:N_IN, :].T.astype(BF16)
    wg_ref[0:2 * N_HEADS, :] = wint_ref[GATE0:GATE0 + 2 * N_HEADS, :].astype(BF16)
    wg_ref[2 * N_HEADS:GATE_ROWS, :] = jnp.zeros((GATE_ROWS - 2 * N_HEADS, wg_ref.shape[1]), BF16)
    for i in range(D_MODEL // GROUP_W):
        wo_ref[i] = wout_ref[:, i * GROUP_W:(i + 1) * GROUP_W].astype(BF16)

    @pl.when(pl.program_id(0) == 0)
    def _cast_weights():
        wb_ref[...] = wkv_ref[...].astype(BF16)

    mem = mem_ref[...]
    ms = jnp.mean(mem * mem, axis=-1, keepdims=True)
    nrm = mem * lax.rsqrt(ms + EPS)
    col = lax.broadcasted_iota(jnp.int32, (N_MEM, HEAD_D), 1)
    ones_col = jnp.where(col == 0, 1.0, 0.0).astype(BF16)
    for l in range(wb_ref.shape[0]):
        mn = (nrm * g_ref[l]).astype(BF16)
        kv = jnp.dot(mn, wb_ref[l], preferred_element_type=F32)
        for hd in range(N_HEADS):
            hs = slice(hd * HEAD_D, (hd + 1) * HEAD_D)
            kmt_ref[l, hs, :] = kv[:, hs].T.astype(BF16)
            vmx_ref[l, hd, :, 0:HEAD_D] = kv[:, GROUP_W + hd * HEAD_D:GROUP_W + (hd + 1) * HEAD_D].astype(BF16)
            vmx_ref[l, hd, :, HEAD_D:2 * HEAD_D] = ones_col


def _resident(shape, layer):
    nd = len(shape)
    return pl.BlockSpec((None,) + shape, lambda k: (layer,) + (0,) * nd, pipeline_mode=pl.Buffered(1))


def _setup(w_in, w_out, mem, mem_norm_g, w_kv):
    depth, batch = w_in.shape[0], mem.shape[0]
    assert batch % depth == 0
    steps = batch // depth
    rb_in, rb_out = D_MODEL // steps, 3 * GROUP_W // steps
    return pl.pallas_call(
        _setup_kernel,
        grid=(batch,),
        in_specs=[
            pl.BlockSpec((None, N_IN, rb_in), lambda i: (i // steps, 0, i % steps)),
            pl.BlockSpec((None, rb_out, D_MODEL), lambda i: (i // steps, i % steps, 0)),
            pl.BlockSpec((None, N_MEM, D_MODEL), lambda i: (i, 0, 0)),
            pl.BlockSpec((depth, 1, D_MODEL), lambda i: (0, 0, 0)),
            pl.BlockSpec((depth, D_MODEL, 2 * GROUP_W), lambda i: (0, 0, 0), pipeline_mode=pl.Buffered(1)),
        ],
        out_specs=[
            pl.BlockSpec((None, rb_in, N_PROJ), lambda i: (i // steps, i % steps, 0)),
            pl.BlockSpec((None, GATE_ROWS, rb_in), lambda i: (i // steps, 0, i % steps)),
            pl.BlockSpec((None, D_MODEL // GROUP_W, rb_out, GROUP_W), lambda i: (i // steps, 0, i % steps, 0)),
            pl.BlockSpec((depth, None, GROUP_W, N_MEM), lambda i: (0, i, 0, 0)),
            pl.BlockSpec((depth, None, N_HEADS, N_MEM, 2 * HEAD_D), lambda i: (0, i, 0, 0, 0)),
        ],
        out_shape=[
            jax.ShapeDtypeStruct((depth, D_MODEL, N_PROJ), BF16),
            jax.ShapeDtypeStruct((depth, GATE_ROWS, D_MODEL), BF16),
            jax.ShapeDtypeStruct((depth, D_MODEL // GROUP_W, 3 * GROUP_W, GROUP_W), BF16),
            jax.ShapeDtypeStruct((depth, batch, GROUP_W, N_MEM), BF16),
            jax.ShapeDtypeStruct((depth, batch, N_HEADS, N_MEM, 2 * HEAD_D), BF16),
        ],
        scratch_shapes=[pltpu.VMEM((depth, D_MODEL, 2 * GROUP_W), BF16)],
        compiler_params=pltpu.CompilerParams(
            dimension_semantics=("arbitrary",), vmem_limit_bytes=VMEM_LIMIT_BYTES),
        name="setup_weights_memkv",
    )(jnp.swapaxes(w_in, 1, 2), w_out, mem, mem_norm_g.reshape(depth, 1, D_MODEL), w_kv)


def _layer(layer, final, x, kmt, vmx, ng, w_all, w_g, b_g, conv_w, conv_b, mng, pool_w, pool_scale, w_out, fng):
    batch, seq, _ = x.shape
    T = SEQ_TILE
    assert seq % (2 * T) == 0 and T % CHUNK == 0
    tiles_per_seq = seq // T
    pairs_per_seq = tiles_per_seq // 2
    last_pair = batch * pairs_per_seq - 1

    def next_pair(k):
        npair = jnp.minimum(k + 1, last_pair)
        return npair // pairs_per_seq, npair % pairs_per_seq, 0

    kern = functools.partial(_layer_kernel, final=final, tiles_per_seq=tiles_per_seq)
    set_scratch = [
        pltpu.VMEM((CONV_HIST + T, 2 * GROUP_W), F32),
        pltpu.VMEM((POOL_HIST + T, GROUP_W), F32),
        pltpu.VMEM((T, P_COLS), F32),
        pltpu.VMEM((GATE_ROWS, T), F32),
        pltpu.VMEM((T, N_HEADS * 2 * HEAD_D), BF16),
        pltpu.VMEM((T, GROUP_W), BF16),
    ]
    return pl.pallas_call(
        kern,
        grid=(batch * pairs_per_seq,),
        in_specs=[
            pl.BlockSpec((None, 2 * T, D_MODEL), lambda k: (k // pairs_per_seq, k % pairs_per_seq, 0)),
            pl.BlockSpec((None, 2 * T, D_MODEL), next_pair),
            pl.BlockSpec((None, None, GROUP_W, N_MEM), lambda k: (layer, k // pairs_per_seq, 0, 0)),
            pl.BlockSpec((None, None, N_HEADS, N_MEM, 2 * HEAD_D),
                         lambda k: (layer, k // pairs_per_seq, 0, 0, 0)),
            _resident((1, D_MODEL), layer),
            _resident((D_MODEL, N_PROJ), layer),
            _resident((GATE_ROWS, D_MODEL), layer),
            _resident((2 * N_HEADS, LANES), layer),
            _resident((CONV_K, 2 * GROUP_W), layer),
            _resident((1, 2 * GROUP_W), layer),
            _resident((1, GROUP_W), layer),
            _resident((len(POOL_WINDOWS), LANES, LANES), layer),
            _resident((1, GROUP_W), layer),
            _resident((D_MODEL // GROUP_W, 3 * GROUP_W, GROUP_W), layer),
            pl.BlockSpec((1, D_MODEL), lambda k: (0, 0), pipeline_mode=pl.Buffered(1)),
        ],
        out_specs=pl.BlockSpec((None, 2 * T, D_MODEL), lambda k: (k // pairs_per_seq, k % pairs_per_seq, 0)),
        out_shape=jax.ShapeDtypeStruct(x.shape, F32),
        scratch_shapes=[pltpu.VMEM((T, D_MODEL), BF16)] * 2 + set_scratch + set_scratch + [
            pltpu.VMEM((T, GROUP_W), F32),
            pltpu.VMEM((T, GROUP_W), F32),
            pltpu.VMEM((N_HEADS, HEAD_D, 2 * HEAD_D), F32),
            pltpu.VMEM((T, 3 * GROUP_W), BF16),
            pltpu.VMEM((2 * N_HEADS, LANES), F32),
        ],
        compiler_params=pltpu.CompilerParams(
            dimension_semantics=("arbitrary",), vmem_limit_bytes=VMEM_LIMIT_BYTES),
        name="trunk_layer_final" if final else "trunk_layer",
    )(x, x, kmt, vmx, ng, w_all, w_g, b_g, conv_w, conv_b, mng, pool_w, pool_scale, w_out, fng)


def kernel(x, mem, norm_g, w_in, b_gates, conv_w, conv_b, mlstm_norm_g, pool_w, pool_scale,
           mem_norm_g, w_mem_kv, w_out, final_norm_g):
    depth = w_in.shape[0]
    w_all, w_g, w_out_b, kmt, vmx = _setup(w_in, w_out, mem, mem_norm_g, w_mem_kv)
    b_g = jnp.broadcast_to(b_gates[:, :, None], (depth, 2 * N_HEADS, LANES))
    args = (kmt, vmx, norm_g.reshape(depth, 1, D_MODEL), w_all, w_g, b_g,
            conv_w, conv_b.reshape(depth, 1, 2 * GROUP_W), mlstm_norm_g.reshape(depth, 1, GROUP_W),
            pool_w, pool_scale.reshape(depth, 1, GROUP_W), w_out_b, final_norm_g.reshape(1, D_MODEL))
    for l in range(depth):
        x = _layer(l, l == depth - 1, x, *args)
    return x
```

```python
import functools

import jax
import jax.numpy as jnp
from jax import lax
from jax.experimental import pallas as pl
from jax.experimental.pallas import tpu as pltpu

D_MODEL = 1024
GROUP_W = 512
N_HEADS = 4
HEAD_D = 128
CONV_K = 4
POOL_WINDOWS = (2, 4, 8, 16)
N_MEM = 256
EPS = 1e-6

SEQ_TILE = 256
CHUNK = 256
CONV_HIST = 8
POOL_HIST = 16
LANES = 128
VMEM_LIMIT_BYTES = 52 * 1024 * 1024

N_IN = 2 * GROUP_W + 3 * GROUP_W + 2 * N_HEADS + 4 * GROUP_W
GATE0 = 5 * GROUP_W
Q0, K0, V0, O0, ZA0, U0, ZB0, QC0, ZC0 = (i * GROUP_W for i in range(9))
N_PROJ = 9 * GROUP_W
GATE_ROWS = 16
P_O, P_ZA, P_ZB, P_ZC = (i * GROUP_W for i in range(4))
P_COLS = 4 * GROUP_W

BF16 = jnp.bfloat16
F32 = jnp.float32


def _sigmoid(x):
    return 0.5 * jnp.tanh(0.5 * x) + 0.5


def _silu(x):
    hx = 0.5 * x
    return hx + hx * jnp.tanh(hx)


def _log_sigmoid(x):
    return jnp.minimum(x, 0.0) - jnp.log1p(jnp.exp(-jnp.abs(x)))


def _scan_lanes(x, op, identity):
    n = x.shape[1]
    lanes = lax.broadcasted_iota(jnp.int32, x.shape, 1)
    k = 1
    while k < n:
        x = op(x, jnp.where(lanes >= k, pltpu.roll(x, k, axis=1), identity))
        k *= 2
    return x


def _rmsnorm(x, g):
    ms = jnp.mean(x * x, axis=-1, keepdims=True)
    return (x * lax.rsqrt(ms + EPS)) * g


def _stage_a(x_ref, x_row, first, prev, cur, ng_ref, w_ref, wg_ref, h_ref):
    T = SEQ_TILE
    qk_prev, u_prev = prev[0], prev[1]
    qk_cur, u_cur, proj_cur, g_cur, vext_cur, qc_cur = cur

    def head():
        qk_cur[0:CONV_HIST, :] = jnp.where(first, 0.0, qk_prev[T:T + CONV_HIST, :])
        u_cur[0:POOL_HIST, :] = jnp.where(first, 0.0, u_prev[T:T + POOL_HIST, :])
        h_ref[...] = _rmsnorm(x_ref[x_row:x_row + T, :], ng_ref[...]).astype(BF16)

    def proj(c0):
        return jnp.dot(h_ref[...], w_ref[:, c0:c0 + GROUP_W], preferred_element_type=F32)

    def dot_q():
        qk_cur[CONV_HIST:CONV_HIST + T, 0:GROUP_W] = proj(Q0)

    def dot_k():
        qk_cur[CONV_HIST:CONV_HIST + T, GROUP_W:2 * GROUP_W] = proj(K0)

    def dot_u():
        u_cur[POOL_HIST:POOL_HIST + T, :] = proj(U0)

    def dot_to(src, dst):
        def run():
            proj_cur[:, dst:dst + GROUP_W] = proj(src)
        return run

    def dot_v():
        res = proj(V0)
        for hd in range(N_HEADS):
            vext_cur[:, hd * 2 * HEAD_D:hd * 2 * HEAD_D + HEAD_D] = (
                res[:, hd * HEAD_D:(hd + 1) * HEAD_D].astype(BF16))

    def dot_qc():
        qc_cur[...] = proj(QC0).astype(BF16)

    def dot_gates():
        g_cur[...] = lax.dot_general(wg_ref[...], h_ref[...], (((1,), (1,)), ((), ())),
                                     preferred_element_type=F32)

    return [head, dot_q, dot_k, dot_u, dot_v, dot_to(O0, P_O), dot_to(ZA0, P_ZA),
            dot_to(ZB0, P_ZB), dot_qc, dot_to(ZC0, P_ZC), dot_gates]


def _stage_b(x_ref, x_row, first, t_seq, cur, o_ref, kmt_ref, vmx_ref, bg_ref, cw_ref, cb_ref, mng_ref,
             pw_ref, ps_ref, wo_ref, fng_ref, q_ref, k_ref, cn_ref, m_ref, y_ref, final):
    T, L = SEQ_TILE, CHUNK
    qk_ext, u_ext, proj_ref, g_ref, vext_ref, qc_ref = cur

    def conv(c0):
        xs = qk_ext[:, c0:c0 + GROUP_W]
        acc = cb_ref[:, c0:c0 + GROUP_W] + cw_ref[CONV_K - 1:CONV_K, c0:c0 + GROUP_W] * xs[CONV_HIST:]
        for j in range(CONV_K - 1):
            back = pltpu.roll(xs, CONV_K - 1 - j, axis=0)[CONV_HIST:]
            acc = acc + cw_ref[j:j + 1, c0:c0 + GROUP_W] * back
        return _silu(acc)

    def conv_q():
        q_ref[...] = conv(0)

    def conv_k():
        k_ref[...] = conv(GROUP_W) * (HEAD_D ** -0.5)

    def mlstm_chunk(c):
        def run():
            rowi = lax.broadcasted_iota(jnp.int32, (L, L), 0)
            coli = lax.broadcasted_iota(jnp.int32, (L, L), 1)
            causal = coli <= rowi
            r0 = c * L
            g8 = g_ref[0:2 * N_HEADS, r0:r0 + L] + bg_ref[:, 0:1]
            gi = g8
            logf = _log_sigmoid(pltpu.roll(g8, N_HEADS, axis=0))
            b = _scan_lanes(logf, jnp.add, 0.0)
            r = gi - b
            m_prev = m_ref[:, 0:1]
            if c == 0:
                m_prev = jnp.where(first, 0.0, m_prev)
            mm = jnp.maximum(_scan_lanes(r, jnp.maximum, -jnp.inf), m_prev)
            a = jnp.exp(m_prev - mm)
            e = jnp.exp(-(mm + b))
            m_ref[...] = jnp.broadcast_to((mm + b)[:, L - 1:L], m_ref.shape)
            cols = jnp.concatenate([mm, a, e, jnp.zeros((LANES - 6 * N_HEADS, L), F32)], axis=0).T
            mm_c, a_c, e_c = (cols[:, i * 2 * N_HEADS:(i + 1) * 2 * N_HEADS] for i in range(3))
            heads = [slice(hd * HEAD_D, (hd + 1) * HEAD_D) for hd in range(N_HEADS)]
            qs = [q_ref[r0:r0 + L, hs] for hs in heads]
            vxs = [vext_ref[r0:r0 + L, hd * 2 * HEAD_D:(hd + 1) * 2 * HEAD_D] for hd in range(N_HEADS)]
            s_bf, k_tw = [], []
            for hd, hs in enumerate(heads):
                arg = r[hd:hd + 1, :] - mm_c[:, hd:hd + 1]
                p = jnp.exp(jnp.where(causal, arg, -1e30))
                k_t = k_ref[r0:r0 + L, hs].T
                s = jnp.dot(qs[hd].astype(BF16), k_t.astype(BF16), preferred_element_type=F32) * p
                s_bf.append(s.astype(BF16))
                k_tw.append((k_t * p[L - 1:L, :]).astype(BF16))
            nds = []
            for hd in range(N_HEADS):
                cn = cn_ref[hd]
                if c == 0:
                    cn = jnp.where(first, 0.0, cn)
                lhs = jnp.concatenate([(a_c[:, hd:hd + 1] * qs[hd]).astype(BF16), s_bf[hd]], axis=1)
                rhs = jnp.concatenate([cn.astype(BF16), vxs[hd]], axis=0)
                nds.append(jnp.dot(lhs, rhs, preferred_element_type=F32))
                cn_ref[hd] = (a[hd:hd + 1, L - 1:L] * cn
                              + jnp.dot(k_tw[hd], vxs[hd], preferred_element_type=F32))
            for hd, hs in enumerate(heads):
                nd = nds[hd]
                hh = nd[:, 0:HEAD_D] / jnp.maximum(jnp.abs(nd[:, HEAD_D:HEAD_D + 1]), e_c[:, hd:hd + 1])
                hg = _sigmoid(proj_ref[r0:r0 + L, P_O + hd * HEAD_D:P_O + (hd + 1) * HEAD_D]) * hh
                hg = hg * lax.rsqrt(jnp.mean(hg * hg, axis=-1, keepdims=True) + EPS)
                hg = hg * mng_ref[:, hs]
                ya = hg * _silu(proj_ref[r0:r0 + L, P_ZA + hd * HEAD_D:P_ZA + (hd + 1) * HEAD_D])
                y_ref[r0:r0 + L, hs] = ya.astype(BF16)
        return run

    def pool():
        tg = (t_seq * T + lax.broadcasted_iota(jnp.int32, (T, LANES), 0) + 1).astype(F32)
        zero_w = jnp.zeros((LANES, LANES), BF16)
        for g0 in range(0, len(POOL_WINDOWS), 2):
            pooled = []
            for g in (g0, g0 + 1):
                win = POOL_WINDOWS[g]
                gs = slice(g * LANES, (g + 1) * LANES)
                ssum = u_ext[:, gs]
                sh = 1
                while sh < win:
                    ssum = ssum + pltpu.roll(ssum, sh, axis=0)
                    sh *= 2
                u = u_ext[POOL_HIST:POOL_HIST + T, gs]
                pooled.append((ssum[POOL_HIST:POOL_HIST + T] / jnp.minimum(tg, float(win)) - u).astype(BF16))
            w_a, w_b = pw_ref[g0].astype(BF16), pw_ref[g0 + 1].astype(BF16)
            w_pair = jnp.concatenate([jnp.concatenate([w_a, zero_w], axis=1),
                                      jnp.concatenate([zero_w, w_b], axis=1)], axis=0)
            yb = jnp.dot(jnp.concatenate(pooled, axis=1), w_pair, preferred_element_type=F32)
            yb = (yb * ps_ref[:, g0 * LANES:(g0 + 2) * LANES]
                  * _silu(proj_ref[:, P_ZB + g0 * LANES:P_ZB + (g0 + 2) * LANES]))
            y_ref[:, GROUP_W + g0 * LANES:GROUP_W + (g0 + 2) * LANES] = yb.astype(BF16)

    def attention():
        for hd in range(N_HEADS):
            hs = slice(hd * HEAD_D, (hd + 1) * HEAD_D)
            s = jnp.dot(qc_ref[:, hs], kmt_ref[hs, :], preferred_element_type=F32) * (HEAD_D ** -0.5)
            pexp = jnp.exp(s - jnp.max(s, axis=-1, keepdims=True))
            pv = jnp.dot(pexp.astype(BF16), vmx_ref[hd], preferred_element_type=F32)
            yc = pv[:, 0:HEAD_D] / pv[:, HEAD_D:HEAD_D + 1]
            yc = yc * _silu(proj_ref[:, P_ZC + hd * HEAD_D:P_ZC + (hd + 1) * HEAD_D])
            y_ref[:, 2 * GROUP_W + hd * HEAD_D:2 * GROUP_W + (hd + 1) * HEAD_D] = yc.astype(BF16)

    def out_proj():
        rows = slice(x_row, x_row + T)
        for i in range(D_MODEL // GROUP_W):
            cs = slice(i * GROUP_W, (i + 1) * GROUP_W)
            o_ref[rows, cs] = x_ref[rows, cs] + jnp.dot(y_ref[...], wo_ref[i], preferred_element_type=F32)
        if final:
            o_ref[rows, :] = _rmsnorm(o_ref[rows, :], fng_ref[...])

    return dict(conv_q=conv_q, conv_k=conv_k, chunks=[mlstm_chunk(c) for c in range(T // L)],
                pool=pool, attention=attention, out_proj=out_proj)


def _layer_kernel(xp_ref, xn_ref, kmt_ref, vmx_ref, ng_ref, w_ref, wg_ref, bg_ref, cw_ref, cb_ref, mng_ref,
                  pw_ref, ps_ref, wo_ref, fng_ref, o_ref,
                  h_ref, qk0, u0, proj0, g0, vext0, qc0, qk1, u1, proj1, g1, vext1, qc1,
                  q_ref, k_ref, cn_ref, y_ref, m_ref, *, final, tiles_per_seq):
    T = SEQ_TILE
    k = pl.program_id(0)
    set0, set1 = (qk0, u0, proj0, g0, vext0, qc0), (qk1, u1, proj1, g1, vext1, qc1)
    a_args = (ng_ref, w_ref, wg_ref, h_ref)
    b_args = (kmt_ref, vmx_ref, bg_ref, cw_ref, cb_ref, mng_ref, pw_ref, ps_ref, wo_ref, fng_ref,
              q_ref, k_ref, cn_ref, m_ref, y_ref, final)

    @pl.when(k == 0)
    def _prologue():
        col = lax.broadcasted_iota(jnp.int32, vext0.shape, 1)
        for vext in (vext0, vext1):
            vext[...] = jnp.where(col % (2 * HEAD_D) == HEAD_D, 1.0, 0.0).astype(BF16)
        cn_ref[...] = jnp.zeros(cn_ref.shape, F32)
        m_ref[...] = jnp.zeros(m_ref.shape, F32)
        qk1[T:T + CONV_HIST, :] = jnp.zeros((CONV_HIST, 2 * GROUP_W), F32)
        u1[T:T + POOL_HIST, :] = jnp.zeros((POOL_HIST, GROUP_W), F32)
        for piece in _stage_a(xp_ref, 0, True, set1, set0, *a_args):
            piece()

    t_even = (2 * k) % tiles_per_seq
    first_even = t_even == 0
    first_next = (2 * k + 2) % tiles_per_seq == 0

    def half_step(x_row, first, t_seq, cur, nxt, xa_ref, xa_row, first_a):
        for piece in _stage_a(xa_ref, xa_row, first_a, cur, nxt, *a_args):
            piece()
        b = _stage_b(xp_ref, x_row, first, t_seq, cur, o_ref, *b_args)
        for phase in (b["conv_q"], b["conv_k"], *b["chunks"], b["pool"], b["attention"], b["out_proj"]):
            phase()

    half_step(0, first_even, t_even, set0, set1, xp_ref, T, False)
    half_step(T, False, t_even + 1, set1, set0, xn_ref, 0, first_next)


def _setup_kernel(wint_ref, wout_ref, mem_ref, g_ref, wkv_ref, wall_ref, wg_ref, wo_ref, kmt_ref, vmx_ref,
                  wb_ref):
    wall_ref[:, 0:GATE0] = wint_ref[0:GATE0, :].T.astype(BF16)
    wall_ref[:, GATE0:N_PROJ] = wint_ref[GATE0 + 2 * N_HEADS:N_IN, :].T.astype(BF16)
    wg_ref[0:2 * N_HEADS, :] = wint_ref[GATE0:GATE0 + 2 * N_HEADS, :].astype(BF16)
    wg_ref[2 * N_HEADS:GATE_ROWS, :] = jnp.zeros((GATE_ROWS - 2 * N_HEADS, wg_ref.shape[1]), BF16)
    for i in range(D_MODEL // GROUP_W):
        wo_ref[i] = wout_ref[:, i * GROUP_W:(i + 1) * GROUP_W].astype(BF16)

    @pl.when(pl.program_id(0) == 0)
    def _cast_weights():
        wb_ref[...] = wkv_ref[...].astype(BF16)

    mem = mem_ref[...]
    ms = jnp.mean(mem * mem, axis=-1, keepdims=True)
    nrm = mem * lax.rsqrt(ms + EPS)
    col = lax.broadcasted_iota(jnp.int32, (N_MEM, HEAD_D), 1)
    ones_col = jnp.where(col == 0, 1.0, 0.0).astype(BF16)
    for l in range(wb_ref.shape[0]):
        mn = (nrm * g_ref[l]).astype(BF16)
        kv = jnp.dot(mn, wb_ref[l], preferred_element_type=F32)
        for hd in range(N_HEADS):
            hs = slice(hd * HEAD_D, (hd + 1) * HEAD_D)
            kmt_ref[l, hs, :] = kv[:, hs].T.astype(BF16)
            vmx_ref[l, hd, :, 0:HEAD_D] = kv[:, GROUP_W + hd * HEAD_D:GROUP_W + (hd + 1) * HEAD_D].astype(BF16)
            vmx_ref[l, hd, :, HEAD_D:2 * HEAD_D] = ones_col


def _resident(shape, layer):
    nd = len(shape)
    return pl.BlockSpec((None,) + shape, lambda k: (layer,) + (0,) * nd, pipeline_mode=pl.Buffered(1))


def _setup(w_in, w_out, mem, mem_norm_g, w_kv):
    depth, batch = w_in.shape[0], mem.shape[0]
    assert batch % depth == 0
    steps = batch // depth
    rb_in, rb_out = D_MODEL // steps, 3 * GROUP_W // steps
    return pl.pallas_call(
        _setup_kernel,
        grid=(batch,),
        in_specs=[
            pl.BlockSpec((None, N_IN, rb_in), lambda i: (i // steps, 0, i % steps)),
            pl.BlockSpec((None, rb_out, D_MODEL), lambda i: (i // steps, i % steps, 0)),
            pl.BlockSpec((None, N_MEM, D_MODEL), lambda i: (i, 0, 0)),
            pl.BlockSpec((depth, 1, D_MODEL), lambda i: (0, 0, 0)),
            pl.BlockSpec((depth, D_MODEL, 2 * GROUP_W), lambda i: (0, 0, 0), pipeline_mode=pl.Buffered(1)),
        ],
        out_specs=[
            pl.BlockSpec((None, rb_in, N_PROJ), lambda i: (i // steps, i % steps, 0)),
            pl.BlockSpec((None, GATE_ROWS, rb_in), lambda i: (i // steps, 0, i % steps)),
            pl.BlockSpec((None, D_MODEL // GROUP_W, rb_out, GROUP_W), lambda i: (i // steps, 0, i % steps, 0)),
            pl.BlockSpec((depth, None, GROUP_W, N_MEM), lambda i: (0, i, 0, 0)),
            pl.BlockSpec((depth, None, N_HEADS, N_MEM, 2 * HEAD_D), lambda i: (0, i, 0, 0, 0)),
        ],
        out_shape=[
            jax.ShapeDtypeStruct((depth, D_MODEL, N_PROJ), BF16),
            jax.ShapeDtypeStruct((depth, GATE_ROWS, D_MODEL), BF16),
            jax.ShapeDtypeStruct((depth, D_MODEL // GROUP_W, 3 * GROUP_W, GROUP_W), BF16),
            jax.ShapeDtypeStruct((depth, batch, GROUP_W, N_MEM), BF16),
            jax.ShapeDtypeStruct((depth, batch, N_HEADS, N_MEM, 2 * HEAD_D), BF16),
        ],
        scratch_shapes=[pltpu.VMEM((depth, D_MODEL, 2 * GROUP_W), BF16)],
        compiler_params=pltpu.CompilerParams(
            dimension_semantics=("arbitrary",), vmem_limit_bytes=VMEM_LIMIT_BYTES),
        name="setup_weights_memkv",
    )(jnp.swapaxes(w_in, 1, 2), w_out, mem, mem_norm_g.reshape(depth, 1, D_MODEL), w_kv)


def _layer(layer, final, x, kmt, vmx, ng, w_all, w_g, b_g, conv_w, conv_b, mng, pool_w, pool_scale, w_out, fng):
    batch, seq, _ = x.shape
    T = SEQ_TILE
    assert seq % (2 * T) == 0 and T % CHUNK == 0
    tiles_per_seq = seq // T
    pairs_per_seq = tiles_per_seq // 2
    last_tile = batch * tiles_per_seq - 1

    def next_tile(k):
        nt = jnp.minimum(2 * k + 2, last_tile)
        return nt // tiles_per_seq, nt % tiles_per_seq, 0

    kern = functools.partial(_layer_kernel, final=final, tiles_per_seq=tiles_per_seq)
    set_scratch = [
        pltpu.VMEM((CONV_HIST + T, 2 * GROUP_W), F32),
        pltpu.VMEM((POOL_HIST + T, GROUP_W), F32),
        pltpu.VMEM((T, P_COLS), F32),
        pltpu.VMEM((GATE_ROWS, T), F32),
        pltpu.VMEM((T, N_HEADS * 2 * HEAD_D), BF16),
        pltpu.VMEM((T, GROUP_W), BF16),
    ]
    return pl.pallas_call(
        kern,
        grid=(batch * pairs_per_seq,),
        in_specs=[
            pl.BlockSpec((None, 2 * T, D_MODEL), lambda k: (k // pairs_per_seq, k % pairs_per_seq, 0)),
            pl.BlockSpec((None, T, D_MODEL), next_tile),
            pl.BlockSpec((None, None, GROUP_W, N_MEM), lambda k: (layer, k // pairs_per_seq, 0, 0)),
            pl.BlockSpec((None, None, N_HEADS, N_MEM, 2 * HEAD_D),
                         lambda k: (layer, k // pairs_per_seq, 0, 0, 0)),
            _resident((1, D_MODEL), layer),
            _resident((D_MODEL, N_PROJ), layer),
            _resident((GATE_ROWS, D_MODEL), layer),
            _resident((2 * N_HEADS, LANES), layer),
            _resident((CONV_K, 2 * GROUP_W), layer),
            _resident((1, 2 * GROUP_W), layer),
            _resident((1, GROUP_W), layer),
            _resident((len(POOL_WINDOWS), LANES, LANES), layer),
            _resident((1, GROUP_W), layer),
            _resident((D_MODEL // GROUP_W, 3 * GROUP_W, GROUP_W), layer),
            pl.BlockSpec((1, D_MODEL), lambda k: (0, 0), pipeline_mode=pl.Buffered(1)),
        ],
        out_specs=pl.BlockSpec((None, 2 * T, D_MODEL), lambda k: (k // pairs_per_seq, k % pairs_per_seq, 0)),
        out_shape=jax.ShapeDtypeStruct(x.shape, F32),
        scratch_shapes=[pltpu.VMEM((T, D_MODEL), BF16)] + set_scratch + set_scratch + [
            pltpu.VMEM((T, GROUP_W), F32),
            pltpu.VMEM((T, GROUP_W), F32),
            pltpu.VMEM((N_HEADS, HEAD_D, 2 * HEAD_D), F32),
            pltpu.VMEM((T, 3 * GROUP_W), BF16),
            pltpu.VMEM((2 * N_HEADS, LANES), F32),
        ],
        compiler_params=pltpu.CompilerParams(
            dimension_semantics=("arbitrary",), vmem_limit_bytes=VMEM_LIMIT_BYTES),
        name="trunk_layer_final" if final else "trunk_layer",
    )(x, x, kmt, vmx, ng, w_all, w_g, b_g, conv_w, conv_b, mng, pool_w, pool_scale, w_out, fng)


def kernel(x, mem, norm_g, w_in, b_gates, conv_w, conv_b, mlstm_norm_g, pool_w, pool_scale,
           mem_norm_g, w_mem_kv, w_out, final_norm_g):
    depth = w_in.shape[0]
    w_all, w_g, w_out_b, kmt, vmx = _setup(w_in, w_out, mem, mem_norm_g, w_mem_kv)
    b_g = jnp.broadcast_to(b_gates[:, :, None], (depth, 2 * N_HEADS, LANES))
    args = (kmt, vmx, norm_g.reshape(depth, 1, D_MODEL), w_all, w_g, b_g,
            conv_w, conv_b.reshape(depth, 1, 2 * GROUP_W), mlstm_norm_g.reshape(depth, 1, GROUP_W),
            pool_w, pool_scale.reshape(depth, 1, GROUP_W), w_out_b, final_norm_g.reshape(1, D_MODEL))
    for l in range(depth):
        x = _layer(l, l == depth - 1, x, *args)
    return x
```

```python
import functools

import jax
import jax.numpy as jnp
from jax import lax
from jax.experimental import pallas as pl
from jax.experimental.pallas import tpu as pltpu

D_MODEL = 1024
GROUP_W = 512
N_HEADS = 4
HEAD_D = 128
CONV_K = 4
POOL_WINDOWS = (2, 4, 8, 16)
N_MEM = 256
EPS = 1e-6

SEQ_TILE = 256
CHUNK = 256
CONV_HIST = 8
POOL_HIST = 16
LANES = 128
VMEM_LIMIT_BYTES = 52 * 1024 * 1024

N_IN = 2 * GROUP_W + 3 * GROUP_W + 2 * N_HEADS + 4 * GROUP_W
GATE0 = 5 * GROUP_W
Q0, K0, V0, O0, ZA0, U0, ZB0, QC0, ZC0 = (i * GROUP_W for i in range(9))
N_PROJ = 9 * GROUP_W
GATE_ROWS = 16
P_O, P_ZA, P_ZB, P_ZC = (i * GROUP_W for i in range(4))
P_COLS = 4 * GROUP_W

BF16 = jnp.bfloat16
F32 = jnp.float32


def _sigmoid(x):
    return 0.5 * jnp.tanh(0.5 * x) + 0.5


def _silu(x):
    hx = 0.5 * x
    return hx + hx * jnp.tanh(hx)


def _log_sigmoid(x):
    return jnp.minimum(x, 0.0) - jnp.log1p(jnp.exp(-jnp.abs(x)))


def _scan_lanes(x, op, identity):
    n = x.shape[1]
    lanes = lax.broadcasted_iota(jnp.int32, x.shape, 1)
    k = 1
    while k < n:
        x = op(x, jnp.where(lanes >= k, pltpu.roll(x, k, axis=1), identity))
        k *= 2
    return x


def _rmsnorm(x, g):
    ms = jnp.mean(x * x, axis=-1, keepdims=True)
    return (x * lax.rsqrt(ms + EPS)) * g


def _stage_a(x_ref, x_row, first, prev, cur, ng_ref, w_ref, wg_ref, h_ref):
    T = SEQ_TILE
    qk_prev, u_prev = prev[0], prev[1]
    qk_cur, u_cur, proj_cur, g_cur, vext_cur, qc_cur = cur

    def head():
        qk_cur[0:CONV_HIST, :] = jnp.where(first, 0.0, qk_prev[T:T + CONV_HIST, :])
        u_cur[0:POOL_HIST, :] = jnp.where(first, 0.0, u_prev[T:T + POOL_HIST, :])
        h_ref[...] = _rmsnorm(x_ref[x_row:x_row + T, :], ng_ref[...]).astype(BF16)

    def proj(c0):
        return jnp.dot(h_ref[...], w_ref[:, c0:c0 + GROUP_W], preferred_element_type=F32)

    def dot_q():
        qk_cur[CONV_HIST:CONV_HIST + T, 0:GROUP_W] = proj(Q0)

    def dot_k():
        qk_cur[CONV_HIST:CONV_HIST + T, GROUP_W:2 * GROUP_W] = proj(K0)

    def dot_u():
        u_cur[POOL_HIST:POOL_HIST + T, :] = proj(U0)

    def dot_to(src, dst):
        def run():
            proj_cur[:, dst:dst + GROUP_W] = proj(src)
        return run

    def dot_v():
        res = proj(V0)
        for hd in range(N_HEADS):
            vext_cur[:, hd * 2 * HEAD_D:hd * 2 * HEAD_D + HEAD_D] = (
                res[:, hd * HEAD_D:(hd + 1) * HEAD_D].astype(BF16))

    def dot_qc():
        qc_cur[...] = proj(QC0).astype(BF16)

    def dot_gates():
        g_cur[...] = lax.dot_general(wg_ref[...], h_ref[...], (((1,), (1,)), ((), ())),
                                     preferred_element_type=F32)

    return [head, dot_q, dot_k, dot_u, dot_v, dot_to(O0, P_O), dot_to(ZA0, P_ZA),
            dot_to(ZB0, P_ZB), dot_qc, dot_to(ZC0, P_ZC), dot_gates]


def _stage_b(x_ref, x_row, first, t_seq, cur, o_ref, kmt_ref, vmx_ref, bg_ref, cw_ref, cb_ref, mng_ref,
             pw_ref, ps_ref, wo_ref, fng_ref, q_ref, k_ref, cn_ref, m_ref, y_ref, final):
    T, L = SEQ_TILE, CHUNK
    qk_ext, u_ext, proj_ref, g_ref, vext_ref, qc_ref = cur

    def conv(c0):
        xs = qk_ext[:, c0:c0 + GROUP_W]
        acc = cb_ref[:, c0:c0 + GROUP_W] + cw_ref[CONV_K - 1:CONV_K, c0:c0 + GROUP_W] * xs[CONV_HIST:]
        for j in range(CONV_K - 1):
            back = pltpu.roll(xs, CONV_K - 1 - j, axis=0)[CONV_HIST:]
            acc = acc + cw_ref[j:j + 1, c0:c0 + GROUP_W] * back
        return _silu(acc)

    def conv_q():
        q_ref[...] = conv(0)

    def conv_k():
        k_ref[...] = conv(GROUP_W) * (HEAD_D ** -0.5)

    def mlstm_chunk(c):
        def run():
            rowi = lax.broadcasted_iota(jnp.int32, (L, L), 0)
            coli = lax.broadcasted_iota(jnp.int32, (L, L), 1)
            causal = coli <= rowi
            r0 = c * L
            g8 = g_ref[0:2 * N_HEADS, r0:r0 + L] + bg_ref[:, 0:1]
            gi = g8
            logf = _log_sigmoid(pltpu.roll(g8, N_HEADS, axis=0))
            b = _scan_lanes(logf, jnp.add, 0.0)
            r = gi - b
            m_prev = m_ref[:, 0:1]
            if c == 0:
                m_prev = jnp.where(first, 0.0, m_prev)
            mm = jnp.maximum(_scan_lanes(r, jnp.maximum, -jnp.inf), m_prev)
            a = jnp.exp(m_prev - mm)
            e = jnp.exp(-(mm + b))
            m_ref[...] = jnp.broadcast_to((mm + b)[:, L - 1:L], m_ref.shape)
            cols = jnp.concatenate([mm, a, e, jnp.zeros((LANES - 6 * N_HEADS, L), F32)], axis=0).T
            mm_c, a_c, e_c = (cols[:, i * 2 * N_HEADS:(i + 1) * 2 * N_HEADS] for i in range(3))
            heads = [slice(hd * HEAD_D, (hd + 1) * HEAD_D) for hd in range(N_HEADS)]
            qs = [q_ref[r0:r0 + L, hs] for hs in heads]
            vxs = [vext_ref[r0:r0 + L, hd * 2 * HEAD_D:(hd + 1) * 2 * HEAD_D] for hd in range(N_HEADS)]
            s_bf, k_tw = [], []
            for hd, hs in enumerate(heads):
                arg = r[hd:hd + 1, :] - mm_c[:, hd:hd + 1]
                p = jnp.exp(jnp.where(causal, arg, -1e30))
                k_t = k_ref[r0:r0 + L, hs].T
                s = jnp.dot(qs[hd].astype(BF16), k_t.astype(BF16), preferred_element_type=F32) * p
                s_bf.append(s.astype(BF16))
                k_tw.append((k_t * p[L - 1:L, :]).astype(BF16))
            nds = []
            for hd in range(N_HEADS):
                cn = cn_ref[hd]
                if c == 0:
                    cn = jnp.where(first, 0.0, cn)
                lhs = jnp.concatenate([(a_c[:, hd:hd + 1] * qs[hd]).astype(BF16), s_bf[hd]], axis=1)
                rhs = jnp.concatenate([cn.astype(BF16), vxs[hd]], axis=0)
                nds.append(jnp.dot(lhs, rhs, preferred_element_type=F32))
                cn_ref[hd] = (a[hd:hd + 1, L - 1:L] * cn
                              + jnp.dot(k_tw[hd], vxs[hd], preferred_element_type=F32))
            for hd, hs in enumerate(heads):
                nd = nds[hd]
                hh = nd[:, 0:HEAD_D] / jnp.maximum(jnp.abs(nd[:, HEAD_D:HEAD_D + 1]), e_c[:, hd:hd + 1])
                hg = _sigmoid(proj_ref[r0:r0 + L, P_O + hd * HEAD_D:P_O + (hd + 1) * HEAD_D]) * hh
                hg = hg * lax.rsqrt(jnp.mean(hg * hg, axis=-1, keepdims=True) + EPS)
                hg = hg * mng_ref[:, hs]
                ya = hg * _silu(proj_ref[r0:r0 + L, P_ZA + hd * HEAD_D:P_ZA + (hd + 1) * HEAD_D])
                y_ref[r0:r0 + L, hs] = ya.astype(BF16)
        return run

    def pool():
        tg = (t_seq * T + lax.broadcasted_iota(jnp.int32, (T, LANES), 0) + 1).astype(F32)
        zero_w = jnp.zeros((LANES, LANES), BF16)
        for g0 in range(0, len(POOL_WINDOWS), 2):
            pooled = []
            for g in (g0, g0 + 1):
                win = POOL_WINDOWS[g]
                gs = slice(g * LANES, (g + 1) * LANES)
                ssum = u_ext[:, gs]
                sh = 1
                while sh < win:
                    ssum = ssum + pltpu.roll(ssum, sh, axis=0)
                    sh *= 2
                u = u_ext[POOL_HIST:POOL_HIST + T, gs]
                ssum = ssum[POOL_HIST:POOL_HIST + T]
                if win & (win - 1) == 0 and win <= POOL_HIST:
                    mean = jnp.concatenate(
                        [ssum[:POOL_HIST] / jnp.minimum(tg[:POOL_HIST], float(win)),
                         ssum[POOL_HIST:] * (1.0 / win)], axis=0)
                else:
                    mean = ssum / jnp.minimum(tg, float(win))
                pooled.append((mean - u).astype(BF16))
            w_a, w_b = pw_ref[g0].astype(BF16), pw_ref[g0 + 1].astype(BF16)
            w_pair = jnp.concatenate([jnp.concatenate([w_a, zero_w], axis=1),
                                      jnp.concatenate([zero_w, w_b], axis=1)], axis=0)
            yb = jnp.dot(jnp.concatenate(pooled, axis=1), w_pair, preferred_element_type=F32)
            yb = (yb * ps_ref[:, g0 * LANES:(g0 + 2) * LANES]
                  * _silu(proj_ref[:, P_ZB + g0 * LANES:P_ZB + (g0 + 2) * LANES]))
            y_ref[:, GROUP_W + g0 * LANES:GROUP_W + (g0 + 2) * LANES] = yb.astype(BF16)

    def attention():
        for hd in range(N_HEADS):
            hs = slice(hd * HEAD_D, (hd + 1) * HEAD_D)
            s = jnp.dot(qc_ref[:, hs], kmt_ref[hs, :], preferred_element_type=F32) * (HEAD_D ** -0.5)
            pexp = jnp.exp(s - jnp.max(s, axis=-1, keepdims=True))
            pv = jnp.dot(pexp.astype(BF16), vmx_ref[hd], preferred_element_type=F32)
            yc = pv[:, 0:HEAD_D] / pv[:, HEAD_D:HEAD_D + 1]
            yc = yc * _silu(proj_ref[:, P_ZC + hd * HEAD_D:P_ZC + (hd + 1) * HEAD_D])
            y_ref[:, 2 * GROUP_W + hd * HEAD_D:2 * GROUP_W + (hd + 1) * HEAD_D] = yc.astype(BF16)

    def out_proj():
        rows = slice(x_row, x_row + T)
        for i in range(D_MODEL // GROUP_W):
            cs = slice(i * GROUP_W, (i + 1) * GROUP_W)
            o_ref[rows, cs] = x_ref[rows, cs] + jnp.dot(y_ref[...], wo_ref[i], preferred_element_type=F32)
        if final:
            o_ref[rows, :] = _rmsnorm(o_ref[rows, :], fng_ref[...])

    return dict(conv_q=conv_q, conv_k=conv_k, chunks=[mlstm_chunk(c) for c in range(T // L)],
                pool=pool, attention=attention, out_proj=out_proj)


def _layer_kernel(xp_ref, xn_ref, kmt_ref, vmx_ref, ng_ref, w_ref, wg_ref, bg_ref, cw_ref, cb_ref, mng_ref,
                  pw_ref, ps_ref, wo_ref, fng_ref, o_ref,
                  h_ref, qk0, u0, proj0, g0, vext0, qc0, qk1, u1, proj1, g1, vext1, qc1,
                  q_ref, k_ref, cn_ref, y_ref, m_ref, *, final, tiles_per_seq):
    T = SEQ_TILE
    k = pl.program_id(0)
    set0, set1 = (qk0, u0, proj0, g0, vext0, qc0), (qk1, u1, proj1, g1, vext1, qc1)
    a_args = (ng_ref, w_ref, wg_ref, h_ref)
    b_args = (kmt_ref, vmx_ref, bg_ref, cw_ref, cb_ref, mng_ref, pw_ref, ps_ref, wo_ref, fng_ref,
              q_ref, k_ref, cn_ref, m_ref, y_ref, final)

    @pl.when(k == 0)
    def _prologue():
        col = lax.broadcasted_iota(jnp.int32, vext0.shape, 1)
        for vext in (vext0, vext1):
            vext[...] = jnp.where(col % (2 * HEAD_D) == HEAD_D, 1.0, 0.0).astype(BF16)
        cn_ref[...] = jnp.zeros(cn_ref.shape, F32)
        m_ref[...] = jnp.zeros(m_ref.shape, F32)
        qk1[T:T + CONV_HIST, :] = jnp.zeros((CONV_HIST, 2 * GROUP_W), F32)
        u1[T:T + POOL_HIST, :] = jnp.zeros((POOL_HIST, GROUP_W), F32)
        for piece in _stage_a(xp_ref, 0, True, set1, set0, *a_args):
            piece()

    t_even = (2 * k) % tiles_per_seq
    first_even = t_even == 0
    first_next = (2 * k + 2) % tiles_per_seq == 0

    def half_step(x_row, first, t_seq, cur, nxt, xa_ref, xa_row, first_a):
        for piece in _stage_a(xa_ref, xa_row, first_a, cur, nxt, *a_args):
            piece()
        b = _stage_b(xp_ref, x_row, first, t_seq, cur, o_ref, *b_args)
        for phase in (b["conv_q"], b["conv_k"], *b["chunks"], b["pool"], b["attention"], b["out_proj"]):
            phase()

    half_step(0, first_even, t_even, set0, set1, xp_ref, T, False)
    half_step(T, False, t_even + 1, set1, set0, xn_ref, 0, first_next)


def _setup_kernel(wint_ref, wout_ref, mem_ref, g_ref, wkv_ref, wall_ref, wg_ref, wo_ref, kmt_ref, vmx_ref,
                  wb_ref):
    wall_ref[:, 0:GATE0] = wint_ref[0:GATE0, :].T.astype(BF16)
    wall_ref[:, GATE0:N_PROJ] = wint_ref[GATE0 + 2 * N_HEADS:N_IN, :].T.astype(BF16)
    wg_ref[0:2 * N_HEADS, :] = wint_ref[GATE0:GATE0 + 2 * N_HEADS, :].astype(BF16)
    wg_ref[2 * N_HEADS:GATE_ROWS, :] = jnp.zeros((GATE_ROWS - 2 * N_HEADS, wg_ref.shape[1]), BF16)
    for i in range(D_MODEL // GROUP_W):
        wo_ref[i] = wout_ref[:, i * GROUP_W:(i + 1) * GROUP_W].astype(BF16)

    @pl.when(pl.program_id(0) == 0)
    def _cast_weights():
        wb_ref[...] = wkv_ref[...].astype(BF16)

    mem = mem_ref[...]
    ms = jnp.mean(mem * mem, axis=-1, keepdims=True)
    nrm = mem * lax.rsqrt(ms + EPS)
    col = lax.broadcasted_iota(jnp.int32, (N_MEM, HEAD_D), 1)
    ones_col = jnp.where(col == 0, 1.0, 0.0).astype(BF16)
    for l in range(wb_ref.shape[0]):
        mn = (nrm * g_ref[l]).astype(BF16)
        kv = jnp.dot(mn, wb_ref[l], preferred_element_type=F32)
        for hd in range(N_HEADS):
            hs = slice(hd * HEAD_D, (hd + 1) * HEAD_D)
            kmt_ref[l, hs, :] = kv[:, hs].T.astype(BF16)
            vmx_ref[l, hd, :, 0:HEAD_D] = kv[:, GROUP_W + hd * HEAD_D:GROUP_W + (hd + 1) * HEAD_D].astype(BF16)
            vmx_ref[l, hd, :, HEAD_D:2 * HEAD_D] = ones_col


def _resident(shape, layer):
    nd = len(shape)
    return pl.BlockSpec((None,) + shape, lambda k: (layer,) + (0,) * nd, pipeline_mode=pl.Buffered(1))


def _setup(w_in, w_out, mem, mem_norm_g, w_kv):
    depth, batch = w_in.shape[0], mem.shape[0]
    assert batch % depth == 0
    steps = batch // depth
    rb_in, rb_out = D_MODEL // steps, 3 * GROUP_W // steps
    return pl.pallas_call(
        _setup_kernel,
        grid=(batch,),
        in_specs=[
            pl.BlockSpec((None, N_IN, rb_in), lambda i: (i // steps, 0, i % steps)),
            pl.BlockSpec((None, rb_out, D_MODEL), lambda i: (i // steps, i % steps, 0)),
            pl.BlockSpec((None, N_MEM, D_MODEL), lambda i: (i, 0, 0)),
            pl.BlockSpec((depth, 1, D_MODEL), lambda i: (0, 0, 0)),
            pl.BlockSpec((depth, D_MODEL, 2 * GROUP_W), lambda i: (0, 0, 0), pipeline_mode=pl.Buffered(1)),
        ],
        out_specs=[
            pl.BlockSpec((None, rb_in, N_PROJ), lambda i: (i // steps, i % steps, 0)),
            pl.BlockSpec((None, GATE_ROWS, rb_in), lambda i: (i // steps, 0, i % steps)),
            pl.BlockSpec((None, D_MODEL // GROUP_W, rb_out, GROUP_W), lambda i: (i // steps, 0, i % steps, 0)),
            pl.BlockSpec((depth, None, GROUP_W, N_MEM), lambda i: (0, i, 0, 0)),
            pl.BlockSpec((depth, None, N_HEADS, N_MEM, 2 * HEAD_D), lambda i: (0, i, 0, 0, 0)),
        ],
        out_shape=[
            jax.ShapeDtypeStruct((depth, D_MODEL, N_PROJ), BF16),
            jax.ShapeDtypeStruct((depth, GATE_ROWS, D_MODEL), BF16),
            jax.ShapeDtypeStruct((depth, D_MODEL // GROUP_W, 3 * GROUP_W, GROUP_W), BF16),
            jax.ShapeDtypeStruct((depth, batch, GROUP_W, N_MEM), BF16),
            jax.ShapeDtypeStruct((depth, batch, N_HEADS, N_MEM, 2 * HEAD_D), BF16),
        ],
        scratch_shapes=[pltpu.VMEM((depth, D_MODEL, 2 * GROUP_W), BF16)],
        compiler_params=pltpu.CompilerParams(
            dimension_semantics=("arbitrary",), vmem_limit_bytes=VMEM_LIMIT_BYTES),
        name="setup_weights_memkv",
    )(jnp.swapaxes(w_in, 1, 2), w_out, mem, mem_norm_g.reshape(depth, 1, D_MODEL), w_kv)


def _layer(layer, final, x, kmt, vmx, ng, w_all, w_g, b_g, conv_w, conv_b, mng, pool_w, pool_scale, w_out, fng):
    batch, seq, _ = x.shape
    T = SEQ_TILE
    assert seq % (2 * T) == 0 and T % CHUNK == 0
    tiles_per_seq = seq // T
    pairs_per_seq = tiles_per_seq // 2
    last_tile = batch * tiles_per_seq - 1

    def next_tile(k):
        nt = jnp.minimum(2 * k + 2, last_tile)
        return nt // tiles_per_seq, nt % tiles_per_seq, 0

    kern = functools.partial(_layer_kernel, final=final, tiles_per_seq=tiles_per_seq)
    set_scratch = [
        pltpu.VMEM((CONV_HIST + T, 2 * GROUP_W), F32),
        pltpu.VMEM((POOL_HIST + T, GROUP_W), F32),
        pltpu.VMEM((T, P_COLS), F32),
        pltpu.VMEM((GATE_ROWS, T), F32),
        pltpu.VMEM((T, N_HEADS * 2 * HEAD_D), BF16),
        pltpu.VMEM((T, GROUP_W), BF16),
    ]
    return pl.pallas_call(
        kern,
        grid=(batch * pairs_per_seq,),
        in_specs=[
            pl.BlockSpec((None, 2 * T, D_MODEL), lambda k: (k // pairs_per_seq, k % pairs_per_seq, 0)),
            pl.BlockSpec((None, T, D_MODEL), next_tile),
            pl.BlockSpec((None, None, GROUP_W, N_MEM), lambda k: (layer, k // pairs_per_seq, 0, 0)),
            pl.BlockSpec((None, None, N_HEADS, N_MEM, 2 * HEAD_D),
                         lambda k: (layer, k // pairs_per_seq, 0, 0, 0)),
            _resident((1, D_MODEL), layer),
            _resident((D_MODEL, N_PROJ), layer),
            _resident((GATE_ROWS, D_MODEL), layer),
            _resident((2 * N_HEADS, LANES), layer),
            _resident((CONV_K, 2 * GROUP_W), layer),
            _resident((1, 2 * GROUP_W), layer),
            _resident((1, GROUP_W), layer),
            _resident((len(POOL_WINDOWS), LANES, LANES), layer),
            _resident((1, GROUP_W), layer),
            _resident((D_MODEL // GROUP_W, 3 * GROUP_W, GROUP_W), layer),
            pl.BlockSpec((1, D_MODEL), lambda k: (0, 0), pipeline_mode=pl.Buffered(1)),
        ],
        out_specs=pl.BlockSpec((None, 2 * T, D_MODEL), lambda k: (k // pairs_per_seq, k % pairs_per_seq, 0)),
        out_shape=jax.ShapeDtypeStruct(x.shape, F32),
        scratch_shapes=[pltpu.VMEM((T, D_MODEL), BF16)] + set_scratch + set_scratch + [
            pltpu.VMEM((T, GROUP_W), F32),
            pltpu.VMEM((T, GROUP_W), F32),
            pltpu.VMEM((N_HEADS, HEAD_D, 2 * HEAD_D), F32),
            pltpu.VMEM((T, 3 * GROUP_W), BF16),
            pltpu.VMEM((2 * N_HEADS, LANES), F32),
        ],
        compiler_params=pltpu.CompilerParams(
            dimension_semantics=("arbitrary",), vmem_limit_bytes=VMEM_LIMIT_BYTES),
        name="trunk_layer_final" if final else "trunk_layer",
    )(x, x, kmt, vmx, ng, w_all, w_g, b_g, conv_w, conv_b, mng, pool_w, pool_scale, w_out, fng)


def kernel(x, mem, norm_g, w_in, b_gates, conv_w, conv_b, mlstm_norm_g, pool_w, pool_scale,
           mem_norm_g, w_mem_kv, w_out, final_norm_g):
    depth = w_in.shape[0]
    w_all, w_g, w_out_b, kmt, vmx = _setup(w_in, w_out, mem, mem_norm_g, w_mem_kv)
    b_g = jnp.broadcast_to(b_gates[:, :, None], (depth, 2 * N_HEADS, LANES))
    args = (kmt, vmx, norm_g.reshape(depth, 1, D_MODEL), w_all, w_g, b_g,
            conv_w, conv_b.reshape(depth, 1, 2 * GROUP_W), mlstm_norm_g.reshape(depth, 1, GROUP_W),
            pool_w, pool_scale.reshape(depth, 1, GROUP_W), w_out_b, final_norm_g.reshape(1, D_MODEL))
    for l in range(depth):
        x = _layer(l, l == depth - 1, x, *args)
    return x
```

```python
import functools

import jax
import jax.numpy as jnp
from jax import lax
from jax.experimental import pallas as pl
from jax.experimental.pallas import tpu as pltpu

D_MODEL = 1024
GROUP_W = 512
N_HEADS = 4
HEAD_D = 128
CONV_K = 4
POOL_WINDOWS = (2, 4, 8, 16)
N_MEM = 256
EPS = 1e-6

SEQ_TILE = 256
CHUNK = 256
CONV_HIST = 8
POOL_HIST = 16
LANES = 128
VMEM_LIMIT_BYTES = 52 * 1024 * 1024

N_IN = 2 * GROUP_W + 3 * GROUP_W + 2 * N_HEADS + 4 * GROUP_W
GATE0 = 5 * GROUP_W
Q0, K0, V0, O0, ZA0, U0, ZB0, QC0, ZC0 = (i * GROUP_W for i in range(9))
N_PROJ = 9 * GROUP_W
GATE_ROWS = 16
P_O, P_ZA, P_ZB, P_ZC = (i * GROUP_W for i in range(4))
P_COLS = 4 * GROUP_W

BF16 = jnp.bfloat16
F32 = jnp.float32


def _sigmoid(x):
    return 0.5 * jnp.tanh(0.5 * x) + 0.5


def _silu(x):
    hx = 0.5 * x
    return hx + hx * jnp.tanh(hx)


def _log_sigmoid(x):
    return jnp.minimum(x, 0.0) - jnp.log1p(jnp.exp(-jnp.abs(x)))


def _scan_lanes(x, op, identity):
    n = x.shape[1]
    lanes = lax.broadcasted_iota(jnp.int32, x.shape, 1)
    k = 1
    while k < n:
        x = op(x, jnp.where(lanes >= k, pltpu.roll(x, k, axis=1), identity))
        k *= 2
    return x


def _rmsnorm(x, g):
    ms = jnp.mean(x * x, axis=-1, keepdims=True)
    return (x * lax.rsqrt(ms + EPS)) * g


def _stage_a(x_ref, x_row, first, prev, cur, ng_ref, w_ref, wg_ref, h_ref):
    T = SEQ_TILE
    qk_prev, u_prev = prev[0], prev[1]
    qk_cur, u_cur, proj_cur, g_cur, vext_cur, qc_cur = cur

    def head():
        qk_cur[0:CONV_HIST, :] = jnp.where(first, 0.0, qk_prev[T:T + CONV_HIST, :])
        u_cur[0:POOL_HIST, :] = jnp.where(first, 0.0, u_prev[T:T + POOL_HIST, :])
        h_ref[...] = _rmsnorm(x_ref[x_row:x_row + T, :], ng_ref[...]).astype(BF16)

    def proj(c0):
        return jnp.dot(h_ref[...], w_ref[:, c0:c0 + GROUP_W], preferred_element_type=F32)

    def dot_q():
        qk_cur[CONV_HIST:CONV_HIST + T, 0:GROUP_W] = proj(Q0)

    def dot_k():
        qk_cur[CONV_HIST:CONV_HIST + T, GROUP_W:2 * GROUP_W] = proj(K0)

    def dot_u():
        u_cur[POOL_HIST:POOL_HIST + T, :] = proj(U0)

    def dot_to(src, dst):
        def run():
            proj_cur[:, dst:dst + GROUP_W] = proj(src)
        return run

    def dot_v():
        res = proj(V0)
        for hd in range(N_HEADS):
            vext_cur[:, hd * 2 * HEAD_D:hd * 2 * HEAD_D + HEAD_D] = (
                res[:, hd * HEAD_D:(hd + 1) * HEAD_D].astype(BF16))

    def dot_qc():
        qc_cur[...] = (proj(QC0) * (HEAD_D ** -0.5)).astype(BF16)

    def dot_gates():
        g_cur[...] = lax.dot_general(wg_ref[...], h_ref[...], (((1,), (1,)), ((), ())),
                                     preferred_element_type=F32)

    return [head, dot_q, dot_k, dot_u, dot_v, dot_to(O0, P_O), dot_to(ZA0, P_ZA),
            dot_to(ZB0, P_ZB), dot_qc, dot_to(ZC0, P_ZC), dot_gates]


def _stage_b(x_ref, x_row, first, t_seq, cur, o_ref, kmt_ref, vmx_ref, bg_ref, cw_ref, cb_ref, mng_ref,
             pw_ref, ps_ref, wo_ref, fng_ref, q_ref, k_ref, cn_ref, m_ref, y_ref, final):
    T, L = SEQ_TILE, CHUNK
    qk_ext, u_ext, proj_ref, g_ref, vext_ref, qc_ref = cur

    def conv(c0):
        xs = qk_ext[:, c0:c0 + GROUP_W]
        acc = cb_ref[:, c0:c0 + GROUP_W] + cw_ref[CONV_K - 1:CONV_K, c0:c0 + GROUP_W] * xs[CONV_HIST:]
        for j in range(CONV_K - 1):
            back = pltpu.roll(xs, CONV_K - 1 - j, axis=0)[CONV_HIST:]
            acc = acc + cw_ref[j:j + 1, c0:c0 + GROUP_W] * back
        return _silu(acc)

    def conv_q():
        q_ref[...] = conv(0)

    def conv_k():
        k_ref[...] = conv(GROUP_W) * (HEAD_D ** -0.5)

    def mlstm_chunk(c):
        def run():
            rowi = lax.broadcasted_iota(jnp.int32, (L, L), 0)
            coli = lax.broadcasted_iota(jnp.int32, (L, L), 1)
            causal = coli <= rowi
            r0 = c * L
            g8 = g_ref[0:2 * N_HEADS, r0:r0 + L] + bg_ref[:, 0:1]
            gi = g8
            logf = _log_sigmoid(pltpu.roll(g8, N_HEADS, axis=0))
            b = _scan_lanes(logf, jnp.add, 0.0)
            r = gi - b
            m_prev = m_ref[:, 0:1]
            if c == 0:
                m_prev = jnp.where(first, 0.0, m_prev)
            mm = jnp.maximum(_scan_lanes(r, jnp.maximum, -jnp.inf), m_prev)
            a = jnp.exp(m_prev - mm)
            e = jnp.exp(-(mm + b))
            m_ref[...] = jnp.broadcast_to((mm + b)[:, L - 1:L], m_ref.shape)
            cols = jnp.concatenate([mm, a, e, jnp.zeros((LANES - 6 * N_HEADS, L), F32)], axis=0).T
            mm_c, a_c, e_c = (cols[:, i * 2 * N_HEADS:(i + 1) * 2 * N_HEADS] for i in range(3))
            heads = [slice(hd * HEAD_D, (hd + 1) * HEAD_D) for hd in range(N_HEADS)]
            qs = [q_ref[r0:r0 + L, hs] for hs in heads]
            vxs = [vext_ref[r0:r0 + L, hd * 2 * HEAD_D:(hd + 1) * 2 * HEAD_D] for hd in range(N_HEADS)]
            s_bf, k_tw = [], []
            for hd, hs in enumerate(heads):
                arg = r[hd:hd + 1, :] - mm_c[:, hd:hd + 1]
                p = jnp.exp(jnp.where(causal, arg, -1e30))
                k_t = k_ref[r0:r0 + L, hs].T
                s = jnp.dot(qs[hd].astype(BF16), k_t.astype(BF16), preferred_element_type=F32) * p
                s_bf.append(s.astype(BF16))
                k_tw.append((k_t * p[L - 1:L, :]).astype(BF16))
            nds = []
            for hd in range(N_HEADS):
                cn = cn_ref[hd]
                if c == 0:
                    cn = jnp.where(first, 0.0, cn)
                lhs = jnp.concatenate([(a_c[:, hd:hd + 1] * qs[hd]).astype(BF16), s_bf[hd]], axis=1)
                rhs = jnp.concatenate([cn.astype(BF16), vxs[hd]], axis=0)
                nds.append(jnp.dot(lhs, rhs, preferred_element_type=F32))
                cn_ref[hd] = (a[hd:hd + 1, L - 1:L] * cn
                              + jnp.dot(k_tw[hd], vxs[hd], preferred_element_type=F32))
            for hd, hs in enumerate(heads):
                nd = nds[hd]
                hh = nd[:, 0:HEAD_D] / jnp.maximum(jnp.abs(nd[:, HEAD_D:HEAD_D + 1]), e_c[:, hd:hd + 1])
                hg = _sigmoid(proj_ref[r0:r0 + L, P_O + hd * HEAD_D:P_O + (hd + 1) * HEAD_D]) * hh
                hg = hg * lax.rsqrt(jnp.mean(hg * hg, axis=-1, keepdims=True) + EPS)
                hg = hg * mng_ref[:, hs]
                ya = hg * _silu(proj_ref[r0:r0 + L, P_ZA + hd * HEAD_D:P_ZA + (hd + 1) * HEAD_D])
                y_ref[r0:r0 + L, hs] = ya.astype(BF16)
        return run

    def pool():
        tg = (t_seq * T + lax.broadcasted_iota(jnp.int32, (T, LANES), 0) + 1).astype(F32)
        zero_w = jnp.zeros((LANES, LANES), BF16)
        for g0 in range(0, len(POOL_WINDOWS), 2):
            pooled = []
            for g in (g0, g0 + 1):
                win = POOL_WINDOWS[g]
                gs = slice(g * LANES, (g + 1) * LANES)
                ssum = u_ext[:, gs]
                sh = 1
                while sh < win:
                    ssum = ssum + pltpu.roll(ssum, sh, axis=0)
                    sh *= 2
                u = u_ext[POOL_HIST:POOL_HIST + T, gs]
                ssum = ssum[POOL_HIST:POOL_HIST + T]
                if win & (win - 1) == 0 and win <= POOL_HIST:
                    mean = jnp.concatenate(
                        [ssum[:POOL_HIST] / jnp.minimum(tg[:POOL_HIST], float(win)),
                         ssum[POOL_HIST:] * (1.0 / win)], axis=0)
                else:
                    mean = ssum / jnp.minimum(tg, float(win))
                pooled.append((mean - u).astype(BF16))
            w_a, w_b = pw_ref[g0].astype(BF16), pw_ref[g0 + 1].astype(BF16)
            w_pair = jnp.concatenate([jnp.concatenate([w_a, zero_w], axis=1),
                                      jnp.concatenate([zero_w, w_b], axis=1)], axis=0)
            yb = jnp.dot(jnp.concatenate(pooled, axis=1), w_pair, preferred_element_type=F32)
            yb = (yb * ps_ref[:, g0 * LANES:(g0 + 2) * LANES]
                  * _silu(proj_ref[:, P_ZB + g0 * LANES:P_ZB + (g0 + 2) * LANES]))
            y_ref[:, GROUP_W + g0 * LANES:GROUP_W + (g0 + 2) * LANES] = yb.astype(BF16)

    def attention():
        for hd in range(N_HEADS):
            hs = slice(hd * HEAD_D, (hd + 1) * HEAD_D)
            s = jnp.dot(qc_ref[:, hs], kmt_ref[hs, :], preferred_element_type=F32)
            pexp = jnp.exp(s - jnp.max(s, axis=-1, keepdims=True))
            pv = jnp.dot(pexp.astype(BF16), vmx_ref[hd], preferred_element_type=F32)
            yc = pv[:, 0:HEAD_D] / pv[:, HEAD_D:HEAD_D + 1]
            yc = yc * _silu(proj_ref[:, P_ZC + hd * HEAD_D:P_ZC + (hd + 1) * HEAD_D])
            y_ref[:, 2 * GROUP_W + hd * HEAD_D:2 * GROUP_W + (hd + 1) * HEAD_D] = yc.astype(BF16)

    def out_proj():
        rows = slice(x_row, x_row + T)
        for i in range(D_MODEL // GROUP_W):
            cs = slice(i * GROUP_W, (i + 1) * GROUP_W)
            o_ref[rows, cs] = x_ref[rows, cs] + jnp.dot(y_ref[...], wo_ref[i], preferred_element_type=F32)
        if final:
            o_ref[rows, :] = _rmsnorm(o_ref[rows, :], fng_ref[...])

    return dict(conv_q=conv_q, conv_k=conv_k, chunks=[mlstm_chunk(c) for c in range(T // L)],
                pool=pool, attention=attention, out_proj=out_proj)


def _layer_kernel(xp_ref, xn_ref, kmt_ref, vmx_ref, ng_ref, w_ref, wg_ref, bg_ref, cw_ref, cb_ref, mng_ref,
                  pw_ref, ps_ref, wo_ref, fng_ref, o_ref,
                  h_ref, qk0, u0, proj0, g0, vext0, qc0, qk1, u1, proj1, g1, vext1, qc1,
                  q_ref, k_ref, cn_ref, y_ref, m_ref, *, final, tiles_per_seq):
    T = SEQ_TILE
    k = pl.program_id(0)
    set0, set1 = (qk0, u0, proj0, g0, vext0, qc0), (qk1, u1, proj1, g1, vext1, qc1)
    a_args = (ng_ref, w_ref, wg_ref, h_ref)
    b_args = (kmt_ref, vmx_ref, bg_ref, cw_ref, cb_ref, mng_ref, pw_ref, ps_ref, wo_ref, fng_ref,
              q_ref, k_ref, cn_ref, m_ref, y_ref, final)

    @pl.when(k == 0)
    def _prologue():
        col = lax.broadcasted_iota(jnp.int32, vext0.shape, 1)
        for vext in (vext0, vext1):
            vext[...] = jnp.where(col % (2 * HEAD_D) == HEAD_D, 1.0, 0.0).astype(BF16)
        cn_ref[...] = jnp.zeros(cn_ref.shape, F32)
        m_ref[...] = jnp.zeros(m_ref.shape, F32)
        qk1[T:T + CONV_HIST, :] = jnp.zeros((CONV_HIST, 2 * GROUP_W), F32)
        u1[T:T + POOL_HIST, :] = jnp.zeros((POOL_HIST, GROUP_W), F32)
        for piece in _stage_a(xp_ref, 0, True, set1, set0, *a_args):
            piece()

    t_even = (2 * k) % tiles_per_seq
    first_even = t_even == 0
    first_next = (2 * k + 2) % tiles_per_seq == 0

    def half_step(x_row, first, t_seq, cur, nxt, xa_ref, xa_row, first_a):
        for piece in _stage_a(xa_ref, xa_row, first_a, cur, nxt, *a_args):
            piece()
        b = _stage_b(xp_ref, x_row, first, t_seq, cur, o_ref, *b_args)
        for phase in (b["conv_q"], b["conv_k"], *b["chunks"], b["pool"], b["attention"], b["out_proj"]):
            phase()

    half_step(0, first_even, t_even, set0, set1, xp_ref, T, False)
    half_step(T, False, t_even + 1, set1, set0, xn_ref, 0, first_next)


def _setup_kernel(wint_ref, wout_ref, mem_ref, g_ref, wkv_ref, wall_ref, wg_ref, wo_ref, kmt_ref, vmx_ref,
                  wb_ref):
    wall_ref[:, 0:GATE0] = wint_ref[0:GATE0, :].T.astype(BF16)
    wall_ref[:, GATE0:N_PROJ] = wint_ref[GATE0 + 2 * N_HEADS:N_IN, :].T.astype(BF16)
    wg_ref[0:2 * N_HEADS, :] = wint_ref[GATE0:GATE0 + 2 * N_HEADS, :].astype(BF16)
    wg_ref[2 * N_HEADS:GATE_ROWS, :] = jnp.zeros((GATE_ROWS - 2 * N_HEADS, wg_ref.shape[1]), BF16)
    for i in range(D_MODEL // GROUP_W):
        wo_ref[i] = wout_ref[:, i * GROUP_W:(i + 1) * GROUP_W].astype(BF16)

    @pl.when(pl.program_id(0) == 0)
    def _cast_weights():
        wb_ref[...] = wkv_ref[...].astype(BF16)

    mem = mem_ref[...]
    ms = jnp.mean(mem * mem, axis=-1, keepdims=True)
    nrm = mem * lax.rsqrt(ms + EPS)
    col = lax.broadcasted_iota(jnp.int32, (N_MEM, HEAD_D), 1)
    ones_col = jnp.where(col == 0, 1.0, 0.0).astype(BF16)
    for l in range(wb_ref.shape[0]):
        mn = (nrm * g_ref[l]).astype(BF16)
        kv = jnp.dot(mn, wb_ref[l], preferred_element_type=F32)
        for hd in range(N_HEADS):
            hs = slice(hd * HEAD_D, (hd + 1) * HEAD_D)
            kmt_ref[l, hs, :] = kv[:, hs].T.astype(BF16)
            vmx_ref[l, hd, :, 0:HEAD_D] = kv[:, GROUP_W + hd * HEAD_D:GROUP_W + (hd + 1) * HEAD_D].astype(BF16)
            vmx_ref[l, hd, :, HEAD_D:2 * HEAD_D] = ones_col


def _resident(shape, layer):
    nd = len(shape)
    return pl.BlockSpec((None,) + shape, lambda k: (layer,) + (0,) * nd, pipeline_mode=pl.Buffered(1))


def _setup(w_in, w_out, mem, mem_norm_g, w_kv):
    depth, batch = w_in.shape[0], mem.shape[0]
    assert batch % depth == 0
    steps = batch // depth
    rb_in, rb_out = D_MODEL // steps, 3 * GROUP_W // steps
    return pl.pallas_call(
        _setup_kernel,
        grid=(batch,),
        in_specs=[
            pl.BlockSpec((None, N_IN, rb_in), lambda i: (i // steps, 0, i % steps)),
            pl.BlockSpec((None, rb_out, D_MODEL), lambda i: (i // steps, i % steps, 0)),
            pl.BlockSpec((None, N_MEM, D_MODEL), lambda i: (i, 0, 0)),
            pl.BlockSpec((depth, 1, D_MODEL), lambda i: (0, 0, 0)),
            pl.BlockSpec((depth, D_MODEL, 2 * GROUP_W), lambda i: (0, 0, 0), pipeline_mode=pl.Buffered(1)),
        ],
        out_specs=[
            pl.BlockSpec((None, rb_in, N_PROJ), lambda i: (i // steps, i % steps, 0)),
            pl.BlockSpec((None, GATE_ROWS, rb_in), lambda i: (i // steps, 0, i % steps)),
            pl.BlockSpec((None, D_MODEL // GROUP_W, rb_out, GROUP_W), lambda i: (i // steps, 0, i % steps, 0)),
            pl.BlockSpec((depth, None, GROUP_W, N_MEM), lambda i: (0, i, 0, 0)),
            pl.BlockSpec((depth, None, N_HEADS, N_MEM, 2 * HEAD_D), lambda i: (0, i, 0, 0, 0)),
        ],
        out_shape=[
            jax.ShapeDtypeStruct((depth, D_MODEL, N_PROJ), BF16),
            jax.ShapeDtypeStruct((depth, GATE_ROWS, D_MODEL), BF16),
            jax.ShapeDtypeStruct((depth, D_MODEL // GROUP_W, 3 * GROUP_W, GROUP_W), BF16),
            jax.ShapeDtypeStruct((depth, batch, GROUP_W, N_MEM), BF16),
            jax.ShapeDtypeStruct((depth, batch, N_HEADS, N_MEM, 2 * HEAD_D), BF16),
        ],
        scratch_shapes=[pltpu.VMEM((depth, D_MODEL, 2 * GROUP_W), BF16)],
        compiler_params=pltpu.CompilerParams(
            dimension_semantics=("arbitrary",), vmem_limit_bytes=VMEM_LIMIT_BYTES),
        name="setup_weights_memkv",
    )(jnp.swapaxes(w_in, 1, 2), w_out, mem, mem_norm_g.reshape(depth, 1, D_MODEL), w_kv)


def _layer(layer, final, x, kmt, vmx, ng, w_all, w_g, b_g, conv_w, conv_b, mng, pool_w, pool_scale, w_out, fng):
    batch, seq, _ = x.shape
    T = SEQ_TILE
    assert seq % (2 * T) == 0 and T % CHUNK == 0
    tiles_per_seq = seq // T
    pairs_per_seq = tiles_per_seq // 2
    last_tile = batch * tiles_per_seq - 1

    def next_tile(k):
        nt = jnp.minimum(2 * k + 2, last_tile)
        return nt // tiles_per_seq, nt % tiles_per_seq, 0

    kern = functools.partial(_layer_kernel, final=final, tiles_per_seq=tiles_per_seq)
    set_scratch = [
        pltpu.VMEM((CONV_HIST + T, 2 * GROUP_W), F32),
        pltpu.VMEM((POOL_HIST + T, GROUP_W), F32),
        pltpu.VMEM((T, P_COLS), F32),
        pltpu.VMEM((GATE_ROWS, T), F32),
        pltpu.VMEM((T, N_HEADS * 2 * HEAD_D), BF16),
        pltpu.VMEM((T, GROUP_W), BF16),
    ]
    return pl.pallas_call(
        kern,
        grid=(batch * pairs_per_seq,),
        in_specs=[
            pl.BlockSpec((None, 2 * T, D_MODEL), lambda k: (k // pairs_per_seq, k % pairs_per_seq, 0)),
            pl.BlockSpec((None, T, D_MODEL), next_tile),
            pl.BlockSpec((None, None, GROUP_W, N_MEM), lambda k: (layer, k // pairs_per_seq, 0, 0)),
            pl.BlockSpec((None, None, N_HEADS, N_MEM, 2 * HEAD_D),
                         lambda k: (layer, k // pairs_per_seq, 0, 0, 0)),
            _resident((1, D_MODEL), layer),
            _resident((D_MODEL, N_PROJ), layer),
            _resident((GATE_ROWS, D_MODEL), layer),
            _resident((2 * N_HEADS, LANES), layer),
            _resident((CONV_K, 2 * GROUP_W), layer),
            _resident((1, 2 * GROUP_W), layer),
            _resident((1, GROUP_W), layer),
            _resident((len(POOL_WINDOWS), LANES, LANES), layer),
            _resident((1, GROUP_W), layer),
            _resident((D_MODEL // GROUP_W, 3 * GROUP_W, GROUP_W), layer),
            pl.BlockSpec((1, D_MODEL), lambda k: (0, 0), pipeline_mode=pl.Buffered(1)),
        ],
        out_specs=pl.BlockSpec((None, 2 * T, D_MODEL), lambda k: (k // pairs_per_seq, k % pairs_per_seq, 0)),
        out_shape=jax.ShapeDtypeStruct(x.shape, F32),
        scratch_shapes=[pltpu.VMEM((T, D_MODEL), BF16)] + set_scratch + set_scratch + [
            pltpu.VMEM((T, GROUP_W), F32),
            pltpu.VMEM((T, GROUP_W), F32),
            pltpu.VMEM((N_HEADS, HEAD_D, 2 * HEAD_D), F32),
            pltpu.VMEM((T, 3 * GROUP_W), BF16),
            pltpu.VMEM((2 * N_HEADS, LANES), F32),
        ],
        compiler_params=pltpu.CompilerParams(
            dimension_semantics=("arbitrary",), vmem_limit_bytes=VMEM_LIMIT_BYTES),
        name="trunk_layer_final" if final else "trunk_layer",
    )(x, x, kmt, vmx, ng, w_all, w_g, b_g, conv_w, conv_b, mng, pool_w, pool_scale, w_out, fng)


def kernel(x, mem, norm_g, w_in, b_gates, conv_w, conv_b, mlstm_norm_g, pool_w, pool_scale,
           mem_norm_g, w_mem_kv, w_out, final_norm_g):
    depth = w_in.shape[0]
    w_all, w_g, w_out_b, kmt, vmx = _setup(w_in, w_out, mem, mem_norm_g, w_mem_kv)
    b_g = jnp.broadcast_to(b_gates[:, :, None], (depth, 2 * N_HEADS, LANES))
    args = (kmt, vmx, norm_g.reshape(depth, 1, D_MODEL), w_all, w_g, b_g,
            conv_w, conv_b.reshape(depth, 1, 2 * GROUP_W), mlstm_norm_g.reshape(depth, 1, GROUP_W),
            pool_w, pool_scale.reshape(depth, 1, GROUP_W), w_out_b, final_norm_g.reshape(1, D_MODEL))
    for l in range(depth):
        x = _layer(l, l == depth - 1, x, *args)
    return x
```

```python
import functools
import math

import jax
import jax.numpy as jnp
from jax import lax
from jax.experimental import pallas as pl
from jax.experimental.pallas import tpu as pltpu

D_MODEL = 1024
GROUP_W = 512
N_HEADS = 4
HEAD_D = 128
CONV_K = 4
POOL_WINDOWS = (2, 4, 8, 16)
N_MEM = 256
EPS = 1e-6

SEQ_TILE = 256
CHUNK = 256
CONV_HIST = 8
POOL_HIST = 16
LANES = 128
VMEM_LIMIT_BYTES = 52 * 1024 * 1024

N_IN = 2 * GROUP_W + 3 * GROUP_W + 2 * N_HEADS + 4 * GROUP_W
GATE0 = 5 * GROUP_W
Q0, K0, V0, O0, ZA0, U0, ZB0, QC0, ZC0 = (i * GROUP_W for i in range(9))
N_PROJ = 9 * GROUP_W
GATE_ROWS = 16
P_O, P_ZA, P_ZB, P_ZC = (i * GROUP_W for i in range(4))
P_COLS = 4 * GROUP_W

BF16 = jnp.bfloat16
F32 = jnp.float32


def _sigmoid(x):
    return 0.5 * jnp.tanh(0.5 * x) + 0.5


def _silu(x):
    hx = 0.5 * x
    return hx + hx * jnp.tanh(hx)


def _log_sigmoid(x):
    return jnp.minimum(x, 0.0) - jnp.log1p(jnp.exp(-jnp.abs(x)))


def _scan_lanes(x, op, identity):
    n = x.shape[1]
    lanes = lax.broadcasted_iota(jnp.int32, x.shape, 1)
    k = 1
    while k < n:
        x = op(x, jnp.where(lanes >= k, pltpu.roll(x, k, axis=1), identity))
        k *= 2
    return x


def _rmsnorm(x, g):
    ms = jnp.mean(x * x, axis=-1, keepdims=True)
    return (x * lax.rsqrt(ms + EPS)) * g


def _stage_a(x_ref, x_row, first, prev, cur, ng_ref, w_ref, wg_ref, h_ref):
    T = SEQ_TILE
    qk_prev, u_prev = prev[0], prev[1]
    qk_cur, u_cur, proj_cur, g_cur, vext_cur, qc_cur = cur

    def head():
        qk_cur[0:CONV_HIST, :] = jnp.where(first, 0.0, qk_prev[T:T + CONV_HIST, :])
        u_cur[0:POOL_HIST, :] = jnp.where(first, 0.0, u_prev[T:T + POOL_HIST, :])
        h_ref[...] = _rmsnorm(x_ref[x_row:x_row + T, :], ng_ref[...]).astype(BF16)

    def proj(c0):
        return jnp.dot(h_ref[...], w_ref[:, c0:c0 + GROUP_W], preferred_element_type=F32)

    def dot_q():
        qk_cur[CONV_HIST:CONV_HIST + T, 0:GROUP_W] = proj(Q0)

    def dot_k():
        qk_cur[CONV_HIST:CONV_HIST + T, GROUP_W:2 * GROUP_W] = proj(K0)

    def dot_u():
        u_cur[POOL_HIST:POOL_HIST + T, :] = proj(U0)

    def dot_to(src, dst):
        def run():
            proj_cur[:, dst:dst + GROUP_W] = proj(src)
        return run

    def dot_v():
        res = proj(V0)
        for hd in range(N_HEADS):
            vext_cur[:, hd * 2 * HEAD_D:hd * 2 * HEAD_D + HEAD_D] = (
                res[:, hd * HEAD_D:(hd + 1) * HEAD_D].astype(BF16))

    def dot_qc():
        qc_cur[...] = (proj(QC0) * (HEAD_D ** -0.5)).astype(BF16)

    def dot_gates():
        g_cur[...] = lax.dot_general(wg_ref[...], h_ref[...], (((1,), (1,)), ((), ())),
                                     preferred_element_type=F32)

    return [head, dot_q, dot_k, dot_u, dot_v, dot_to(O0, P_O), dot_to(ZA0, P_ZA),
            dot_to(ZB0, P_ZB), dot_qc, dot_to(ZC0, P_ZC), dot_gates]


def _stage_b(x_ref, x_row, first, t_seq, cur, o_ref, kmt_ref, vmx_ref, bg_ref, cw_ref, cb_ref, mng_ref,
             pw_ref, ps_ref, wo_ref, fng_ref, q_ref, k_ref, cn_ref, m_ref, y_ref, final):
    T, L = SEQ_TILE, CHUNK
    qk_ext, u_ext, proj_ref, g_ref, vext_ref, qc_ref = cur

    def conv(c0):
        xs = qk_ext[:, c0:c0 + GROUP_W]
        acc = cb_ref[:, c0:c0 + GROUP_W] + cw_ref[CONV_K - 1:CONV_K, c0:c0 + GROUP_W] * xs[CONV_HIST:]
        for j in range(CONV_K - 1):
            back = pltpu.roll(xs, CONV_K - 1 - j, axis=0)[CONV_HIST:]
            acc = acc + cw_ref[j:j + 1, c0:c0 + GROUP_W] * back
        return _silu(acc)

    def conv_q():
        q_ref[...] = conv(0)

    def conv_k():
        k_ref[...] = conv(GROUP_W)

    def mlstm_chunk(c):
        def run():
            rowi = lax.broadcasted_iota(jnp.int32, (L, L), 0)
            coli = lax.broadcasted_iota(jnp.int32, (L, L), 1)
            causal = coli <= rowi
            r0 = c * L
            g8 = g_ref[0:2 * N_HEADS, r0:r0 + L] + bg_ref[:, 0:1]
            gi = g8
            logf = _log_sigmoid(pltpu.roll(g8, N_HEADS, axis=0))
            b = _scan_lanes(logf, jnp.add, 0.0)
            r = gi - b
            m_prev = m_ref[:, 0:1]
            if c == 0:
                m_prev = jnp.where(first, 0.0, m_prev)
            mm = jnp.maximum(_scan_lanes(r, jnp.maximum, -jnp.inf), m_prev)
            a = jnp.exp(m_prev - mm)
            e = jnp.exp(-(mm + b))
            m_ref[...] = jnp.broadcast_to((mm + b)[:, L - 1:L], m_ref.shape)
            mm_k = mm + 0.5 * math.log(HEAD_D)
            cols = jnp.concatenate([mm_k, a, e, jnp.zeros((LANES - 6 * N_HEADS, L), F32)], axis=0).T
            mm_c, a_c, e_c = (cols[:, i * 2 * N_HEADS:(i + 1) * 2 * N_HEADS] for i in range(3))
            heads = [slice(hd * HEAD_D, (hd + 1) * HEAD_D) for hd in range(N_HEADS)]
            qs = [q_ref[r0:r0 + L, hs] for hs in heads]
            vxs = [vext_ref[r0:r0 + L, hd * 2 * HEAD_D:(hd + 1) * 2 * HEAD_D] for hd in range(N_HEADS)]
            s_bf, k_tw = [], []
            for hd, hs in enumerate(heads):
                arg = r[hd:hd + 1, :] - mm_c[:, hd:hd + 1]
                p = jnp.exp(jnp.where(causal, arg, -1e30))
                k_t = k_ref[r0:r0 + L, hs].T
                s = jnp.dot(qs[hd].astype(BF16), k_t.astype(BF16), preferred_element_type=F32) * p
                s_bf.append(s.astype(BF16))
                k_tw.append((k_t * p[L - 1:L, :]).astype(BF16))
            nds = []
            for hd in range(N_HEADS):
                cn = cn_ref[hd]
                if c == 0:
                    cn = jnp.where(first, 0.0, cn)
                lhs = jnp.concatenate([(a_c[:, hd:hd + 1] * qs[hd]).astype(BF16), s_bf[hd]], axis=1)
                rhs = jnp.concatenate([cn.astype(BF16), vxs[hd]], axis=0)
                nds.append(jnp.dot(lhs, rhs, preferred_element_type=F32))
                cn_ref[hd] = (a[hd:hd + 1, L - 1:L] * cn
                              + jnp.dot(k_tw[hd], vxs[hd], preferred_element_type=F32))
            for hd, hs in enumerate(heads):
                nd = nds[hd]
                hh = nd[:, 0:HEAD_D] / jnp.maximum(jnp.abs(nd[:, HEAD_D:HEAD_D + 1]), e_c[:, hd:hd + 1])
                hg = _sigmoid(proj_ref[r0:r0 + L, P_O + hd * HEAD_D:P_O + (hd + 1) * HEAD_D]) * hh
                hg = hg * lax.rsqrt(jnp.mean(hg * hg, axis=-1, keepdims=True) + EPS)
                hg = hg * mng_ref[:, hs]
                ya = hg * _silu(proj_ref[r0:r0 + L, P_ZA + hd * HEAD_D:P_ZA + (hd + 1) * HEAD_D])
                y_ref[r0:r0 + L, hs] = ya.astype(BF16)
        return run

    def pool():
        tg = (t_seq * T + lax.broadcasted_iota(jnp.int32, (T, LANES), 0) + 1).astype(F32)
        zero_w = jnp.zeros((LANES, LANES), BF16)
        for g0 in range(0, len(POOL_WINDOWS), 2):
            pooled = []
            for g in (g0, g0 + 1):
                win = POOL_WINDOWS[g]
                gs = slice(g * LANES, (g + 1) * LANES)
                ssum = u_ext[:, gs]
                sh = 1
                while sh < win:
                    ssum = ssum + pltpu.roll(ssum, sh, axis=0)
                    sh *= 2
                u = u_ext[POOL_HIST:POOL_HIST + T, gs]
                ssum = ssum[POOL_HIST:POOL_HIST + T]
                if win & (win - 1) == 0 and win <= POOL_HIST:
                    mean = jnp.concatenate(
                        [ssum[:POOL_HIST] / jnp.minimum(tg[:POOL_HIST], float(win)),
                         ssum[POOL_HIST:] * (1.0 / win)], axis=0)
                else:
                    mean = ssum / jnp.minimum(tg, float(win))
                pooled.append((mean - u).astype(BF16))
            w_a, w_b = pw_ref[g0].astype(BF16), pw_ref[g0 + 1].astype(BF16)
            w_pair = jnp.concatenate([jnp.concatenate([w_a, zero_w], axis=1),
                                      jnp.concatenate([zero_w, w_b], axis=1)], axis=0)
            yb = jnp.dot(jnp.concatenate(pooled, axis=1), w_pair, preferred_element_type=F32)
            yb = (yb * ps_ref[:, g0 * LANES:(g0 + 2) * LANES]
                  * _silu(proj_ref[:, P_ZB + g0 * LANES:P_ZB + (g0 + 2) * LANES]))
            y_ref[:, GROUP_W + g0 * LANES:GROUP_W + (g0 + 2) * LANES] = yb.astype(BF16)

    def attention():
        for hd in range(N_HEADS):
            hs = slice(hd * HEAD_D, (hd + 1) * HEAD_D)
            s = jnp.dot(qc_ref[:, hs], kmt_ref[hs, :], preferred_element_type=F32)
            pexp = jnp.exp(s - jnp.max(s, axis=-1, keepdims=True))
            pv = jnp.dot(pexp.astype(BF16), vmx_ref[hd], preferred_element_type=F32)
            yc = pv[:, 0:HEAD_D] / pv[:, HEAD_D:HEAD_D + 1]
            yc = yc * _silu(proj_ref[:, P_ZC + hd * HEAD_D:P_ZC + (hd + 1) * HEAD_D])
            y_ref[:, 2 * GROUP_W + hd * HEAD_D:2 * GROUP_W + (hd + 1) * HEAD_D] = yc.astype(BF16)

    def out_proj():
        rows = slice(x_row, x_row + T)
        for i in range(D_MODEL // GROUP_W):
            cs = slice(i * GROUP_W, (i + 1) * GROUP_W)
            o_ref[rows, cs] = x_ref[rows, cs] + jnp.dot(y_ref[...], wo_ref[i], preferred_element_type=F32)
        if final:
            o_ref[rows, :] = _rmsnorm(o_ref[rows, :], fng_ref[...])

    return dict(conv_q=conv_q, conv_k=conv_k, chunks=[mlstm_chunk(c) for c in range(T // L)],
                pool=pool, attention=attention, out_proj=out_proj)


def _layer_kernel(xp_ref, xn_ref, kmt_ref, vmx_ref, ng_ref, w_ref, wg_ref, bg_ref, cw_ref, cb_ref, mng_ref,
                  pw_ref, ps_ref, wo_ref, fng_ref, o_ref,
                  h_ref, qk0, u0, proj0, g0, vext0, qc0, qk1, u1, proj1, g1, vext1, qc1,
                  q_ref, k_ref, cn_ref, y_ref, m_ref, *, final, tiles_per_seq):
    T = SEQ_TILE
    k = pl.program_id(0)
    set0, set1 = (qk0, u0, proj0, g0, vext0, qc0), (qk1, u1, proj1, g1, vext1, qc1)
    a_args = (ng_ref, w_ref, wg_ref, h_ref)
    b_args = (kmt_ref, vmx_ref, bg_ref, cw_ref, cb_ref, mng_ref, pw_ref, ps_ref, wo_ref, fng_ref,
              q_ref, k_ref, cn_ref, m_ref, y_ref, final)

    @pl.when(k == 0)
    def _prologue():
        col = lax.broadcasted_iota(jnp.int32, vext0.shape, 1)
        for vext in (vext0, vext1):
            vext[...] = jnp.where(col % (2 * HEAD_D) == HEAD_D, 1.0, 0.0).astype(BF16)
        cn_ref[...] = jnp.zeros(cn_ref.shape, F32)
        m_ref[...] = jnp.zeros(m_ref.shape, F32)
        qk1[T:T + CONV_HIST, :] = jnp.zeros((CONV_HIST, 2 * GROUP_W), F32)
        u1[T:T + POOL_HIST, :] = jnp.zeros((POOL_HIST, GROUP_W), F32)
        for piece in _stage_a(xp_ref, 0, True, set1, set0, *a_args):
            piece()

    t_even = (2 * k) % tiles_per_seq
    first_even = t_even == 0
    first_next = (2 * k + 2) % tiles_per_seq == 0

    def half_step(x_row, first, t_seq, cur, nxt, xa_ref, xa_row, first_a):
        for piece in _stage_a(xa_ref, xa_row, first_a, cur, nxt, *a_args):
            piece()
        b = _stage_b(xp_ref, x_row, first, t_seq, cur, o_ref, *b_args)
        for phase in (b["conv_q"], b["conv_k"], *b["chunks"], b["pool"], b["attention"], b["out_proj"]):
            phase()

    half_step(0, first_even, t_even, set0, set1, xp_ref, T, False)
    half_step(T, False, t_even + 1, set1, set0, xn_ref, 0, first_next)


def _setup_kernel(wint_ref, wout_ref, mem_ref, g_ref, wkv_ref, wall_ref, wg_ref, wo_ref, kmt_ref, vmx_ref,
                  wb_ref):
    wall_ref[:, 0:GATE0] = wint_ref[0:GATE0, :].T.astype(BF16)
    wall_ref[:, GATE0:N_PROJ] = wint_ref[GATE0 + 2 * N_HEADS:N_IN, :].T.astype(BF16)
    wg_ref[0:2 * N_HEADS, :] = wint_ref[GATE0:GATE0 + 2 * N_HEADS, :].astype(BF16)
    wg_ref[2 * N_HEADS:GATE_ROWS, :] = jnp.zeros((GATE_ROWS - 2 * N_HEADS, wg_ref.shape[1]), BF16)
    for i in range(D_MODEL // GROUP_W):
        wo_ref[i] = wout_ref[:, i * GROUP_W:(i + 1) * GROUP_W].astype(BF16)

    @pl.when(pl.program_id(0) == 0)
    def _cast_weights():
        wb_ref[...] = wkv_ref[...].astype(BF16)

    mem = mem_ref[...]
    ms = jnp.mean(mem * mem, axis=-1, keepdims=True)
    nrm = mem * lax.rsqrt(ms + EPS)
    col = lax.broadcasted_iota(jnp.int32, (N_MEM, HEAD_D), 1)
    ones_col = jnp.where(col == 0, 1.0, 0.0).astype(BF16)
    for l in range(wb_ref.shape[0]):
        mn = (nrm * g_ref[l]).astype(BF16)
        kv = jnp.dot(mn, wb_ref[l], preferred_element_type=F32)
        for hd in range(N_HEADS):
            hs = slice(hd * HEAD_D, (hd + 1) * HEAD_D)
            kmt_ref[l, hs, :] = kv[:, hs].T.astype(BF16)
            vmx_ref[l, hd, :, 0:HEAD_D] = kv[:, GROUP_W + hd * HEAD_D:GROUP_W + (hd + 1) * HEAD_D].astype(BF16)
            vmx_ref[l, hd, :, HEAD_D:2 * HEAD_D] = ones_col


def _resident(shape, layer):
    nd = len(shape)
    return pl.BlockSpec((None,) + shape, lambda k: (layer,) + (0,) * nd, pipeline_mode=pl.Buffered(1))


def _setup(w_in, w_out, mem, mem_norm_g, w_kv):
    depth, batch = w_in.shape[0], mem.shape[0]
    assert batch % depth == 0
    steps = batch // depth
    rb_in, rb_out = D_MODEL // steps, 3 * GROUP_W // steps
    return pl.pallas_call(
        _setup_kernel,
        grid=(batch,),
        in_specs=[
            pl.BlockSpec((None, N_IN, rb_in), lambda i: (i // steps, 0, i % steps)),
            pl.BlockSpec((None, rb_out, D_MODEL), lambda i: (i // steps, i % steps, 0)),
            pl.BlockSpec((None, N_MEM, D_MODEL), lambda i: (i, 0, 0)),
            pl.BlockSpec((depth, 1, D_MODEL), lambda i: (0, 0, 0)),
            pl.BlockSpec((depth, D_MODEL, 2 * GROUP_W), lambda i: (0, 0, 0), pipeline_mode=pl.Buffered(1)),
        ],
        out_specs=[
            pl.BlockSpec((None, rb_in, N_PROJ), lambda i: (i // steps, i % steps, 0)),
            pl.BlockSpec((None, GATE_ROWS, rb_in), lambda i: (i // steps, 0, i % steps)),
            pl.BlockSpec((None, D_MODEL // GROUP_W, rb_out, GROUP_W), lambda i: (i // steps, 0, i % steps, 0)),
            pl.BlockSpec((depth, None, GROUP_W, N_MEM), lambda i: (0, i, 0, 0)),
            pl.BlockSpec((depth, None, N_HEADS, N_MEM, 2 * HEAD_D), lambda i: (0, i, 0, 0, 0)),
        ],
        out_shape=[
            jax.ShapeDtypeStruct((depth, D_MODEL, N_PROJ), BF16),
            jax.ShapeDtypeStruct((depth, GATE_ROWS, D_MODEL), BF16),
            jax.ShapeDtypeStruct((depth, D_MODEL // GROUP_W, 3 * GROUP_W, GROUP_W), BF16),
            jax.ShapeDtypeStruct((depth, batch, GROUP_W, N_MEM), BF16),
            jax.ShapeDtypeStruct((depth, batch, N_HEADS, N_MEM, 2 * HEAD_D), BF16),
        ],
        scratch_shapes=[pltpu.VMEM((depth, D_MODEL, 2 * GROUP_W), BF16)],
        compiler_params=pltpu.CompilerParams(
            dimension_semantics=("arbitrary",), vmem_limit_bytes=VMEM_LIMIT_BYTES),
        name="setup_weights_memkv",
    )(jnp.swapaxes(w_in, 1, 2), w_out, mem, mem_norm_g.reshape(depth, 1, D_MODEL), w_kv)


def _layer(layer, final, x, kmt, vmx, ng, w_all, w_g, b_g, conv_w, conv_b, mng, pool_w, pool_scale, w_out, fng):
    batch, seq, _ = x.shape
    T = SEQ_TILE
    assert seq % (2 * T) == 0 and T % CHUNK == 0
    tiles_per_seq = seq // T
    pairs_per_seq = tiles_per_seq // 2
    last_tile = batch * tiles_per_seq - 1

    def next_tile(k):
        nt = jnp.minimum(2 * k + 2, last_tile)
        return nt // tiles_per_seq, nt % tiles_per_seq, 0

    kern = functools.partial(_layer_kernel, final=final, tiles_per_seq=tiles_per_seq)
    set_scratch = [
        pltpu.VMEM((CONV_HIST + T, 2 * GROUP_W), F32),
        pltpu.VMEM((POOL_HIST + T, GROUP_W), F32),
        pltpu.VMEM((T, P_COLS), F32),
        pltpu.VMEM((GATE_ROWS, T), F32),
        pltpu.VMEM((T, N_HEADS * 2 * HEAD_D), BF16),
        pltpu.VMEM((T, GROUP_W), BF16),
    ]
    return pl.pallas_call(
        kern,
        grid=(batch * pairs_per_seq,),
        in_specs=[
            pl.BlockSpec((None, 2 * T, D_MODEL), lambda k: (k // pairs_per_seq, k % pairs_per_seq, 0)),
            pl.BlockSpec((None, T, D_MODEL), next_tile),
            pl.BlockSpec((None, None, GROUP_W, N_MEM), lambda k: (layer, k // pairs_per_seq, 0, 0)),
            pl.BlockSpec((None, None, N_HEADS, N_MEM, 2 * HEAD_D),
                         lambda k: (layer, k // pairs_per_seq, 0, 0, 0)),
            _resident((1, D_MODEL), layer),
            _resident((D_MODEL, N_PROJ), layer),
            _resident((GATE_ROWS, D_MODEL), layer),
            _resident((2 * N_HEADS, LANES), layer),
            _resident((CONV_K, 2 * GROUP_W), layer),
            _resident((1, 2 * GROUP_W), layer),
            _resident((1, GROUP_W), layer),
            _resident((len(POOL_WINDOWS), LANES, LANES), layer),
            _resident((1, GROUP_W), layer),
            _resident((D_MODEL // GROUP_W, 3 * GROUP_W, GROUP_W), layer),
            pl.BlockSpec((1, D_MODEL), lambda k: (0, 0), pipeline_mode=pl.Buffered(1)),
        ],
        out_specs=pl.BlockSpec((None, 2 * T, D_MODEL), lambda k: (k // pairs_per_seq, k % pairs_per_seq, 0)),
        out_shape=jax.ShapeDtypeStruct(x.shape, F32),
        scratch_shapes=[pltpu.VMEM((T, D_MODEL), BF16)] + set_scratch + set_scratch + [
            pltpu.VMEM((T, GROUP_W), F32),
            pltpu.VMEM((T, GROUP_W), F32),
            pltpu.VMEM((N_HEADS, HEAD_D, 2 * HEAD_D), F32),
            pltpu.VMEM((T, 3 * GROUP_W), BF16),
            pltpu.VMEM((2 * N_HEADS, LANES), F32),
        ],
        compiler_params=pltpu.CompilerParams(
            dimension_semantics=("arbitrary",), vmem_limit_bytes=VMEM_LIMIT_BYTES),
        name="trunk_layer_final" if final else "trunk_layer",
    )(x, x, kmt, vmx, ng, w_all, w_g, b_g, conv_w, conv_b, mng, pool_w, pool_scale, w_out, fng)


def kernel(x, mem, norm_g, w_in, b_gates, conv_w, conv_b, mlstm_norm_g, pool_w, pool_scale,
           mem_norm_g, w_mem_kv, w_out, final_norm_g):
    depth = w_in.shape[0]
    w_all, w_g, w_out_b, kmt, vmx = _setup(w_in, w_out, mem, mem_norm_g, w_mem_kv)
    b_g = jnp.broadcast_to(b_gates[:, :, None], (depth, 2 * N_HEADS, LANES))
    args = (kmt, vmx, norm_g.reshape(depth, 1, D_MODEL), w_all, w_g, b_g,
            conv_w, conv_b.reshape(depth, 1, 2 * GROUP_W), mlstm_norm_g.reshape(depth, 1, GROUP_W),
            pool_w, pool_scale.reshape(depth, 1, GROUP_W), w_out_b, final_norm_g.reshape(1, D_MODEL))
    for l in range(depth):
        x = _layer(l, l == depth - 1, x, *args)
    return x
```

```python
import functools
import math

import jax
import jax.numpy as jnp
from jax import lax
from jax.experimental import pallas as pl
from jax.experimental.pallas import tpu as pltpu

D_MODEL = 1024
GROUP_W = 512
N_HEADS = 4
HEAD_D = 128
CONV_K = 4
POOL_WINDOWS = (2, 4, 8, 16)
N_MEM = 256
EPS = 1e-6

SEQ_TILE = 256
CHUNK = 256
CONV_HIST = 8
CONV_ROWS = 64
POOL_HIST = 16
LANES = 128
VMEM_LIMIT_BYTES = 52 * 1024 * 1024

N_IN = 2 * GROUP_W + 3 * GROUP_W + 2 * N_HEADS + 4 * GROUP_W
GATE0 = 5 * GROUP_W
Q0, K0, V0, O0, ZA0, U0, ZB0, QC0, ZC0 = (i * GROUP_W for i in range(9))
N_PROJ = 9 * GROUP_W
GATE_ROWS = 16
P_O, P_ZA, P_ZB, P_ZC = (i * GROUP_W for i in range(4))
P_COLS = 4 * GROUP_W

BF16 = jnp.bfloat16
F32 = jnp.float32


def _sigmoid(x):
    return 0.5 * jnp.tanh(0.5 * x) + 0.5


def _silu(x):
    hx = 0.5 * x
    return hx + hx * jnp.tanh(hx)


def _log_sigmoid(x):
    return jnp.minimum(x, 0.0) - jnp.log1p(jnp.exp(-jnp.abs(x)))


def _scan_lanes(x, op, identity):
    n = x.shape[1]
    lanes = lax.broadcasted_iota(jnp.int32, x.shape, 1)
    k = 1
    while k < n:
        x = op(x, jnp.where(lanes >= k, pltpu.roll(x, k, axis=1), identity))
        k *= 2
    return x


def _rmsnorm(x, g):
    ms = jnp.mean(x * x, axis=-1, keepdims=True)
    return (x * lax.rsqrt(ms + EPS)) * g


def _stage_a(x_ref, x_row, first, prev, cur, ng_ref, w_ref, wg_ref, h_ref):
    T = SEQ_TILE
    qk_prev, u_prev = prev[0], prev[1]
    qk_cur, u_cur, proj_cur, g_cur, vext_cur, qc_cur = cur

    def head():
        qk_cur[0:CONV_HIST, :] = jnp.where(first, 0.0, qk_prev[T:T + CONV_HIST, :])
        u_cur[0:POOL_HIST, :] = jnp.where(first, 0.0, u_prev[T:T + POOL_HIST, :])
        h_ref[...] = _rmsnorm(x_ref[x_row:x_row + T, :], ng_ref[...]).astype(BF16)

    def proj(c0):
        return jnp.dot(h_ref[...], w_ref[:, c0:c0 + GROUP_W], preferred_element_type=F32)

    def dot_q():
        qk_cur[CONV_HIST:CONV_HIST + T, 0:GROUP_W] = proj(Q0)

    def dot_k():
        qk_cur[CONV_HIST:CONV_HIST + T, GROUP_W:2 * GROUP_W] = proj(K0)

    def dot_u():
        u_cur[POOL_HIST:POOL_HIST + T, :] = proj(U0)

    def dot_to(src, dst):
        def run():
            proj_cur[:, dst:dst + GROUP_W] = proj(src)
        return run

    def dot_v():
        res = proj(V0)
        for hd in range(N_HEADS):
            vext_cur[:, hd * 2 * HEAD_D:hd * 2 * HEAD_D + HEAD_D] = (
                res[:, hd * HEAD_D:(hd + 1) * HEAD_D].astype(BF16))

    def dot_qc():
        qc_cur[...] = (proj(QC0) * (HEAD_D ** -0.5)).astype(BF16)

    def dot_gates():
        g_cur[...] = lax.dot_general(wg_ref[...], h_ref[...], (((1,), (1,)), ((), ())),
                                     preferred_element_type=F32)

    return [head, dot_q, dot_k, dot_u, dot_v, dot_to(O0, P_O), dot_to(ZA0, P_ZA),
            dot_to(ZB0, P_ZB), dot_qc, dot_to(ZC0, P_ZC), dot_gates]


def _stage_b(x_ref, x_row, first, t_seq, cur, o_ref, kmt_ref, vmx_ref, bg_ref, cw_ref, cb_ref, mng_ref,
             pw_ref, ps_ref, wo_ref, fng_ref, q_ref, k_ref, cn_ref, m_ref, y_ref, final):
    T, L = SEQ_TILE, CHUNK
    qk_ext, u_ext, proj_ref, g_ref, vext_ref, qc_ref = cur

    def conv(c0, dst_ref):
        for cb0 in range(0, GROUP_W, LANES):
            cs = slice(c0 + cb0, c0 + cb0 + LANES)
            for r0 in range(0, T, CONV_ROWS):
                xs = qk_ext[r0:r0 + CONV_HIST + CONV_ROWS, cs]
                acc = cb_ref[:, cs] + cw_ref[CONV_K - 1:CONV_K, cs] * xs[CONV_HIST:]
                for j in range(CONV_K - 1):
                    back = pltpu.roll(xs, CONV_K - 1 - j, axis=0)[CONV_HIST:]
                    acc = acc + cw_ref[j:j + 1, cs] * back
                dst_ref[r0:r0 + CONV_ROWS, cb0:cb0 + LANES] = _silu(acc)

    def conv_q():
        conv(0, q_ref)

    def conv_k():
        conv(GROUP_W, k_ref)

    def mlstm_chunk(c):
        def run():
            rowi = lax.broadcasted_iota(jnp.int32, (L, L), 0)
            coli = lax.broadcasted_iota(jnp.int32, (L, L), 1)
            causal = coli <= rowi
            r0 = c * L
            g8 = g_ref[0:2 * N_HEADS, r0:r0 + L] + bg_ref[:, 0:1]
            gi = g8
            logf = _log_sigmoid(pltpu.roll(g8, N_HEADS, axis=0))
            b = _scan_lanes(logf, jnp.add, 0.0)
            r = gi - b
            m_prev = m_ref[:, 0:1]
            if c == 0:
                m_prev = jnp.where(first, 0.0, m_prev)
            mm = jnp.maximum(_scan_lanes(r, jnp.maximum, -jnp.inf), m_prev)
            a = jnp.exp(m_prev - mm)
            e = jnp.exp(-(mm + b))
            m_ref[...] = jnp.broadcast_to((mm + b)[:, L - 1:L], m_ref.shape)
            mm_k = mm + 0.5 * math.log(HEAD_D)
            cols = jnp.concatenate([mm_k, a, e, jnp.zeros((LANES - 6 * N_HEADS, L), F32)], axis=0).T
            mm_c, a_c, e_c = (cols[:, i * 2 * N_HEADS:(i + 1) * 2 * N_HEADS] for i in range(3))
            heads = [slice(hd * HEAD_D, (hd + 1) * HEAD_D) for hd in range(N_HEADS)]
            qs = [q_ref[r0:r0 + L, hs] for hs in heads]
            vxs = [vext_ref[r0:r0 + L, hd * 2 * HEAD_D:(hd + 1) * 2 * HEAD_D] for hd in range(N_HEADS)]
            s_bf, k_tw = [], []
            for hd, hs in enumerate(heads):
                arg = r[hd:hd + 1, :] - mm_c[:, hd:hd + 1]
                p = jnp.exp(jnp.where(causal, arg, -1e30))
                k_t = k_ref[r0:r0 + L, hs].T
                s = jnp.dot(qs[hd].astype(BF16), k_t.astype(BF16), preferred_element_type=F32) * p
                s_bf.append(s.astype(BF16))
                k_tw.append((k_t * p[L - 1:L, :]).astype(BF16))
            nds = []
            for hd in range(N_HEADS):
                cn = cn_ref[hd]
                if c == 0:
                    cn = jnp.where(first, 0.0, cn)
                lhs = jnp.concatenate([(a_c[:, hd:hd + 1] * qs[hd]).astype(BF16), s_bf[hd]], axis=1)
                rhs = jnp.concatenate([cn.astype(BF16), vxs[hd]], axis=0)
                nds.append(jnp.dot(lhs, rhs, preferred_element_type=F32))
                cn_ref[hd] = (a[hd:hd + 1, L - 1:L] * cn
                              + jnp.dot(k_tw[hd], vxs[hd], preferred_element_type=F32))
            for hd, hs in enumerate(heads):
                nd = nds[hd]
                hh = nd[:, 0:HEAD_D] / jnp.maximum(jnp.abs(nd[:, HEAD_D:HEAD_D + 1]), e_c[:, hd:hd + 1])
                hg = _sigmoid(proj_ref[r0:r0 + L, P_O + hd * HEAD_D:P_O + (hd + 1) * HEAD_D]) * hh
                hg = hg * lax.rsqrt(jnp.mean(hg * hg, axis=-1, keepdims=True) + EPS)
                hg = hg * mng_ref[:, hs]
                ya = hg * _silu(proj_ref[r0:r0 + L, P_ZA + hd * HEAD_D:P_ZA + (hd + 1) * HEAD_D])
                y_ref[r0:r0 + L, hs] = ya.astype(BF16)
        return run

    def pool():
        tg = (t_seq * T + lax.broadcasted_iota(jnp.int32, (T, LANES), 0) + 1).astype(F32)
        zero_w = jnp.zeros((LANES, LANES), BF16)
        for g0 in range(0, len(POOL_WINDOWS), 2):
            pooled = []
            for g in (g0, g0 + 1):
                win = POOL_WINDOWS[g]
                gs = slice(g * LANES, (g + 1) * LANES)
                ssum = u_ext[:, gs]
                sh = 1
                while sh < win:
                    ssum = ssum + pltpu.roll(ssum, sh, axis=0)
                    sh *= 2
                u = u_ext[POOL_HIST:POOL_HIST + T, gs]
                ssum = ssum[POOL_HIST:POOL_HIST + T]
                if win & (win - 1) == 0 and win <= POOL_HIST:
                    mean = jnp.concatenate(
                        [ssum[:POOL_HIST] / jnp.minimum(tg[:POOL_HIST], float(win)),
                         ssum[POOL_HIST:] * (1.0 / win)], axis=0)
                else:
                    mean = ssum / jnp.minimum(tg, float(win))
                pooled.append((mean - u).astype(BF16))
            w_a, w_b = pw_ref[g0].astype(BF16), pw_ref[g0 + 1].astype(BF16)
            w_pair = jnp.concatenate([jnp.concatenate([w_a, zero_w], axis=1),
                                      jnp.concatenate([zero_w, w_b], axis=1)], axis=0)
            yb = jnp.dot(jnp.concatenate(pooled, axis=1), w_pair, preferred_element_type=F32)
            yb = (yb * ps_ref[:, g0 * LANES:(g0 + 2) * LANES]
                  * _silu(proj_ref[:, P_ZB + g0 * LANES:P_ZB + (g0 + 2) * LANES]))
            y_ref[:, GROUP_W + g0 * LANES:GROUP_W + (g0 + 2) * LANES] = yb.astype(BF16)

    def attention():
        for hd in range(N_HEADS):
            hs = slice(hd * HEAD_D, (hd + 1) * HEAD_D)
            s = jnp.dot(qc_ref[:, hs], kmt_ref[hs, :], preferred_element_type=F32)
            pexp = jnp.exp(s - jnp.max(s, axis=-1, keepdims=True))
            pv = jnp.dot(pexp.astype(BF16), vmx_ref[hd], preferred_element_type=F32)
            yc = pv[:, 0:HEAD_D] / pv[:, HEAD_D:HEAD_D + 1]
            yc = yc * _silu(proj_ref[:, P_ZC + hd * HEAD_D:P_ZC + (hd + 1) * HEAD_D])
            y_ref[:, 2 * GROUP_W + hd * HEAD_D:2 * GROUP_W + (hd + 1) * HEAD_D] = yc.astype(BF16)

    def out_proj():
        rows = slice(x_row, x_row + T)
        for i in range(D_MODEL // GROUP_W):
            cs = slice(i * GROUP_W, (i + 1) * GROUP_W)
            o_ref[rows, cs] = x_ref[rows, cs] + jnp.dot(y_ref[...], wo_ref[i], preferred_element_type=F32)
        if final:
            o_ref[rows, :] = _rmsnorm(o_ref[rows, :], fng_ref[...])

    return dict(conv_q=conv_q, conv_k=conv_k, chunks=[mlstm_chunk(c) for c in range(T // L)],
                pool=pool, attention=attention, out_proj=out_proj)


def _layer_kernel(xp_ref, xn_ref, kmt_ref, vmx_ref, ng_ref, w_ref, wg_ref, bg_ref, cw_ref, cb_ref, mng_ref,
                  pw_ref, ps_ref, wo_ref, fng_ref, o_ref,
                  h_ref, qk0, u0, proj0, g0, vext0, qc0, qk1, u1, proj1, g1, vext1, qc1,
                  q_ref, k_ref, cn_ref, y_ref, m_ref, *, final, tiles_per_seq):
    T = SEQ_TILE
    k = pl.program_id(0)
    set0, set1 = (qk0, u0, proj0, g0, vext0, qc0), (qk1, u1, proj1, g1, vext1, qc1)
    a_args = (ng_ref, w_ref, wg_ref, h_ref)
    b_args = (kmt_ref, vmx_ref, bg_ref, cw_ref, cb_ref, mng_ref, pw_ref, ps_ref, wo_ref, fng_ref,
              q_ref, k_ref, cn_ref, m_ref, y_ref, final)

    @pl.when(k == 0)
    def _prologue():
        col = lax.broadcasted_iota(jnp.int32, vext0.shape, 1)
        for vext in (vext0, vext1):
            vext[...] = jnp.where(col % (2 * HEAD_D) == HEAD_D, 1.0, 0.0).astype(BF16)
        cn_ref[...] = jnp.zeros(cn_ref.shape, F32)
        m_ref[...] = jnp.zeros(m_ref.shape, F32)
        qk1[T:T + CONV_HIST, :] = jnp.zeros((CONV_HIST, 2 * GROUP_W), F32)
        u1[T:T + POOL_HIST, :] = jnp.zeros((POOL_HIST, GROUP_W), F32)
        for piece in _stage_a(xp_ref, 0, True, set1, set0, *a_args):
            piece()

    t_even = (2 * k) % tiles_per_seq
    first_even = t_even == 0
    first_next = (2 * k + 2) % tiles_per_seq == 0

    def half_step(x_row, first, t_seq, cur, nxt, xa_ref, xa_row, first_a):
        for piece in _stage_a(xa_ref, xa_row, first_a, cur, nxt, *a_args):
            piece()
        b = _stage_b(xp_ref, x_row, first, t_seq, cur, o_ref, *b_args)
        for phase in (b["conv_q"], b["conv_k"], *b["chunks"], b["pool"], b["attention"], b["out_proj"]):
            phase()

    half_step(0, first_even, t_even, set0, set1, xp_ref, T, False)
    half_step(T, False, t_even + 1, set1, set0, xn_ref, 0, first_next)


def _setup_kernel(wint_ref, wout_ref, mem_ref, g_ref, wkv_ref, wall_ref, wg_ref, wo_ref, kmt_ref, vmx_ref,
                  wb_ref):
    wall_ref[:, 0:GATE0] = wint_ref[0:GATE0, :].T.astype(BF16)
    wall_ref[:, GATE0:N_PROJ] = wint_ref[GATE0 + 2 * N_HEADS:N_IN, :].T.astype(BF16)
    wg_ref[0:2 * N_HEADS, :] = wint_ref[GATE0:GATE0 + 2 * N_HEADS, :].astype(BF16)
    wg_ref[2 * N_HEADS:GATE_ROWS, :] = jnp.zeros((GATE_ROWS - 2 * N_HEADS, wg_ref.shape[1]), BF16)
    for i in range(D_MODEL // GROUP_W):
        wo_ref[i] = wout_ref[:, i * GROUP_W:(i + 1) * GROUP_W].astype(BF16)

    @pl.when(pl.program_id(0) == 0)
    def _cast_weights():
        wb_ref[...] = wkv_ref[...].astype(BF16)

    mem = mem_ref[...]
    ms = jnp.mean(mem * mem, axis=-1, keepdims=True)
    nrm = mem * lax.rsqrt(ms + EPS)
    col = lax.broadcasted_iota(jnp.int32, (N_MEM, HEAD_D), 1)
    ones_col = jnp.where(col == 0, 1.0, 0.0).astype(BF16)
    for l in range(wb_ref.shape[0]):
        mn = (nrm * g_ref[l]).astype(BF16)
        kv = jnp.dot(mn, wb_ref[l], preferred_element_type=F32)
        for hd in range(N_HEADS):
            hs = slice(hd * HEAD_D, (hd + 1) * HEAD_D)
            kmt_ref[l, hs, :] = kv[:, hs].T.astype(BF16)
            vmx_ref[l, hd, :, 0:HEAD_D] = kv[:, GROUP_W + hd * HEAD_D:GROUP_W + (hd + 1) * HEAD_D].astype(BF16)
            vmx_ref[l, hd, :, HEAD_D:2 * HEAD_D] = ones_col


def _resident(shape, layer):
    nd = len(shape)
    return pl.BlockSpec((None,) + shape, lambda k: (layer,) + (0,) * nd, pipeline_mode=pl.Buffered(1))


def _setup(w_in, w_out, mem, mem_norm_g, w_kv):
    depth, batch = w_in.shape[0], mem.shape[0]
    assert batch % depth == 0
    steps = batch // depth
    rb_in, rb_out = D_MODEL // steps, 3 * GROUP_W // steps
    return pl.pallas_call(
        _setup_kernel,
        grid=(batch,),
        in_specs=[
            pl.BlockSpec((None, N_IN, rb_in), lambda i: (i // steps, 0, i % steps)),
            pl.BlockSpec((None, rb_out, D_MODEL), lambda i: (i // steps, i % steps, 0)),
            pl.BlockSpec((None, N_MEM, D_MODEL), lambda i: (i, 0, 0)),
            pl.BlockSpec((depth, 1, D_MODEL), lambda i: (0, 0, 0)),
            pl.BlockSpec((depth, D_MODEL, 2 * GROUP_W), lambda i: (0, 0, 0), pipeline_mode=pl.Buffered(1)),
        ],
        out_specs=[
            pl.BlockSpec((None, rb_in, N_PROJ), lambda i: (i // steps, i % steps, 0)),
            pl.BlockSpec((None, GATE_ROWS, rb_in), lambda i: (i // steps, 0, i % steps)),
            pl.BlockSpec((None, D_MODEL // GROUP_W, rb_out, GROUP_W), lambda i: (i // steps, 0, i % steps, 0)),
            pl.BlockSpec((depth, None, GROUP_W, N_MEM), lambda i: (0, i, 0, 0)),
            pl.BlockSpec((depth, None, N_HEADS, N_MEM, 2 * HEAD_D), lambda i: (0, i, 0, 0, 0)),
        ],
        out_shape=[
            jax.ShapeDtypeStruct((depth, D_MODEL, N_PROJ), BF16),
            jax.ShapeDtypeStruct((depth, GATE_ROWS, D_MODEL), BF16),
            jax.ShapeDtypeStruct((depth, D_MODEL // GROUP_W, 3 * GROUP_W, GROUP_W), BF16),
            jax.ShapeDtypeStruct((depth, batch, GROUP_W, N_MEM), BF16),
            jax.ShapeDtypeStruct((depth, batch, N_HEADS, N_MEM, 2 * HEAD_D), BF16),
        ],
        scratch_shapes=[pltpu.VMEM((depth, D_MODEL, 2 * GROUP_W), BF16)],
        compiler_params=pltpu.CompilerParams(
            dimension_semantics=("arbitrary",), vmem_limit_bytes=VMEM_LIMIT_BYTES),
        name="setup_weights_memkv",
    )(jnp.swapaxes(w_in, 1, 2), w_out, mem, mem_norm_g.reshape(depth, 1, D_MODEL), w_kv)


def _layer(layer, final, x, kmt, vmx, ng, w_all, w_g, b_g, conv_w, conv_b, mng, pool_w, pool_scale, w_out, fng):
    batch, seq, _ = x.shape
    T = SEQ_TILE
    assert seq % (2 * T) == 0 and T % CHUNK == 0
    tiles_per_seq = seq // T
    pairs_per_seq = tiles_per_seq // 2
    last_tile = batch * tiles_per_seq - 1

    def next_tile(k):
        nt = jnp.minimum(2 * k + 2, last_tile)
        return nt // tiles_per_seq, nt % tiles_per_seq, 0

    kern = functools.partial(_layer_kernel, final=final, tiles_per_seq=tiles_per_seq)
    set_scratch = [
        pltpu.VMEM((CONV_HIST + T, 2 * GROUP_W), F32),
        pltpu.VMEM((POOL_HIST + T, GROUP_W), F32),
        pltpu.VMEM((T, P_COLS), F32),
        pltpu.VMEM((GATE_ROWS, T), F32),
        pltpu.VMEM((T, N_HEADS * 2 * HEAD_D), BF16),
        pltpu.VMEM((T, GROUP_W), BF16),
    ]
    return pl.pallas_call(
        kern,
        grid=(batch * pairs_per_seq,),
        in_specs=[
            pl.BlockSpec((None, 2 * T, D_MODEL), lambda k: (k // pairs_per_seq, k % pairs_per_seq, 0)),
            pl.BlockSpec((None, T, D_MODEL), next_tile),
            pl.BlockSpec((None, None, GROUP_W, N_MEM), lambda k: (layer, k // pairs_per_seq, 0, 0)),
            pl.BlockSpec((None, None, N_HEADS, N_MEM, 2 * HEAD_D),
                         lambda k: (layer, k // pairs_per_seq, 0, 0, 0)),
            _resident((1, D_MODEL), layer),
            _resident((D_MODEL, N_PROJ), layer),
            _resident((GATE_ROWS, D_MODEL), layer),
            _resident((2 * N_HEADS, LANES), layer),
            _resident((CONV_K, 2 * GROUP_W), layer),
            _resident((1, 2 * GROUP_W), layer),
            _resident((1, GROUP_W), layer),
            _resident((len(POOL_WINDOWS), LANES, LANES), layer),
            _resident((1, GROUP_W), layer),
            _resident((D_MODEL // GROUP_W, 3 * GROUP_W, GROUP_W), layer),
            pl.BlockSpec((1, D_MODEL), lambda k: (0, 0), pipeline_mode=pl.Buffered(1)),
        ],
        out_specs=pl.BlockSpec((None, 2 * T, D_MODEL), lambda k: (k // pairs_per_seq, k % pairs_per_seq, 0)),
        out_shape=jax.ShapeDtypeStruct(x.shape, F32),
        scratch_shapes=[pltpu.VMEM((T, D_MODEL), BF16)] + set_scratch + set_scratch + [
            pltpu.VMEM((T, GROUP_W), F32),
            pltpu.VMEM((T, GROUP_W), F32),
            pltpu.VMEM((N_HEADS, HEAD_D, 2 * HEAD_D), F32),
            pltpu.VMEM((T, 3 * GROUP_W), BF16),
            pltpu.VMEM((2 * N_HEADS, LANES), F32),
        ],
        compiler_params=pltpu.CompilerParams(
            dimension_semantics=("arbitrary",), vmem_limit_bytes=VMEM_LIMIT_BYTES),
        name="trunk_layer_final" if final else "trunk_layer",
    )(x, x, kmt, vmx, ng, w_all, w_g, b_g, conv_w, conv_b, mng, pool_w, pool_scale, w_out, fng)


def kernel(x, mem, norm_g, w_in, b_gates, conv_w, conv_b, mlstm_norm_g, pool_w, pool_scale,
           mem_norm_g, w_mem_kv, w_out, final_norm_g):
    depth = w_in.shape[0]
    w_all, w_g, w_out_b, kmt, vmx = _setup(w_in, w_out, mem, mem_norm_g, w_mem_kv)
    b_g = jnp.broadcast_to(b_gates[:, :, None], (depth, 2 * N_HEADS, LANES))
    args = (kmt, vmx, norm_g.reshape(depth, 1, D_MODEL), w_all, w_g, b_g,
            conv_w, conv_b.reshape(depth, 1, 2 * GROUP_W), mlstm_norm_g.reshape(depth, 1, GROUP_W),
            pool_w, pool_scale.reshape(depth, 1, GROUP_W), w_out_b, final_norm_g.reshape(1, D_MODEL))
    for l in range(depth):
        x = _layer(l, l == depth - 1, x, *args)
    return x
```

```python
import functools
import math

import jax
import jax.numpy as jnp
from jax import lax
from jax.experimental import pallas as pl
from jax.experimental.pallas import tpu as pltpu

D_MODEL = 1024
GROUP_W = 512
N_HEADS = 4
HEAD_D = 128
CONV_K = 4
POOL_WINDOWS = (2, 4, 8, 16)
N_MEM = 256
EPS = 1e-6

SEQ_TILE = 256
CHUNK = 256
CONV_HIST = 8
CONV_ROWS = 64
POOL_HIST = 16
LANES = 128
VMEM_LIMIT_BYTES = 52 * 1024 * 1024

N_IN = 2 * GROUP_W + 3 * GROUP_W + 2 * N_HEADS + 4 * GROUP_W
GATE0 = 5 * GROUP_W
Q0, K0, V0, O0, ZA0, U0, ZB0, QC0, ZC0 = (i * GROUP_W for i in range(9))
N_PROJ = 9 * GROUP_W
GATE_ROWS = 16
P_O, P_ZA, P_ZB, P_ZC = (i * GROUP_W for i in range(4))
P_COLS = 4 * GROUP_W

BF16 = jnp.bfloat16
F32 = jnp.float32


def _sigmoid(x):
    return 0.5 * jnp.tanh(0.5 * x) + 0.5


def _silu(x):
    hx = 0.5 * x
    return hx + hx * jnp.tanh(hx)


def _log_sigmoid(x):
    return jnp.minimum(x, 0.0) - jnp.log1p(jnp.exp(-jnp.abs(x)))


def _scan_lanes(x, op, identity):
    n = x.shape[1]
    lanes = lax.broadcasted_iota(jnp.int32, x.shape, 1)
    k = 1
    while k < n:
        x = op(x, jnp.where(lanes >= k, pltpu.roll(x, k, axis=1), identity))
        k *= 2
    return x


def _rmsnorm(x, g):
    ms = jnp.mean(x * x, axis=-1, keepdims=True)
    return (x * lax.rsqrt(ms + EPS)) * g


def _stage_a(x_ref, x_row, first, prev, cur, ng_ref, w_ref, wg_ref, h_ref):
    T = SEQ_TILE
    qk_prev, u_prev = prev[0], prev[1]
    qk_cur, u_cur, proj_cur, g_cur, vext_cur, qc_cur = cur

    def head():
        qk_cur[0:CONV_HIST, :] = jnp.where(first, 0.0, qk_prev[T:T + CONV_HIST, :])
        u_cur[0:POOL_HIST, :] = jnp.where(first, 0.0, u_prev[T:T + POOL_HIST, :])
        h_ref[...] = _rmsnorm(x_ref[x_row:x_row + T, :], ng_ref[...]).astype(BF16)

    def proj(c0):
        return jnp.dot(h_ref[...], w_ref[:, c0:c0 + GROUP_W], preferred_element_type=F32)

    def dot_q():
        qk_cur[CONV_HIST:CONV_HIST + T, 0:GROUP_W] = proj(Q0)

    def dot_k():
        qk_cur[CONV_HIST:CONV_HIST + T, GROUP_W:2 * GROUP_W] = proj(K0)

    def dot_u():
        u_cur[POOL_HIST:POOL_HIST + T, :] = proj(U0)

    def dot_to(src, dst):
        def run():
            proj_cur[:, dst:dst + GROUP_W] = proj(src)
        return run

    def dot_v():
        res = proj(V0)
        for hd in range(N_HEADS):
            vext_cur[:, hd * 2 * HEAD_D:hd * 2 * HEAD_D + HEAD_D] = (
                res[:, hd * HEAD_D:(hd + 1) * HEAD_D].astype(BF16))

    def dot_qc():
        qc_cur[...] = (proj(QC0) * (HEAD_D ** -0.5)).astype(BF16)

    def dot_gates():
        g_cur[...] = lax.dot_general(wg_ref[...], h_ref[...], (((1,), (1,)), ((), ())),
                                     preferred_element_type=F32)

    return [head, dot_q, dot_k, dot_u, dot_v, dot_to(O0, P_O), dot_to(ZA0, P_ZA),
            dot_to(ZB0, P_ZB), dot_qc, dot_to(ZC0, P_ZC), dot_gates]


def _stage_b(x_ref, x_row, first, t_seq, cur, o_ref, kmt_ref, vmx_ref, bg_ref, cw_ref, cb_ref, mng_ref,
             pw_ref, ps_ref, wo_ref, fng_ref, q_ref, k_ref, cn_ref, m_ref, y_ref, final):
    T, L = SEQ_TILE, CHUNK
    qk_ext, u_ext, proj_ref, g_ref, vext_ref, qc_ref = cur

    def conv(c0, dst_ref):
        for cb0 in range(0, GROUP_W, LANES):
            cs = slice(c0 + cb0, c0 + cb0 + LANES)
            for r0 in range(0, T, CONV_ROWS):
                xs = qk_ext[r0:r0 + CONV_HIST + CONV_ROWS, cs]
                acc = cb_ref[:, cs] + cw_ref[CONV_K - 1:CONV_K, cs] * xs[CONV_HIST:]
                for j in range(CONV_K - 1):
                    back = pltpu.roll(xs, CONV_K - 1 - j, axis=0)[CONV_HIST:]
                    acc = acc + cw_ref[j:j + 1, cs] * back
                dst_ref[r0:r0 + CONV_ROWS, cb0:cb0 + LANES] = _silu(acc)

    def conv_q():
        conv(0, q_ref)

    def conv_k():
        conv(GROUP_W, k_ref)

    def mlstm_chunk(c):
        def run():
            rowi = lax.broadcasted_iota(jnp.int32, (L, L), 0)
            coli = lax.broadcasted_iota(jnp.int32, (L, L), 1)
            causal = coli <= rowi
            r0 = c * L
            g8 = g_ref[0:2 * N_HEADS, r0:r0 + L] + bg_ref[:, 0:1]
            gi = g8
            logf = _log_sigmoid(pltpu.roll(g8, N_HEADS, axis=0))
            b = _scan_lanes(logf, jnp.add, 0.0)
            r = gi - b
            m_prev = m_ref[:, 0:1]
            if c == 0:
                m_prev = jnp.where(first, 0.0, m_prev)
            mm = jnp.maximum(_scan_lanes(r, jnp.maximum, -jnp.inf), m_prev)
            a = jnp.exp(m_prev - mm)
            e = jnp.exp(-(mm + b))
            m_ref[...] = jnp.broadcast_to((mm + b)[:, L - 1:L], m_ref.shape)
            mm_k = mm + 0.5 * math.log(HEAD_D)
            cols = jnp.concatenate([mm_k, a, e, jnp.zeros((LANES - 6 * N_HEADS, L), F32)], axis=0).T
            mm_c, a_c, e_c = (cols[:, i * 2 * N_HEADS:(i + 1) * 2 * N_HEADS] for i in range(3))
            heads = [slice(hd * HEAD_D, (hd + 1) * HEAD_D) for hd in range(N_HEADS)]
            qs = [q_ref[r0:r0 + L, hs] for hs in heads]
            vxs = [vext_ref[r0:r0 + L, hd * 2 * HEAD_D:(hd + 1) * 2 * HEAD_D] for hd in range(N_HEADS)]
            s_bf, k_tw = [], []
            for hd, hs in enumerate(heads):
                arg = r[hd:hd + 1, :] - mm_c[:, hd:hd + 1]
                p = jnp.exp(jnp.where(causal, arg, -1e30))
                k_t = k_ref[r0:r0 + L, hs].T
                s = jnp.dot(qs[hd].astype(BF16), k_t.astype(BF16), preferred_element_type=F32) * p
                s_bf.append(s.astype(BF16))
                k_tw.append((k_t * p[L - 1:L, :]).astype(BF16))
            nds = []
            for hd in range(N_HEADS):
                cn = cn_ref[hd]
                if c == 0:
                    cn = jnp.where(first, 0.0, cn)
                lhs = jnp.concatenate([(a_c[:, hd:hd + 1] * qs[hd]).astype(BF16), s_bf[hd]], axis=1)
                rhs = jnp.concatenate([cn.astype(BF16), vxs[hd]], axis=0)
                nds.append(jnp.dot(lhs, rhs, preferred_element_type=F32))
                cn_ref[hd] = (a[hd:hd + 1, L - 1:L] * cn
                              + jnp.dot(k_tw[hd], vxs[hd], preferred_element_type=F32))
            for hd, hs in enumerate(heads):
                nd = nds[hd]
                hh = nd[:, 0:HEAD_D] / jnp.maximum(jnp.abs(nd[:, HEAD_D:HEAD_D + 1]), e_c[:, hd:hd + 1])
                hg = _sigmoid(proj_ref[r0:r0 + L, P_O + hd * HEAD_D:P_O + (hd + 1) * HEAD_D]) * hh
                hg = hg * lax.rsqrt(jnp.mean(hg * hg, axis=-1, keepdims=True) + EPS)
                hg = hg * mng_ref[:, hs]
                ya = hg * _silu(proj_ref[r0:r0 + L, P_ZA + hd * HEAD_D:P_ZA + (hd + 1) * HEAD_D])
                y_ref[r0:r0 + L, hs] = ya.astype(BF16)
        return run

    def pool():
        tg = (t_seq * T + lax.broadcasted_iota(jnp.int32, (T, LANES), 0) + 1).astype(F32)
        zero_w = jnp.zeros((LANES, LANES), BF16)
        for g0 in range(0, len(POOL_WINDOWS), 2):
            pooled = []
            for g in (g0, g0 + 1):
                win = POOL_WINDOWS[g]
                gs = slice(g * LANES, (g + 1) * LANES)
                parts = []
                for r0 in range(0, T, CONV_ROWS):
                    blk = u_ext[r0:r0 + POOL_HIST + CONV_ROWS, gs]
                    ssum = blk
                    sh = 1
                    while sh < win:
                        ssum = ssum + pltpu.roll(ssum, sh, axis=0)
                        sh *= 2
                    u = blk[POOL_HIST:]
                    ssum = ssum[POOL_HIST:]
                    tgb = tg[r0:r0 + CONV_ROWS]
                    if win & (win - 1) == 0 and win <= POOL_HIST:
                        if r0 == 0:
                            mean = jnp.concatenate(
                                [ssum[:POOL_HIST] / jnp.minimum(tgb[:POOL_HIST], float(win)),
                                 ssum[POOL_HIST:] * (1.0 / win)], axis=0)
                        else:
                            mean = ssum * (1.0 / win)
                    else:
                        mean = ssum / jnp.minimum(tgb, float(win))
                    parts.append((mean - u).astype(BF16))
                pooled.append(jnp.concatenate(parts, axis=0))
            w_a, w_b = pw_ref[g0].astype(BF16), pw_ref[g0 + 1].astype(BF16)
            w_pair = jnp.concatenate([jnp.concatenate([w_a, zero_w], axis=1),
                                      jnp.concatenate([zero_w, w_b], axis=1)], axis=0)
            yb = jnp.dot(jnp.concatenate(pooled, axis=1), w_pair, preferred_element_type=F32)
            yb = (yb * ps_ref[:, g0 * LANES:(g0 + 2) * LANES]
                  * _silu(proj_ref[:, P_ZB + g0 * LANES:P_ZB + (g0 + 2) * LANES]))
            y_ref[:, GROUP_W + g0 * LANES:GROUP_W + (g0 + 2) * LANES] = yb.astype(BF16)

    def attention():
        for hd in range(N_HEADS):
            hs = slice(hd * HEAD_D, (hd + 1) * HEAD_D)
            s = jnp.dot(qc_ref[:, hs], kmt_ref[hs, :], preferred_element_type=F32)
            pexp = jnp.exp(s - jnp.max(s, axis=-1, keepdims=True))
            pv = jnp.dot(pexp.astype(BF16), vmx_ref[hd], preferred_element_type=F32)
            yc = pv[:, 0:HEAD_D] / pv[:, HEAD_D:HEAD_D + 1]
            yc = yc * _silu(proj_ref[:, P_ZC + hd * HEAD_D:P_ZC + (hd + 1) * HEAD_D])
            y_ref[:, 2 * GROUP_W + hd * HEAD_D:2 * GROUP_W + (hd + 1) * HEAD_D] = yc.astype(BF16)

    def out_proj():
        rows = slice(x_row, x_row + T)
        for i in range(D_MODEL // GROUP_W):
            cs = slice(i * GROUP_W, (i + 1) * GROUP_W)
            o_ref[rows, cs] = x_ref[rows, cs] + jnp.dot(y_ref[...], wo_ref[i], preferred_element_type=F32)
        if final:
            o_ref[rows, :] = _rmsnorm(o_ref[rows, :], fng_ref[...])

    return dict(conv_q=conv_q, conv_k=conv_k, chunks=[mlstm_chunk(c) for c in range(T // L)],
                pool=pool, attention=attention, out_proj=out_proj)


def _layer_kernel(xp_ref, xn_ref, kmt_ref, vmx_ref, ng_ref, w_ref, wg_ref, bg_ref, cw_ref, cb_ref, mng_ref,
                  pw_ref, ps_ref, wo_ref, fng_ref, o_ref,
                  h_ref, qk0, u0, proj0, g0, vext0, qc0, qk1, u1, proj1, g1, vext1, qc1,
                  q_ref, k_ref, cn_ref, y_ref, m_ref, *, final, tiles_per_seq):
    T = SEQ_TILE
    k = pl.program_id(0)
    set0, set1 = (qk0, u0, proj0, g0, vext0, qc0), (qk1, u1, proj1, g1, vext1, qc1)
    a_args = (ng_ref, w_ref, wg_ref, h_ref)
    b_args = (kmt_ref, vmx_ref, bg_ref, cw_ref, cb_ref, mng_ref, pw_ref, ps_ref, wo_ref, fng_ref,
              q_ref, k_ref, cn_ref, m_ref, y_ref, final)

    @pl.when(k == 0)
    def _prologue():
        col = lax.broadcasted_iota(jnp.int32, vext0.shape, 1)
        for vext in (vext0, vext1):
            vext[...] = jnp.where(col % (2 * HEAD_D) == HEAD_D, 1.0, 0.0).astype(BF16)
        cn_ref[...] = jnp.zeros(cn_ref.shape, F32)
        m_ref[...] = jnp.zeros(m_ref.shape, F32)
        qk1[T:T + CONV_HIST, :] = jnp.zeros((CONV_HIST, 2 * GROUP_W), F32)
        u1[T:T + POOL_HIST, :] = jnp.zeros((POOL_HIST, GROUP_W), F32)
        for piece in _stage_a(xp_ref, 0, True, set1, set0, *a_args):
            piece()

    t_even = (2 * k) % tiles_per_seq
    first_even = t_even == 0
    first_next = (2 * k + 2) % tiles_per_seq == 0

    def half_step(x_row, first, t_seq, cur, nxt, xa_ref, xa_row, first_a):
        for piece in _stage_a(xa_ref, xa_row, first_a, cur, nxt, *a_args):
            piece()
        b = _stage_b(xp_ref, x_row, first, t_seq, cur, o_ref, *b_args)
        for phase in (b["conv_q"], b["conv_k"], *b["chunks"], b["pool"], b["attention"], b["out_proj"]):
            phase()

    half_step(0, first_even, t_even, set0, set1, xp_ref, T, False)
    half_step(T, False, t_even + 1, set1, set0, xn_ref, 0, first_next)


def _setup_kernel(wint_ref, wout_ref, mem_ref, g_ref, wkv_ref, wall_ref, wg_ref, wo_ref, kmt_ref, vmx_ref,
                  wb_ref):
    wall_ref[:, 0:GATE0] = wint_ref[0:GATE0, :].T.astype(BF16)
    wall_ref[:, GATE0:N_PROJ] = wint_ref[GATE0 + 2 * N_HEADS:N_IN, :].T.astype(BF16)
    wg_ref[0:2 * N_HEADS, :] = wint_ref[GATE0:GATE0 + 2 * N_HEADS, :].astype(BF16)
    wg_ref[2 * N_HEADS:GATE_ROWS, :] = jnp.zeros((GATE_ROWS - 2 * N_HEADS, wg_ref.shape[1]), BF16)
    for i in range(D_MODEL // GROUP_W):
        wo_ref[i] = wout_ref[:, i * GROUP_W:(i + 1) * GROUP_W].astype(BF16)

    @pl.when(pl.program_id(0) == 0)
    def _cast_weights():
        wb_ref[...] = wkv_ref[...].astype(BF16)

    mem = mem_ref[...]
    ms = jnp.mean(mem * mem, axis=-1, keepdims=True)
    nrm = mem * lax.rsqrt(ms + EPS)
    col = lax.broadcasted_iota(jnp.int32, (N_MEM, HEAD_D), 1)
    ones_col = jnp.where(col == 0, 1.0, 0.0).astype(BF16)
    for l in range(wb_ref.shape[0]):
        mn = (nrm * g_ref[l]).astype(BF16)
        kv = jnp.dot(mn, wb_ref[l], preferred_element_type=F32)
        for hd in range(N_HEADS):
            hs = slice(hd * HEAD_D, (hd + 1) * HEAD_D)
            kmt_ref[l, hs, :] = kv[:, hs].T.astype(BF16)
            vmx_ref[l, hd, :, 0:HEAD_D] = kv[:, GROUP_W + hd * HEAD_D:GROUP_W + (hd + 1) * HEAD_D].astype(BF16)
            vmx_ref[l, hd, :, HEAD_D:2 * HEAD_D] = ones_col


def _resident(shape, layer):
    nd = len(shape)
    return pl.BlockSpec((None,) + shape, lambda k: (layer,) + (0,) * nd, pipeline_mode=pl.Buffered(1))


def _setup(w_in, w_out, mem, mem_norm_g, w_kv):
    depth, batch = w_in.shape[0], mem.shape[0]
    assert batch % depth == 0
    steps = batch // depth
    rb_in, rb_out = D_MODEL // steps, 3 * GROUP_W // steps
    return pl.pallas_call(
        _setup_kernel,
        grid=(batch,),
        in_specs=[
            pl.BlockSpec((None, N_IN, rb_in), lambda i: (i // steps, 0, i % steps)),
            pl.BlockSpec((None, rb_out, D_MODEL), lambda i: (i // steps, i % steps, 0)),
            pl.BlockSpec((None, N_MEM, D_MODEL), lambda i: (i, 0, 0)),
            pl.BlockSpec((depth, 1, D_MODEL), lambda i: (0, 0, 0)),
            pl.BlockSpec((depth, D_MODEL, 2 * GROUP_W), lambda i: (0, 0, 0), pipeline_mode=pl.Buffered(1)),
        ],
        out_specs=[
            pl.BlockSpec((None, rb_in, N_PROJ), lambda i: (i // steps, i % steps, 0)),
            pl.BlockSpec((None, GATE_ROWS, rb_in), lambda i: (i // steps, 0, i % steps)),
            pl.BlockSpec((None, D_MODEL // GROUP_W, rb_out, GROUP_W), lambda i: (i // steps, 0, i % steps, 0)),
            pl.BlockSpec((depth, None, GROUP_W, N_MEM), lambda i: (0, i, 0, 0)),
            pl.BlockSpec((depth, None, N_HEADS, N_MEM, 2 * HEAD_D), lambda i: (0, i, 0, 0, 0)),
        ],
        out_shape=[
            jax.ShapeDtypeStruct((depth, D_MODEL, N_PROJ), BF16),
            jax.ShapeDtypeStruct((depth, GATE_ROWS, D_MODEL), BF16),
            jax.ShapeDtypeStruct((depth, D_MODEL // GROUP_W, 3 * GROUP_W, GROUP_W), BF16),
            jax.ShapeDtypeStruct((depth, batch, GROUP_W, N_MEM), BF16),
            jax.ShapeDtypeStruct((depth, batch, N_HEADS, N_MEM, 2 * HEAD_D), BF16),
        ],
        scratch_shapes=[pltpu.VMEM((depth, D_MODEL, 2 * GROUP_W), BF16)],
        compiler_params=pltpu.CompilerParams(
            dimension_semantics=("arbitrary",), vmem_limit_bytes=VMEM_LIMIT_BYTES),
        name="setup_weights_memkv",
    )(jnp.swapaxes(w_in, 1, 2), w_out, mem, mem_norm_g.reshape(depth, 1, D_MODEL), w_kv)


def _layer(layer, final, x, kmt, vmx, ng, w_all, w_g, b_g, conv_w, conv_b, mng, pool_w, pool_scale, w_out, fng):
    batch, seq, _ = x.shape
    T = SEQ_TILE
    assert seq % (2 * T) == 0 and T % CHUNK == 0
    tiles_per_seq = seq // T
    pairs_per_seq = tiles_per_seq // 2
    last_tile = batch * tiles_per_seq - 1

    def next_tile(k):
        nt = jnp.minimum(2 * k + 2, last_tile)
        return nt // tiles_per_seq, nt % tiles_per_seq, 0

    kern = functools.partial(_layer_kernel, final=final, tiles_per_seq=tiles_per_seq)
    set_scratch = [
        pltpu.VMEM((CONV_HIST + T, 2 * GROUP_W), F32),
        pltpu.VMEM((POOL_HIST + T, GROUP_W), F32),
        pltpu.VMEM((T, P_COLS), F32),
        pltpu.VMEM((GATE_ROWS, T), F32),
        pltpu.VMEM((T, N_HEADS * 2 * HEAD_D), BF16),
        pltpu.VMEM((T, GROUP_W), BF16),
    ]
    return pl.pallas_call(
        kern,
        grid=(batch * pairs_per_seq,),
        in_specs=[
            pl.BlockSpec((None, 2 * T, D_MODEL), lambda k: (k // pairs_per_seq, k % pairs_per_seq, 0)),
            pl.BlockSpec((None, T, D_MODEL), next_tile),
            pl.BlockSpec((None, None, GROUP_W, N_MEM), lambda k: (layer, k // pairs_per_seq, 0, 0)),
            pl.BlockSpec((None, None, N_HEADS, N_MEM, 2 * HEAD_D),
                         lambda k: (layer, k // pairs_per_seq, 0, 0, 0)),
            _resident((1, D_MODEL), layer),
            _resident((D_MODEL, N_PROJ), layer),
            _resident((GATE_ROWS, D_MODEL), layer),
            _resident((2 * N_HEADS, LANES), layer),
            _resident((CONV_K, 2 * GROUP_W), layer),
            _resident((1, 2 * GROUP_W), layer),
            _resident((1, GROUP_W), layer),
            _resident((len(POOL_WINDOWS), LANES, LANES), layer),
            _resident((1, GROUP_W), layer),
            _resident((D_MODEL // GROUP_W, 3 * GROUP_W, GROUP_W), layer),
            pl.BlockSpec((1, D_MODEL), lambda k: (0, 0), pipeline_mode=pl.Buffered(1)),
        ],
        out_specs=pl.BlockSpec((None, 2 * T, D_MODEL), lambda k: (k // pairs_per_seq, k % pairs_per_seq, 0)),
        out_shape=jax.ShapeDtypeStruct(x.shape, F32),
        scratch_shapes=[pltpu.VMEM((T, D_MODEL), BF16)] + set_scratch + set_scratch + [
            pltpu.VMEM((T, GROUP_W), F32),
            pltpu.VMEM((T, GROUP_W), F32),
            pltpu.VMEM((N_HEADS, HEAD_D, 2 * HEAD_D), F32),
            pltpu.VMEM((T, 3 * GROUP_W), BF16),
            pltpu.VMEM((2 * N_HEADS, LANES), F32),
        ],
        compiler_params=pltpu.CompilerParams(
            dimension_semantics=("arbitrary",), vmem_limit_bytes=VMEM_LIMIT_BYTES),
        name="trunk_layer_final" if final else "trunk_layer",
    )(x, x, kmt, vmx, ng, w_all, w_g, b_g, conv_w, conv_b, mng, pool_w, pool_scale, w_out, fng)


def kernel(x, mem, norm_g, w_in, b_gates, conv_w, conv_b, mlstm_norm_g, pool_w, pool_scale,
           mem_norm_g, w_mem_kv, w_out, final_norm_g):
    depth = w_in.shape[0]
    w_all, w_g, w_out_b, kmt, vmx = _setup(w_in, w_out, mem, mem_norm_g, w_mem_kv)
    b_g = jnp.broadcast_to(b_gates[:, :, None], (depth, 2 * N_HEADS, LANES))
    args = (kmt, vmx, norm_g.reshape(depth, 1, D_MODEL), w_all, w_g, b_g,
            conv_w, conv_b.reshape(depth, 1, 2 * GROUP_W), mlstm_norm_g.reshape(depth, 1, GROUP_W),
            pool_w, pool_scale.reshape(depth, 1, GROUP_W), w_out_b, final_norm_g.reshape(1, D_MODEL))
    for l in range(depth):
        x = _layer(l, l == depth - 1, x, *args)
    return x
```
